```python
import jax
import jax.numpy as jnp
from jax import lax
import numpy as np

D_MODEL = 1024
BATCH = 4
SEQ = 8192
DEPTH = 4
DEC_BATCH = 8
DEC_SEQ = 64
PAST_LEN = 4096

CHUNK = 64
A_HEADS = 8
A_HEAD_DIM = 64
A_WIDTH = A_HEADS * A_HEAD_DIM
A_PAST_CHUNKS = 8
A_WINDOW = A_PAST_CHUNKS * CHUNK
A_MAX_REL = 256
B_HEADS = 8
B_KEY_DIM = 64
B_VAL_DIM = 64
B_WIDTH = B_HEADS * B_KEY_DIM
C_WIDTH = 512
C_BLOCKS = 8
C_BLOCK = C_WIDTH // C_BLOCKS
C_CONV = 4
C_GATE_C = 8.0
N_BRANCH = 3
BRANCH_WIDTH = 512
IN_COLS = 3 * A_WIDTH + 4 * B_WIDTH + 2 * C_WIDTH + N_BRANCH * D_MODEL
M_TOKENS = 256
X_HEADS = 4
X_HEAD_DIM = D_MODEL // X_HEADS
N_EXPERTS = 16
N_GROUPS = 4
TOP_K = 2
D_EXPERT = D_MODEL // 2
DN_ALPHA = (2 * DEPTH) ** 0.25
DN_BETA = (8 * DEPTH) ** -0.25
LN_EPS = 1e-5
RMS_EPS = 1e-6
NEG_INF = -1e30

kernel_name = 'hybrid_streaming_encoder_step'


def layer_norm(x, g, b):
    x32 = x.astype(jnp.float32)
    mu = jnp.mean(x32, axis=-1, keepdims=True)
    var = jnp.mean(jnp.square(x32 - mu), axis=-1, keepdims=True)
    y = (x32 - mu) * lax.rsqrt(var + LN_EPS) * g.astype(jnp.float32) + b.astype(jnp.float32)
    return y.astype(x.dtype)


def rel_bias(table, q_off, k_off):
    dist = jnp.clip(q_off[:, None] - k_off[None, :], -A_MAX_REL, A_MAX_REL) + A_MAX_REL
    return table.astype(jnp.float32)[:, dist]


def band_attention(q, k, v, bias, valid):
    s = jnp.einsum('...qhd,...khd->...hqk', q, k).astype(jnp.float32) * (A_HEAD_DIM ** -0.5) + bias
    s = jnp.where(valid, s, NEG_INF)
    p = jax.nn.softmax(s, axis=-1).astype(v.dtype)
    return jnp.einsum('...hqk,...khd->...qhd', p, v)


def chunk_attention_prompt(q, k, v, table):
    _, T, H, dh = q.shape
    nc = T // CHUNK
    band = (A_PAST_CHUNKS + 1) * CHUNK
    k_off = jnp.arange(band) - A_WINDOW
    bias = rel_bias(table, jnp.arange(CHUNK), k_off)
    key_chunk = jnp.arange(nc)[:, None] - A_PAST_CHUNKS + (jnp.arange(band) // CHUNK)[None, :]
    valid = (key_chunk >= 0)[:, None, None, :]

    def gather_band(t):
        tc = t.reshape(nc, CHUNK, H, dh)
        tp = jnp.concatenate([jnp.zeros((A_PAST_CHUNKS, CHUNK, H, dh), t.dtype), tc], axis=0)
        return jnp.concatenate([tp[j:j + nc] for j in range(A_PAST_CHUNKS + 1)], axis=1)

    def one_stream(qkv):
        qs, ks, vs = qkv
        out = band_attention(qs.reshape(nc, CHUNK, H, dh), gather_band(ks), gather_band(vs), bias, valid)
        return out.reshape(T, H * dh)

    return lax.map(one_stream, (q, k, v))


def chunk_attention_sample(q, k, v, cache_k, cache_v, table):
    Bd, T, H, dh = q.shape
    W = cache_k.shape[1]
    kk = jnp.concatenate([cache_k.astype(k.dtype), k], axis=1)
    vv = jnp.concatenate([cache_v.astype(v.dtype), v], axis=1)
    k_off = jnp.concatenate([jnp.arange(W) - W, jnp.arange(T)])
    bias = rel_bias(table, jnp.arange(T), k_off)
    out = band_attention(q, kk, vv, bias, True)
    return out.reshape(Bd, T, H * dh), kk[:, T:], vv[:, T:]


def hgrn2_branch(q, f_logit, v_in, g, lb, norm_g, s0, block):
    B, T, _ = q.shape
    n = T // block
    f32 = jnp.float32
    f = lb + (1.0 - lb) * jax.nn.sigmoid(f_logit.astype(f32))
    log_f = jnp.log(f)
    k = 1.0 - f

    def heads(t):
        return t.astype(f32).reshape(B, n, block, B_HEADS, -1).transpose(1, 0, 3, 2, 4)

    causal = jnp.tril(jnp.ones((block, block), dtype=bool))[:, :, None]

    def step(s, blk):
        qb, gb, kb, vb = blk
        G = jnp.cumsum(gb, axis=2)
        o_inter = jnp.einsum('bhtc,bhcv->bhtv', qb * jnp.exp(G), s)
        decay = jnp.exp(jnp.where(causal, G[:, :, :, None, :] - G[:, :, None, :, :], -jnp.inf))
        att = jnp.einsum('bhtc,bhsc,bhtsc->bhts', qb, kb, decay)
        o = o_inter + jnp.einsum('bhts,bhsv->bhtv', att, vb)
        GL = G[:, :, -1:, :]
        s = jnp.exp(GL[:, :, 0, :, None]) * s + jnp.einsum('bhsc,bhsv->bhcv', kb * jnp.exp(GL - G), vb)
        return s, o

    s_fin, o = lax.scan(step, s0.astype(f32), (heads(q), heads(log_f), heads(k), heads(v_in)))
    o = o.transpose(1, 0, 3, 2, 4).reshape(B, T, B_HEADS, B_VAL_DIM)
    o = o * lax.rsqrt(jnp.mean(jnp.square(o), axis=-1, keepdims=True) + RMS_EPS) * norm_g.astype(f32)
    o = o.reshape(B, T, B_WIDTH) * jax.nn.silu(g.astype(f32))
    return o.astype(q.dtype), s_fin


def rglru_branch(xr, gate, conv_w, conv_b, wa, ba, wx, bx, lam, conv0, h0, at_start):
    B, T, C = xr.shape
    f32 = jnp.float32
    xp = jnp.concatenate([conv0.astype(xr.dtype), xr], axis=1)
    xc = conv_b + sum(conv_w[j] * xp[:, j:j + T] for j in range(C_CONV))
    conv_new = xp[:, T:]
    xb = xc.reshape(B, T, C_BLOCKS, C_BLOCK)
    r = jax.nn.sigmoid((jnp.einsum('btnd,nde->btne', xb, wa).reshape(B, T, C) + ba).astype(f32))
    ig = jax.nn.sigmoid((jnp.einsum('btnd,nde->btne', xb, wx).reshape(B, T, C) + bx).astype(f32))
    log_a = -C_GATE_C * r * jax.nn.softplus(-lam.astype(f32))
    a = jnp.exp(log_a)
    mult = jnp.sqrt(-jnp.expm1(2.0 * log_a))
    if at_start:
        mult = mult.at[:, 0].set(1.0)
    b = mult * ig * xc.astype(f32)
    b = b.at[:, 0].add(a[:, 0] * h0.astype(f32))

    def combine(left, right):
        a_l, b_l = left
        a_r, b_r = right
        return a_l * a_r, a_r * b_l + b_r

    _, h = lax.associative_scan(combine, (a, b), axis=1)
    out = (h * jax.nn.gelu(gate.astype(f32))).astype(xr.dtype)
    return out, conv_new, h[:, -1]


def memory_kv(mem, wk, wv):
    B, M, _ = mem.shape
    return ((mem @ wk).reshape(B, M, X_HEADS, X_HEAD_DIM), (mem @ wv).reshape(B, M, X_HEADS, X_HEAD_DIM))


def cross_attention(x, mk, mv, wq, wo):
    B, T, _ = x.shape
    q = (x @ wq).reshape(B, T, X_HEADS, X_HEAD_DIM)
    s = jnp.einsum('bqhd,bkhd->bhqk', q, mk.astype(x.dtype)).astype(jnp.float32) * (X_HEAD_DIM ** -0.5)
    p = jax.nn.softmax(s, axis=-1).astype(x.dtype)
    o = jnp.einsum('bhqk,bkhd->bqhd', p, mv.astype(x.dtype)).reshape(B, T, D_MODEL)
    return o @ wo


def moe(x, router, w1, w3, w2):
    B, T, D = x.shape
    xf = x.reshape(B * T, D)
    probs = jax.nn.softmax((xf @ router).astype(jnp.float32), axis=-1)
    per = N_EXPERTS // N_GROUPS
    group_score = lax.top_k(probs.reshape(-1, N_GROUPS, per), TOP_K)[0].sum(-1)
    group = jnp.argmax(group_score, axis=-1)
    in_group = (jnp.arange(N_EXPERTS) // per)[None, :] == group[:, None]
    vals, idx = lax.top_k(jnp.where(in_group, probs, -1.0), TOP_K)
    gates = vals / jnp.sum(vals, axis=-1, keepdims=True)
    combine = jnp.einsum('nk,nke->ne', gates, jax.nn.one_hot(idx, N_EXPERTS, dtype=jnp.float32)).astype(x.dtype)
    out = jnp.zeros_like(xf)
    for e in range(N_EXPERTS):
        h = jax.nn.silu(xf @ w1[e]) * (xf @ w3[e])
        out = out + combine[:, e:e + 1] * (h @ w2[e])
    return out.reshape(B, T, D)


def trunk_layer(x, w, mem_k, mem_v, cache_k, cache_v, s0, h0, conv0, prompt):
    B, T, _ = x.shape
    splits = [int(s) for s in np.cumsum([A_WIDTH] * 3 + [B_WIDTH] * 4 + [C_WIDTH] * 2)]
    proj = x @ w['w_in']
    aq, ak, av, bq, bf, bi, bg, cx, cg, gl = jnp.split(proj, splits, axis=-1)
    heads = lambda t: t.reshape(B, T, A_HEADS, A_HEAD_DIM)
    if prompt:
        oa = chunk_attention_prompt(heads(aq), heads(ak), heads(av), w['rel'])
        W = min(A_WINDOW, T)
        new_k, new_v = heads(ak)[:, T - W:], heads(av)[:, T - W:]
        block = CHUNK
    else:
        oa, new_k, new_v = chunk_attention_sample(heads(aq), heads(ak), heads(av), cache_k, cache_v, w['rel'])
        block = T
    ob, s_new = hgrn2_branch(bq, bf, bi, bg, w['lb'], w['hgrn_g'], s0, block)
    oc, conv_new, h_new = rglru_branch(cx, cg, w['conv_w'], w['conv_b'], w['wa'], w['ba'], w['wx'], w['bx'],
                                       w['lam'], conv0, h0, prompt)
    branches = jnp.stack([oa, ob, oc], axis=2)
    per_branch = jnp.einsum('btnc,ncd->btnd', branches, w['w_branch'])
    gates = jax.nn.sigmoid(gl.reshape(B, T, N_BRANCH, D_MODEL))
    mixed = jnp.sum(gates * per_branch, axis=2) @ w['w_out']
    x = layer_norm(DN_ALPHA * x + mixed, w['ln1_g'], w['ln1_b'])
    x = layer_norm(DN_ALPHA * x + cross_attention(x, mem_k, mem_v, w['xa_wq'], w['xa_wo']), w['ln2_g'], w['ln2_b'])
    x = layer_norm(DN_ALPHA * x + moe(x, w['router'], w['w1'], w['w3'], w['w2']), w['ln3_g'], w['ln3_b'])
    return x, (new_k, new_v, s_new.astype(x.dtype), h_new.astype(x.dtype), conv_new)


def setup_inputs(seed: int = 0) -> dict:
    key = jax.random.key(seed)
    k = jax.random.split(key, 37)
    f32 = jnp.float32
    nrm = lambda kk, shape, scale: jax.random.normal(kk, shape, f32) * scale
    W = min(A_WINDOW, PAST_LEN)
    u = jax.random.uniform(k[20], (DEPTH, C_WIDTH), f32, minval=0.9, maxval=0.999)
    return {
        'x_prompt': nrm(k[0], (BATCH, SEQ, D_MODEL), 1.0),
        'x_sample': nrm(k[1], (DEC_BATCH, DEC_SEQ, D_MODEL), 1.0),
        'cache_attn_k': nrm(k[2], (DEPTH, DEC_BATCH, W, A_HEADS, A_HEAD_DIM), 1.0),
        'cache_attn_v': nrm(k[3], (DEPTH, DEC_BATCH, W, A_HEADS, A_HEAD_DIM), 1.0),
        'state_hgrn': nrm(k[4], (DEPTH, DEC_BATCH, B_HEADS, B_KEY_DIM, B_VAL_DIM), 0.5),
        'state_rglru': nrm(k[5], (DEPTH, DEC_BATCH, C_WIDTH), 0.5),
        'state_conv': nrm(k[6], (DEPTH, DEC_BATCH, C_CONV - 1, C_WIDTH), 1.0),
        'cache_mem_k': nrm(k[7], (DEPTH, DEC_BATCH, M_TOKENS, X_HEADS, X_HEAD_DIM), 1.0),
        'cache_mem_v': nrm(k[8], (DEPTH, DEC_BATCH, M_TOKENS, X_HEADS, X_HEAD_DIM), 1.0),
        'mem_prompt': nrm(k[9], (BATCH, M_TOKENS, D_MODEL), 1.0),
        'w_in': nrm(k[10], (DEPTH, D_MODEL, IN_COLS), D_MODEL ** -0.5),
        'attn_rel_bias': nrm(k[11], (DEPTH, A_HEADS, 2 * A_MAX_REL + 1), 0.2),
        'hgrn_lb_logits': nrm(k[12], (DEPTH, B_WIDTH), 0.1),
        'hgrn_norm_g': 1.0 + nrm(k[13], (DEPTH, B_VAL_DIM), 0.02),
        'rg_conv_w': nrm(k[14], (DEPTH, C_CONV, C_WIDTH), C_CONV ** -0.5),
        'rg_conv_b': nrm(k[15], (DEPTH, C_WIDTH), 0.02),
        'rg_wa': nrm(k[16], (DEPTH, C_BLOCKS, C_BLOCK, C_BLOCK), C_BLOCK ** -0.5),
        'rg_ba': nrm(k[17], (DEPTH, C_WIDTH), 0.02),
        'rg_wx': nrm(k[18], (DEPTH, C_BLOCKS, C_BLOCK, C_BLOCK), C_BLOCK ** -0.5),
        'rg_bx': nrm(k[19], (DEPTH, C_WIDTH), 0.02),
        'rg_lambda': jnp.log(u) - jnp.log1p(-u),
        'w_branch': nrm(k[21], (DEPTH, N_BRANCH, BRANCH_WIDTH, D_MODEL), BRANCH_WIDTH ** -0.5),
        'w_out': nrm(k[22], (DEPTH, D_MODEL, D_MODEL), DN_BETA * D_MODEL ** -0.5),
        'ln1_g': 1.0 + nrm(k[23], (DEPTH, D_MODEL), 0.02),
        'ln1_b': nrm(k[24], (DEPTH, D_MODEL), 0.02),
        'xa_wq': nrm(k[25], (DEPTH, D_MODEL, D_MODEL), D_MODEL ** -0.5),
        'xa_wk': nrm(k[26], (DEPTH, D_MODEL, D_MODEL), D_MODEL ** -0.5),
        'xa_wv': nrm(k[27], (DEPTH, D_MODEL, D_MODEL), D_MODEL ** -0.5),
        'xa_wo': nrm(k[28], (DEPTH, D_MODEL, D_MODEL), DN_BETA * D_MODEL ** -0.5),
        'ln2_g': 1.0 + nrm(k[29], (DEPTH, D_MODEL), 0.02),
        'ln2_b': nrm(k[30], (DEPTH, D_MODEL), 0.02),
        'moe_router': nrm(k[31], (D_MODEL, N_EXPERTS), D_MODEL ** -0.5),
        'moe_w1': nrm(k[32], (DEPTH, N_EXPERTS, D_MODEL, D_EXPERT), D_MODEL ** -0.5),
        'moe_w3': nrm(k[33], (DEPTH, N_EXPERTS, D_MODEL, D_EXPERT), D_MODEL ** -0.5),
        'moe_w2': nrm(k[34], (DEPTH, N_EXPERTS, D_EXPERT, D_MODEL), DN_BETA * D_EXPERT ** -0.5),
        'ln3_g': 1.0 + nrm(k[35], (DEPTH, D_MODEL), 0.02),
        'ln3_b': nrm(k[36], (DEPTH, D_MODEL), 0.02),
    }


def reference(x_prompt, x_sample, cache_attn_k, cache_attn_v, state_hgrn, state_rglru, state_conv,
              cache_mem_k, cache_mem_v, mem_prompt, w_in, attn_rel_bias, hgrn_lb_logits, hgrn_norm_g,
              rg_conv_w, rg_conv_b, rg_wa, rg_ba, rg_wx, rg_bx, rg_lambda, w_branch, w_out, ln1_g, ln1_b,
              xa_wq, xa_wk, xa_wv, xa_wo, ln2_g, ln2_b, moe_router, moe_w1, moe_w3, moe_w2, ln3_g, ln3_b):
    p = jax.nn.softmax(hgrn_lb_logits.astype(jnp.float32), axis=0)
    lb_all = jnp.cumsum(p, axis=0) - p[0:1]
    B = x_prompt.shape[0]
    s0p = jnp.zeros((B, B_HEADS, B_KEY_DIM, B_VAL_DIM), jnp.float32)
    h0p = jnp.zeros((B, C_WIDTH), jnp.float32)
    conv0p = jnp.zeros((B, C_CONV - 1, C_WIDTH), x_prompt.dtype)
    xp, xs = x_prompt, x_sample
    pk, pv, ph, pr, pc, pmk, pmv = [], [], [], [], [], [], []
    sk, sv, sh, sr, sc = [], [], [], [], []
    for l in range(DEPTH):
        w = {'w_in': w_in[l], 'rel': attn_rel_bias[l], 'lb': lb_all[l], 'hgrn_g': hgrn_norm_g[l],
             'conv_w': rg_conv_w[l], 'conv_b': rg_conv_b[l], 'wa': rg_wa[l], 'ba': rg_ba[l],
             'wx': rg_wx[l], 'bx': rg_bx[l], 'lam': rg_lambda[l], 'w_branch': w_branch[l], 'w_out': w_out[l],
             'ln1_g': ln1_g[l], 'ln1_b': ln1_b[l], 'xa_wq': xa_wq[l], 'xa_wo': xa_wo[l],
             'ln2_g': ln2_g[l], 'ln2_b': ln2_b[l], 'router': moe_router, 'w1': moe_w1[l], 'w3': moe_w3[l],
             'w2': moe_w2[l], 'ln3_g': ln3_g[l], 'ln3_b': ln3_b[l]}
        mk_p, mv_p = memory_kv(mem_prompt, xa_wk[l], xa_wv[l])
        xp, st_p = trunk_layer(xp, w, mk_p, mv_p, None, None, s0p, h0p, conv0p, True)
        xs, st_s = trunk_layer(xs, w, cache_mem_k[l], cache_mem_v[l], cache_attn_k[l], cache_attn_v[l],
                               state_hgrn[l], state_rglru[l], state_conv[l], False)
        pk.append(st_p[0]); pv.append(st_p[1]); ph.append(st_p[2]); pr.append(st_p[3]); pc.append(st_p[4])
        pmk.append(mk_p); pmv.append(mv_p)
        sk.append(st_s[0]); sv.append(st_s[1]); sh.append(st_s[2]); sr.append(st_s[3]); sc.append(st_s[4])
    return (xp, xs,
            jnp.stack(pk), jnp.stack(pv), jnp.stack(ph), jnp.stack(pr), jnp.stack(pc), jnp.stack(pmk), jnp.stack(pmv),
            jnp.stack(sk), jnp.stack(sv), jnp.stack(sh), jnp.stack(sr), jnp.stack(sc))
```

```python
import functools

import numpy as np
import jax
import jax.numpy as jnp
from jax import lax
from jax.experimental import pallas as pl
from jax.experimental.pallas import tpu as pltpu

F32 = jnp.float32
BF16 = jnp.bfloat16

D_MODEL = 1024
CHUNK = 64
A_HEADS = 8
A_HEAD_DIM = 64
A_WIDTH = A_HEADS * A_HEAD_DIM
A_PAST_CHUNKS = 8
A_WINDOW = A_PAST_CHUNKS * CHUNK
A_MAX_REL = 256
B_HEADS = 8
B_KEY_DIM = 64
B_VAL_DIM = 64
B_WIDTH = B_HEADS * B_KEY_DIM
C_WIDTH = 512
C_CONV = 4
C_GATE_C = 8.0
N_BRANCH = 3
IN_COLS = 3 * A_WIDTH + 4 * B_WIDTH + 2 * C_WIDTH + N_BRANCH * D_MODEL
X_HEADS = 4
X_HEAD_DIM = D_MODEL // X_HEADS
N_EXPERTS = 16
N_GROUPS = 4
GROUP_SIZE = N_EXPERTS // N_GROUPS
D_EXPERT = D_MODEL // 2
DEPTH = 4
DN_ALPHA = (2 * DEPTH) ** 0.25
LN_EPS = 1e-5
RMS_EPS = 1e-6
NEG_INF = -1e30

VMEM_LIMIT_BYTES = 56 * 1024 * 1024

ATTN_Q_TILE = 256
ATTN_K_TILE = 256
HG_GROUP = 256
HG_HALF = CHUNK // 2
HG_QUARTER = CHUNK // 4


def _dot(a, b):
    return jnp.dot(a, b, preferred_element_type=F32)


def _dot_nt(a, b):
    return lax.dot_general(a, b, (((1,), (1,)), ((), ())), preferred_element_type=F32)


def _dot_tn(a, b):
    return lax.dot_general(a, b, (((0,), (0,)), ((), ())), preferred_element_type=F32)


def _split2(x):
    hi = x.astype(BF16)
    lo = (x - hi.astype(F32)).astype(BF16)
    return hi, lo


def _split3(x):
    hi = x.astype(BF16)
    r1 = x - hi.astype(F32)
    mid = r1.astype(BF16)
    lo = (r1 - mid.astype(F32)).astype(BF16)
    return hi, mid, lo


def _sigmoid(x):
    return 1.0 / (1.0 + jnp.exp(-x))


def _neg_expm1(y):
    series = -y * (1.0 + y * (1.0 / 2 + y * (1.0 / 6 + y * (1.0 / 24 + y * (1.0 / 120 + y * (1.0 / 720))))))
    return jnp.where(y > -0.1, series, 1.0 - jnp.exp(y))


def _layer_norm(x, g, b):
    mu = jnp.mean(x, axis=-1, keepdims=True)
    xc = x - mu
    var = jnp.mean(xc * xc, axis=-1, keepdims=True)
    return xc * lax.rsqrt(var + LN_EPS) * g + b


def _params(*semantics):
    return pltpu.CompilerParams(dimension_semantics=semantics, vmem_limit_bytes=VMEM_LIMIT_BYTES)


def _const_spec(shape):
    nd = len(shape)
    return pl.BlockSpec(shape, lambda *_: (0,) * nd)


def _inproj_kernel(x_ref, w_ref, qkv_ref, kv_ref, hg_ref, rg_ref, gl_ref):
    xb = x_ref[...].astype(BF16)
    cw = 512

    def mm(c0):
        return _dot(xb, w_ref[:, c0:c0 + cw])

    for j in range(3):
        acc = mm(cw * j)
        qkv_ref[:, cw * j:cw * (j + 1)] = acc.astype(BF16)
        if j >= 1:
            kv_ref[:, cw * (j - 1):cw * j] = acc
    base = 3 * A_WIDTH
    for j in range(4):
        hg_ref[:, cw * j:cw * (j + 1)] = mm(base + cw * j)
    base += 4 * B_WIDTH
    for j in range(2):
        rg_ref[:, cw * j:cw * (j + 1)] = mm(base + cw * j)
    base += 2 * C_WIDTH
    for j in range(N_BRANCH * D_MODEL // cw):
        gl_ref[:, cw * j:cw * (j + 1)] = mm(base + cw * j).astype(BF16)


def _inproj(x, w_in):
    n = x.shape[0]
    tm = 256
    assert n % tm == 0
    row = lambda i: (i, 0)
    return pl.pallas_call(
        _inproj_kernel,
        grid=(n // tm,),
        in_specs=[pl.BlockSpec((tm, D_MODEL), row), _const_spec((D_MODEL, IN_COLS))],
        out_specs=[pl.BlockSpec((tm, 3 * A_WIDTH), row), pl.BlockSpec((tm, 2 * A_WIDTH), row),
                   pl.BlockSpec((tm, 4 * B_WIDTH), row), pl.BlockSpec((tm, 2 * C_WIDTH), row),
                   pl.BlockSpec((tm, N_BRANCH * D_MODEL), row)],
        out_shape=[jax.ShapeDtypeStruct((n, 3 * A_WIDTH), BF16), jax.ShapeDtypeStruct((n, 2 * A_WIDTH), F32),
                   jax.ShapeDtypeStruct((n, 4 * B_WIDTH), F32), jax.ShapeDtypeStruct((n, 2 * C_WIDTH), F32),
                   jax.ShapeDtypeStruct((n, N_BRANCH * D_MODEL), BF16)],
        compiler_params=_params("arbitrary"),
        name="inproj",
    )(x, w_in)


def _matmul_kernel(x_ref, w_ref, o_ref):
    o_ref[...] = _dot(x_ref[...].astype(BF16), w_ref[...])


def _matmul(x, w):
    n, k = x.shape
    m = w.shape[1]
    tm = 256
    return pl.pallas_call(
        _matmul_kernel,
        grid=(n // tm,),
        in_specs=[pl.BlockSpec((tm, k), lambda i: (i, 0)), _const_spec((k, m))],
        out_specs=pl.BlockSpec((tm, m), lambda i: (i, 0)),
        out_shape=jax.ShapeDtypeStruct((n, m), F32),
        compiler_params=_params("arbitrary"),
        name="matmul",
    )(x, w)


def _attn_core(q, k, v, bias_ref, valid):
    rows = q.shape[0]
    lane = lax.broadcasted_iota(jnp.int32, (rows, 2 * A_HEAD_DIM), 1)
    first = lane < A_HEAD_DIM
    scale = A_HEAD_DIM ** -0.5
    outs = []
    for p in range(A_HEADS // 2):
        sl = slice(2 * A_HEAD_DIM * p, 2 * A_HEAD_DIM * (p + 1))
        qp, kp, vp = q[:, sl], k[:, sl], v[:, sl]
        pair = None
        for hh in range(2):
            sel = first if hh == 0 else jnp.logical_not(first)
            qm = jnp.where(sel, qp, jnp.zeros_like(qp))
            s = _dot_nt(qm, kp) * scale + bias_ref[2 * p + hh]
            if valid is not None:
                s = jnp.where(valid, s, NEG_INF)
            m = jnp.max(s, axis=-1, keepdims=True)
            e = jnp.exp(s - m)
            l = jnp.sum(e, axis=-1, keepdims=True)
            o = _dot(e.astype(BF16), vp) * (1.0 / l)
            pair = o if hh == 0 else jnp.where(first, pair, o)
        outs.append(pair)
    return jnp.concatenate(outs, axis=-1)


def _attn_prompt_kernel(q_ref, k0_ref, k1_ref, k2_ref, v0_ref, v1_ref, v2_ref, bias_ref, o_ref):
    i = pl.program_id(1)
    k = jnp.concatenate([k0_ref[...], k1_ref[...], k2_ref[...]], axis=0)
    v = jnp.concatenate([v0_ref[...], v1_ref[...], v2_ref[...]], axis=0)
    col = lax.broadcasted_iota(jnp.int32, (ATTN_Q_TILE, 3 * ATTN_K_TILE), 1)
    valid = col >= (2 - i) * ATTN_K_TILE
    o_ref[...] = _attn_core(q_ref[...], k, v, bias_ref, valid).astype(BF16)


def _attn_prompt(qkv, bias, batch, seq):
    nt = seq // ATTN_Q_TILE
    blk = (ATTN_Q_TILE, A_WIDTH)

    def kv_spec(j, col):
        return pl.BlockSpec(blk, lambda b, i: (b * nt + jnp.maximum(i - 2 + j, 0), col))

    return pl.pallas_call(
        _attn_prompt_kernel,
        grid=(batch, nt),
        in_specs=[pl.BlockSpec(blk, lambda b, i: (b * nt + i, 0))]
        + [kv_spec(j, 1) for j in range(3)] + [kv_spec(j, 2) for j in range(3)]
        + [_const_spec(bias.shape)],
        out_specs=pl.BlockSpec(blk, lambda b, i: (b * nt + i, 0)),
        out_shape=jax.ShapeDtypeStruct((batch * seq, A_WIDTH), BF16),
        compiler_params=_params("arbitrary", "arbitrary"),
        name="attn_prompt",
    )(qkv, qkv, qkv, qkv, qkv, qkv, qkv, bias)


def _attn_sample_kernel(q_ref, kn_ref, vn_ref, ck_ref, cv_ref, bias_ref, o_ref):
    pad = jnp.zeros((CHUNK, A_WIDTH), BF16)
    k = jnp.concatenate([ck_ref[0].astype(BF16), kn_ref[...], pad], axis=0)
    v = jnp.concatenate([cv_ref[0].astype(BF16), vn_ref[...], pad], axis=0)
    o_ref[...] = _attn_core(q_ref[...], k, v, bias_ref, None).astype(BF16)


def _attn_sample(qkv, cache_k, cache_v, bias, batch, seq):
    win = cache_k.shape[1]
    blk = (seq, A_WIDTH)
    return pl.pallas_call(
        _attn_sample_kernel,
        grid=(batch,),
        in_specs=[pl.BlockSpec(blk, lambda b: (b, 0)), pl.BlockSpec(blk, lambda b: (b, 1)),
                  pl.BlockSpec(blk, lambda b: (b, 2)),
                  pl.BlockSpec((1, win, A_WIDTH), lambda b: (b, 0, 0)),
                  pl.BlockSpec((1, win, A_WIDTH), lambda b: (b, 0, 0)),
                  _const_spec(bias.shape)],
        out_specs=pl.BlockSpec(blk, lambda b: (b, 0)),
        out_shape=jax.ShapeDtypeStruct((batch * seq, A_WIDTH), BF16),
        compiler_params=_params("arbitrary"),
        name="attn_sample",
    )(qkv, qkv, qkv, cache_k, cache_v, bias)


def _band_bias(table):
    r = np.arange(ATTN_Q_TILE)[:, None]
    j = np.arange(3 * ATTN_K_TILE)[None, :]
    idx = np.clip(r + 2 * ATTN_K_TILE - j, -A_MAX_REL, A_MAX_REL) + A_MAX_REL
    qc = r // CHUNK
    kc = j // CHUNK
    band = (kc >= qc) & (kc <= qc + A_PAST_CHUNKS)
    return jnp.where(band[None], table.astype(F32)[:, idx], NEG_INF)


def _head_block_diag(x, head_masks):
    zero = jnp.zeros_like(x)
    return jnp.concatenate([jnp.where(m, x, zero) for m in head_masks], axis=0)


def _hgrn_kernel(hg_ref, lb_ref, ng_ref, s0_ref, o_ref, sfin_ref, st_ref, ot_ref, *, tile):
    i = pl.program_id(1)
    n_groups = B_WIDTH // HG_GROUP

    @pl.when(i == 0)
    def _():
        st_ref[...] = s0_ref[0]

    q = hg_ref[:, 0:B_WIDTH]
    f_logit = hg_ref[:, B_WIDTH:2 * B_WIDTH]
    v_in = hg_ref[:, 2 * B_WIDTH:3 * B_WIDTH].astype(BF16)
    lb = lb_ref[...]
    f = lb + (1.0 - lb) * _sigmoid(f_logit)
    log_f = jnp.log(f)
    kk = 1.0 - f

    r_t = lax.broadcasted_iota(jnp.int32, (tile, tile), 0)
    c_t = lax.broadcasted_iota(jnp.int32, (tile, tile), 1)
    tri = jnp.where((r_t // CHUNK == c_t // CHUNK) & (c_t <= r_t), 1.0, 0.0).astype(BF16)
    hi, mid, lo = _split3(log_f)
    g_all = _dot(tri, lo) + _dot(tri, mid) + _dot(tri, hi)

    row = lax.broadcasted_iota(jnp.int32, (CHUNK, B_WIDTH), 0)
    upper = row >= HG_HALF
    lane_g = lax.broadcasted_iota(jnp.int32, (CHUNK, HG_GROUP), 1)
    row_g = lax.broadcasted_iota(jnp.int32, (CHUNK, HG_GROUP), 0)
    head_masks = [lane_g // B_KEY_DIM == h for h in range(HG_GROUP // B_KEY_DIM)]
    causal = (lane_g % CHUNK) <= row_g
    r_bd = lax.broadcasted_iota(jnp.int32, (HG_GROUP, HG_GROUP), 0)
    c_bd = lax.broadcasted_iota(jnp.int32, (HG_GROUP, HG_GROUP), 1)
    diag_blocks = (r_bd // B_VAL_DIM) == (c_bd // B_KEY_DIM)

    for c in range(tile // CHUNK):
        rs = slice(CHUNK * c, CHUNK * (c + 1))
        g = g_all[rs]
        qc = q[rs]
        kc = kk[rs]
        vc = v_in[rs]
        g_q1 = g[HG_QUARTER - 1:HG_QUARTER]
        g_mid = g[HG_HALF - 1:HG_HALF]
        g_q3 = g[HG_HALF + HG_QUARTER - 1:HG_HALF + HG_QUARTER]
        g_last = g[CHUNK - 1:CHUNK]
        d_diag = g - jnp.where(upper, g_q3, g_q1)
        d_off = jnp.where(upper, g - g_mid, g_mid - g)
        q_in = (qc * jnp.exp(g)).astype(BF16)
        k_st = (kc * jnp.exp(g_last - g)).astype(BF16)
        q_diag = qc * jnp.exp(d_diag)
        k_diag = kc * jnp.exp(-d_diag)
        e_off = jnp.exp(d_off)
        zero = jnp.zeros_like(qc)
        q_lo = jnp.where(upper, zero, q_diag).astype(BF16)
        q_hi = jnp.where(upper, q_diag, zero).astype(BF16)
        q_x = jnp.where(upper, qc * e_off, zero).astype(BF16)
        k_lo = jnp.where(upper, zero, k_diag).astype(BF16)
        k_hi = jnp.where(upper, k_diag, zero).astype(BF16)
        k_x = jnp.where(upper, zero, kc * e_off).astype(BF16)
        decay = jnp.exp(g_last)
        for gi in range(n_groups):
            cs = slice(HG_GROUP * gi, HG_GROUP * (gi + 1))
            a_cat = jnp.concatenate([q_lo[:, cs], q_hi[:, cs], q_x[:, cs]], axis=1)
            k_cat = jnp.concatenate([_head_block_diag(k_lo[:, cs], head_masks),
                                     _head_block_diag(k_hi[:, cs], head_masks),
                                     _head_block_diag(k_x[:, cs], head_masks)], axis=1)
            att = _dot_nt(a_cat, k_cat)
            att = jnp.where(causal, att, 0.0).astype(BF16)
            v_bd = _head_block_diag(vc[:, cs], head_masks)
            st = st_ref[gi]
            o_g = _dot_nt(q_in[:, cs], st.astype(BF16)) + _dot(att, v_bd)
            ot_ref[rs, cs] = o_g
            upd = _dot_tn(vc[:, cs], k_st[:, cs])
            st_ref[gi] = st * decay[:, cs] + jnp.where(diag_blocks, upd, 0.0)

    o = ot_ref[...]
    lane_i = lax.broadcasted_iota(jnp.int32, (B_WIDTH, B_WIDTH), 0)
    lane_j = lax.broadcasted_iota(jnp.int32, (B_WIDTH, B_WIDTH), 1)
    head_ones = jnp.where(lane_i // B_VAL_DIM == lane_j // B_VAL_DIM, 1.0, 0.0).astype(BF16)
    sq_hi, sq_mid, sq_lo = _split3(o * o)
    ms = (_dot(sq_lo, head_ones) + _dot(sq_mid, head_ones) + _dot(sq_hi, head_ones)) * (1.0 / B_VAL_DIM)
    gate = hg_ref[:, 3 * B_WIDTH:4 * B_WIDTH]
    out = o * lax.rsqrt(ms + RMS_EPS) * ng_ref[...] * (gate * _sigmoid(gate))
    o_ref[...] = out.astype(BF16)

    @pl.when(i == pl.num_programs(1) - 1)
    def _():
        sfin_ref[0] = st_ref[...]


def _hgrn(hg, lb, norm_g, s0_bd, batch, seq, tile):
    nt = seq // tile
    n_groups = B_WIDTH // HG_GROUP
    st_blk = (1, n_groups, HG_GROUP, HG_GROUP)
    return pl.pallas_call(
        functools.partial(_hgrn_kernel, tile=tile),
        grid=(batch, nt),
        in_specs=[pl.BlockSpec((tile, 4 * B_WIDTH), lambda b, i: (b * nt + i, 0)),
                  _const_spec((1, B_WIDTH)), _const_spec((1, B_WIDTH)),
                  pl.BlockSpec(st_blk, lambda b, i: (b, 0, 0, 0))],
        out_specs=[pl.BlockSpec((tile, B_WIDTH), lambda b, i: (b * nt + i, 0)),
                   pl.BlockSpec(st_blk, lambda b, i: (b, 0, 0, 0))],
        out_shape=[jax.ShapeDtypeStruct((batch * seq, B_WIDTH), BF16),
                   jax.ShapeDtypeStruct((batch,) + st_blk[1:], F32)],
        scratch_shapes=[pltpu.VMEM((n_groups, HG_GROUP, HG_GROUP), F32), pltpu.VMEM((tile, B_WIDTH), F32)],
        compiler_params=_params("arbitrary", "arbitrary"),
        name="hgrn",
    )(hg, lb, norm_g, s0_bd)


def _state_to_block_diag(s):
    b = s.shape[0]
    hpg = HG_GROUP // B_KEY_DIM
    st = s.astype(F32).reshape(b, B_HEADS // hpg, hpg, B_KEY_DIM, B_VAL_DIM).transpose(0, 1, 2, 4, 3)
    bd = jnp.einsum('bghvc,hk->bghvkc', st, jnp.eye(hpg, dtype=F32))
    return bd.reshape(b, B_HEADS // hpg, HG_GROUP, HG_GROUP)


def _block_diag_to_state(bd):
    b = bd.shape[0]
    hpg = HG_GROUP // B_KEY_DIM
    x = bd.reshape(b, B_HEADS // hpg, hpg, B_VAL_DIM, hpg, B_KEY_DIM)
    st = jnp.einsum('bghvkc,hk->bghvc', x, jnp.eye(hpg, dtype=F32))
    return st.transpose(0, 1, 2, 4, 3).reshape(b, B_HEADS, B_KEY_DIM, B_VAL_DIM)


def _rglru_kernel(rg_ref, conv0_ref, h0_ref, cw_ref, cb_ref, wa_ref, ba_ref, wx_ref, bx_ref, lam_ref,
                  o_ref, hlast_ref, xbuf_ref, hc_ref, *, tile, at_start):
    i = pl.program_id(1)
    pad = 8

    @pl.when(i == 0)
    def _():
        xbuf_ref[0:pad] = conv0_ref[0]
        hc_ref[...] = h0_ref[0]

    xr = rg_ref[:, 0:C_WIDTH]
    gate = rg_ref[:, C_WIDTH:2 * C_WIDTH]
    xbuf_ref[pad:pad + tile] = xr
    xc = cb_ref[...] + cw_ref[C_CONV - 1:C_CONV] * xr
    for j in range(1, C_CONV):
        xc = xc + cw_ref[C_CONV - 1 - j:C_CONV - j] * xbuf_ref[pad - j:pad - j + tile]
    xbuf_ref[0:pad] = xbuf_ref[tile:tile + pad]

    xcb = xc.astype(BF16)
    r = _sigmoid(_dot(xcb, wa_ref[...]) + ba_ref[...])
    ig = _sigmoid(_dot(xcb, wx_ref[...]) + bx_ref[...])
    neg_lam = -lam_ref[...]
    softplus = jnp.maximum(neg_lam, 0.0) + jnp.log(1.0 + jnp.exp(-jnp.abs(neg_lam)))
    log_a = -C_GATE_C * r * softplus
    a = jnp.exp(log_a)
    mult = jnp.sqrt(_neg_expm1(2.0 * log_a))
    row = lax.broadcasted_iota(jnp.int32, (tile, C_WIDTH), 0)
    if at_start:
        mult = jnp.where((row == 0) & (i == 0), 1.0, mult)
    b = mult * ig * xc

    d = 1
    while d < tile:
        a_sh = pltpu.roll(a, d, 0)
        b_sh = pltpu.roll(b, d, 0)
        keep = row >= d
        b = jnp.where(keep, a * b_sh + b, b)
        a = jnp.where(keep, a * a_sh, a)
        d *= 2
    h = a * hc_ref[...] + b
    hc_ref[...] = h[tile - 1:tile]
    hlast_ref[0] = h[tile - 1:tile]
    gelu = 0.5 * gate * (1.0 + jnp.tanh(np.sqrt(2.0 / np.pi).astype(np.float32) * (gate + 0.044715 * gate * gate * gate)))
    o_ref[...] = (h * gelu).astype(BF16)


def _rglru(rg, conv0_pad, h0, w, batch, seq, tile, at_start):
    nt = seq // tile
    vec = _const_spec((1, C_WIDTH))
    return pl.pallas_call(
        functools.partial(_rglru_kernel, tile=tile, at_start=at_start),
        grid=(batch, nt),
        in_specs=[pl.BlockSpec((tile, 2 * C_WIDTH), lambda b, i: (b * nt + i, 0)),
                  pl.BlockSpec((1, 8, C_WIDTH), lambda b, i: (b, 0, 0)),
                  pl.BlockSpec((1, 1, C_WIDTH), lambda b, i: (b, 0, 0)),
                  _const_spec((C_CONV, C_WIDTH)), vec,
                  _const_spec((C_WIDTH, C_WIDTH)), vec, _const_spec((C_WIDTH, C_WIDTH)), vec, vec],
        out_specs=[pl.BlockSpec((tile, C_WIDTH), lambda b, i: (b * nt + i, 0)),
                   pl.BlockSpec((1, 1, C_WIDTH), lambda b, i: (b, 0, 0))],
        out_shape=[jax.ShapeDtypeStruct((batch * seq, C_WIDTH), BF16),
                   jax.ShapeDtypeStruct((batch, 1, C_WIDTH), F32)],
        scratch_shapes=[pltpu.VMEM((tile + 8, C_WIDTH), F32), pltpu.VMEM((1, C_WIDTH), F32)],
        compiler_params=_params("arbitrary", "arbitrary"),
        name="rglru",
    )(rg, conv0_pad, h0, w['conv_w'], w['conv_b'], w['wa_bd'], w['ba'], w['wx_bd'], w['bx'], w['lam'])


def _block_diag_weight(w):
    n, d, e = w.shape
    return jnp.einsum('nde,nm->ndme', w, jnp.eye(n, dtype=w.dtype)).reshape(n * d, n * e)


def _route(logits_t):
    m = jnp.max(logits_t, axis=0, keepdims=True)
    e = jnp.exp(logits_t - m)
    p = e / jnp.sum(e, axis=0, keepdims=True)
    rows = [p[j:j + 1] for j in range(N_EXPERTS)]
    scores = []
    for g in range(N_GROUPS):
        mem = rows[GROUP_SIZE * g:GROUP_SIZE * (g + 1)]
        best = None
        for a in range(GROUP_SIZE):
            for b in range(a + 1, GROUP_SIZE):
                pair = mem[a] + mem[b]
                best = pair if best is None else jnp.maximum(best, pair)
        scores.append(best)
    smax = functools.reduce(jnp.maximum, scores)
    taken = jnp.zeros_like(smax)
    sel = []
    for g in range(N_GROUPS):
        hit = jnp.where(scores[g] == smax, 1.0, 0.0) * (1.0 - taken)
        taken = taken + hit
        sel.append(hit)
    picked = []
    for j in range(N_EXPERTS):
        g = j // GROUP_SIZE
        rank = jnp.zeros_like(smax)
        for o in range(GROUP_SIZE * g, GROUP_SIZE * (g + 1)):
            if o == j:
                continue
            ahead = (rows[o] >= rows[j]) if o < j else (rows[o] > rows[j])
            rank = rank + jnp.where(ahead, 1.0, 0.0)
        picked.append(sel[g] * jnp.where(rank < float(2), 1.0, 0.0))
    denom = functools.reduce(lambda x, y: x + y, [picked[j] * rows[j] for j in range(N_EXPERTS)])
    return jnp.concatenate([picked[j] * rows[j] / denom for j in range(N_EXPERTS)], axis=0)


def _mix_kernel(x_ref, oa_ref, ob_ref, oc_ref, gl_ref, mk_ref, mv_ref, wb_ref, wout_ref, wq_ref, wo_ref,
                g1_ref, b1_ref, g2_ref, b2_ref, rt_ref, x2_ref, comb_ref):
    x = x_ref[...]
    mixed = None
    for b, o_ref in enumerate((oa_ref, ob_ref, oc_ref)):
        per_branch = _dot(o_ref[...], wb_ref[b])
        gate = _sigmoid(gl_ref[:, D_MODEL * b:D_MODEL * (b + 1)].astype(F32))
        mixed = gate * per_branch if mixed is None else mixed + gate * per_branch
    x1 = _layer_norm(DN_ALPHA * x + _dot(mixed.astype(BF16), wout_ref[...]), g1_ref[...], b1_ref[...])

    q = _dot(x1.astype(BF16), wq_ref[...]).astype(BF16)
    scale = X_HEAD_DIM ** -0.5
    heads = []
    for h in range(X_HEADS):
        sl = slice(X_HEAD_DIM * h, X_HEAD_DIM * (h + 1))
        s = _dot_nt(q[:, sl], mk_ref[0, :, sl]) * scale
        m = jnp.max(s, axis=-1, keepdims=True)
        e = jnp.exp(s - m)
        l = jnp.sum(e, axis=-1, keepdims=True)
        heads.append((_dot(e.astype(BF16), mv_ref[0, :, sl]) * (1.0 / l)).astype(BF16))
    attn = _dot(jnp.concatenate(heads, axis=-1), wo_ref[...])
    x2 = _layer_norm(DN_ALPHA * x1 + attn, g2_ref[...], b2_ref[...])
    x2_ref[...] = x2

    x_hi, x_lo = _split2(x2)
    r_hi, r_lo = _split2(rt_ref[...])
    logits_t = _dot_nt(r_lo, x_hi) + _dot_nt(r_hi, x_lo) + _dot_nt(r_hi, x_hi)
    comb_ref[0] = _route(logits_t)


def _mix(x, oa, ob, oc, gl, mk, mv, w, batch, seq):
    tm = min(256, seq)
    nt = seq // tm
    row = lambda b, i: (b * nt + i, 0)
    vec = _const_spec((1, D_MODEL))
    n_mem = mk.shape[1]
    mem = pl.BlockSpec((1, n_mem, D_MODEL), lambda b, i: (b, 0, 0))
    sq = _const_spec((D_MODEL, D_MODEL))
    return pl.pallas_call(
        _mix_kernel,
        grid=(batch, nt),
        in_specs=[pl.BlockSpec((tm, D_MODEL), row)] + [pl.BlockSpec((tm, A_WIDTH), row)] * 3
        + [pl.BlockSpec((tm, N_BRANCH * D_MODEL), row), mem, mem,
           _const_spec((N_BRANCH, A_WIDTH, D_MODEL)), sq, sq, sq, vec, vec, vec, vec,
           _const_spec((N_EXPERTS, D_MODEL))],
        out_specs=[pl.BlockSpec((tm, D_MODEL), row),
                   pl.BlockSpec((1, N_EXPERTS, tm), lambda b, i: (b * nt + i, 0, 0))],
        out_shape=[jax.ShapeDtypeStruct((batch * seq, D_MODEL), F32),
                   jax.ShapeDtypeStruct((batch * nt, N_EXPERTS, tm), F32)],
        compiler_params=_params("arbitrary", "arbitrary"),
        name="mix",
    )(x, oa, ob, oc, gl, mk, mv, w['w_branch'], w['w_out'], w['xa_wq'], w['xa_wo'],
      w['ln1_g'], w['ln1_b'], w['ln2_g'], w['ln2_b'], w['router_t'])


def _moe_kernel(x_ref, comb_ref, w1_ref, w3_ref, w2_ref, g_ref, b_ref, o_ref, xb_ref, acc_ref):
    e = pl.program_id(1)

    @pl.when(e == 0)
    def _():
        xb_ref[...] = x_ref[...].astype(BF16)
        acc_ref[...] = jnp.zeros_like(acc_ref)

    xb = xb_ref[...]
    h1 = _dot(xb, w1_ref[0])
    h3 = _dot(xb, w3_ref[0])
    h = (h1 * _sigmoid(h1) * h3).astype(BF16)
    lane = lax.broadcasted_iota(jnp.int32, comb_ref.shape, 1)
    c = jnp.sum(jnp.where(lane == e, comb_ref[...], 0.0), axis=1, keepdims=True)
    acc_ref[...] += c * _dot(h, w2_ref[0])

    @pl.when(e == pl.num_programs(1) - 1)
    def _():
        o_ref[...] = _layer_norm(DN_ALPHA * x_ref[...] + acc_ref[...], g_ref[...], b_ref[...])


def _moe(x, comb, w):
    n = x.shape[0]
    tm = min(n, 1024)
    vec = _const_spec((1, D_MODEL))
    return pl.pallas_call(
        _moe_kernel,
        grid=(n // tm, N_EXPERTS),
        in_specs=[pl.BlockSpec((tm, D_MODEL), lambda i, e: (i, 0)),
                  pl.BlockSpec((tm, N_EXPERTS), lambda i, e: (i, 0)),
                  pl.BlockSpec((1, D_MODEL, D_EXPERT), lambda i, e: (e, 0, 0)),
                  pl.BlockSpec((1, D_MODEL, D_EXPERT), lambda i, e: (e, 0, 0)),
                  pl.BlockSpec((1, D_EXPERT, D_MODEL), lambda i, e: (e, 0, 0)), vec, vec],
        out_specs=pl.BlockSpec((tm, D_MODEL), lambda i, e: (i, 0)),
        out_shape=jax.ShapeDtypeStruct((n, D_MODEL), F32),
        scratch_shapes=[pltpu.VMEM((tm, D_MODEL), BF16), pltpu.VMEM((tm, D_MODEL), F32)],
        compiler_params=_params("arbitrary", "arbitrary"),
        name="moe",
    )(x, comb, w['w1'], w['w3'], w['w2'], w['ln3_g'], w['ln3_b'])


def _trunk_layer(x, w, mem_k, mem_v, cache_k, cache_v, s0, h0, conv0, batch, seq, prompt):
    qkv, kv32, hg, rg, gl = _inproj(x, w['w_in'])
    kv32 = kv32.reshape(batch, seq, 2, A_HEADS, A_HEAD_DIM)
    if prompt:
        oa = _attn_prompt(qkv, w['band_bias'], batch, seq)
        win = min(A_WINDOW, seq)
        new_k, new_v = kv32[:, seq - win:, 0], kv32[:, seq - win:, 1]
        tile = 256
    else:
        win = cache_k.shape[1]
        ck = cache_k.reshape(batch, win, A_WIDTH)
        cv = cache_v.reshape(batch, win, A_WIDTH)
        oa = _attn_sample(qkv, ck, cv, w['band_bias'][:, :seq, :win + 2 * CHUNK], batch, seq)
        new_k = jnp.concatenate([cache_k, kv32[:, :, 0]], axis=1)[:, seq:]
        new_v = jnp.concatenate([cache_v, kv32[:, :, 1]], axis=1)[:, seq:]
        tile = seq
    ob, s_bd = _hgrn(hg, w['lb'], w['hgrn_g'], _state_to_block_diag(s0), batch, seq, tile)
    conv0_pad = jnp.concatenate([jnp.zeros((batch, 8 - (C_CONV - 1), C_WIDTH), F32), conv0.astype(F32)], axis=1)
    oc, h_new = _rglru(rg, conv0_pad, h0.astype(F32).reshape(batch, 1, C_WIDTH), w, batch, seq, tile, prompt)
    conv_new = jnp.concatenate([conv0.astype(F32), rg.reshape(batch, seq, 2 * C_WIDTH)[:, :, :C_WIDTH]],
                               axis=1)[:, seq:]
    x2, comb_t = _mix(x, oa, ob, oc, gl, mem_k, mem_v, w, batch, seq)
    x3 = _moe(x2, comb_t.transpose(0, 2, 1).reshape(batch * seq, N_EXPERTS), w)
    return x3, (new_k, new_v, _block_diag_to_state(s_bd), h_new.reshape(batch, C_WIDTH), conv_new)


def kernel(x_prompt, x_sample, cache_attn_k, cache_attn_v, state_hgrn, state_rglru, state_conv, cache_mem_k, cache_mem_v, mem_prompt, w_in, attn_rel_bias, hgrn_lb_logits, hgrn_norm_g, rg_conv_w, rg_conv_b, rg_wa, rg_ba, rg_wx, rg_bx, rg_lambda, w_branch, w_out, ln1_g, ln1_b, xa_wq, xa_wk, xa_wv, xa_wo, ln2_g, ln2_b, moe_router, moe_w1, moe_w3, moe_w2, ln3_g, ln3_b):
    bp, tp, _ = x_prompt.shape
    bs, ts, _ = x_sample.shape
    n_mem = mem_prompt.shape[1]
    depth = w_in.shape[0]

    p = jax.nn.softmax(hgrn_lb_logits.astype(F32), axis=0)
    lb_all = jnp.cumsum(p, axis=0) - p[0:1]
    vec = lambda t: t.astype(F32).reshape(1, -1)

    xp = x_prompt.reshape(bp * tp, D_MODEL)
    xs = x_sample.reshape(bs * ts, D_MODEL)
    mem2d = mem_prompt.reshape(bp * n_mem, D_MODEL)
    s0p = jnp.zeros((bp, B_HEADS, B_KEY_DIM, B_VAL_DIM), F32)
    h0p = jnp.zeros((bp, C_WIDTH), F32)
    conv0p = jnp.zeros((bp, C_CONV - 1, C_WIDTH), F32)
    router_t = moe_router.astype(F32).T

    outs_p, outs_s, mem_ks, mem_vs = [], [], [], []
    for l in range(depth):
        w = {
            'w_in': w_in[l].astype(BF16), 'band_bias': _band_bias(attn_rel_bias[l]),
            'lb': vec(lb_all[l]), 'hgrn_g': vec(jnp.tile(hgrn_norm_g[l], B_HEADS)),
            'conv_w': rg_conv_w[l].astype(F32), 'conv_b': vec(rg_conv_b[l]),
            'wa_bd': _block_diag_weight(rg_wa[l]).astype(BF16), 'ba': vec(rg_ba[l]),
            'wx_bd': _block_diag_weight(rg_wx[l]).astype(BF16), 'bx': vec(rg_bx[l]), 'lam': vec(rg_lambda[l]),
            'w_branch': w_branch[l].astype(BF16), 'w_out': w_out[l].astype(BF16),
            'xa_wq': xa_wq[l].astype(BF16), 'xa_wo': xa_wo[l].astype(BF16),
            'ln1_g': vec(ln1_g[l]), 'ln1_b': vec(ln1_b[l]), 'ln2_g': vec(ln2_g[l]), 'ln2_b': vec(ln2_b[l]),
            'ln3_g': vec(ln3_g[l]), 'ln3_b': vec(ln3_b[l]), 'router_t': router_t,
            'w1': moe_w1[l].astype(BF16), 'w3': moe_w3[l].astype(BF16), 'w2': moe_w2[l].astype(BF16),
        }
        mk_p = _matmul(mem2d, xa_wk[l].astype(BF16))
        mv_p = _matmul(mem2d, xa_wv[l].astype(BF16))
        mem_ks.append(mk_p.reshape(bp, n_mem, X_HEADS, X_HEAD_DIM))
        mem_vs.append(mv_p.reshape(bp, n_mem, X_HEADS, X_HEAD_DIM))
        xp, st_p = _trunk_layer(xp, w, mk_p.reshape(bp, n_mem, D_MODEL).astype(BF16),
                                mv_p.reshape(bp, n_mem, D_MODEL).astype(BF16),
                                None, None, s0p, h0p, conv0p, bp, tp, True)
        xs, st_s = _trunk_layer(xs, w, cache_mem_k[l].reshape(bs, n_mem, D_MODEL).astype(BF16),
                                cache_mem_v[l].reshape(bs, n_mem, D_MODEL).astype(BF16),
                                cache_attn_k[l], cache_attn_v[l], state_hgrn[l], state_rglru[l], state_conv[l],
                                bs, ts, False)
        outs_p.append(st_p)
        outs_s.append(st_s)

    stack = lambda items, j: jnp.stack([it[j] for it in items])
    return (xp.reshape(bp, tp, D_MODEL), xs.reshape(bs, ts, D_MODEL),
            stack(outs_p, 0), stack(outs_p, 1), stack(outs_p, 2), stack(outs_p, 3), stack(outs_p, 4),
            jnp.stack(mem_ks), jnp.stack(mem_vs),
            stack(outs_s, 0), stack(outs_s, 1), stack(outs_s, 2), stack(outs_s, 3), stack(outs_s, 4))
```

```python
import functools

import numpy as np
import jax
import jax.numpy as jnp
from jax import lax
from jax.experimental import pallas as pl
from jax.experimental.pallas import tpu as pltpu
from jax.experimental.pallas import tpu_sc as plsc

F32 = jnp.float32
BF16 = jnp.bfloat16

D_MODEL = 1024
CHUNK = 64
A_HEADS = 8
A_HEAD_DIM = 64
A_WIDTH = A_HEADS * A_HEAD_DIM
A_PAST_CHUNKS = 8
A_WINDOW = A_PAST_CHUNKS * CHUNK
A_MAX_REL = 256
B_HEADS = 8
B_KEY_DIM = 64
B_VAL_DIM = 64
B_WIDTH = B_HEADS * B_KEY_DIM
C_WIDTH = 512
C_CONV = 4
C_GATE_C = 8.0
N_BRANCH = 3
IN_COLS = 3 * A_WIDTH + 4 * B_WIDTH + 2 * C_WIDTH + N_BRANCH * D_MODEL
X_HEADS = 4
X_HEAD_DIM = D_MODEL // X_HEADS
N_EXPERTS = 16
N_GROUPS = 4
GROUP_SIZE = N_EXPERTS // N_GROUPS
D_EXPERT = D_MODEL // 2
DEPTH = 4
DN_ALPHA = (2 * DEPTH) ** 0.25
LN_EPS = 1e-5
RMS_EPS = 1e-6
NEG_INF = -1e30

VMEM_LIMIT_BYTES = 56 * 1024 * 1024

ATTN_Q_TILE = 256
ATTN_K_TILE = 256
HG_GROUP = 256
HG_HALF = CHUNK // 2
HG_QUARTER = CHUNK // 4
ROUTE_ROWS = 16
ROUTE_LANES = 128
MOE_TILE = 512
SC_CORES = 2
SC_SUBCORES = 16
SC_WORKERS = SC_CORES * SC_SUBCORES
SC_GATHER_WINDOW = 32


def _dot(a, b):
    return jnp.dot(a, b, preferred_element_type=F32)


def _dot_nt(a, b):
    return lax.dot_general(a, b, (((1,), (1,)), ((), ())), preferred_element_type=F32)


def _dot_tn(a, b):
    return lax.dot_general(a, b, (((0,), (0,)), ((), ())), preferred_element_type=F32)


def _split2(x):
    hi = x.astype(BF16)
    lo = (x - hi.astype(F32)).astype(BF16)
    return hi, lo


def _split3(x):
    hi = x.astype(BF16)
    r1 = x - hi.astype(F32)
    mid = r1.astype(BF16)
    lo = (r1 - mid.astype(F32)).astype(BF16)
    return hi, mid, lo


def _sigmoid(x):
    return 1.0 / (1.0 + jnp.exp(-x))


def _neg_expm1(y):
    series = -y * (1.0 + y * (1.0 / 2 + y * (1.0 / 6 + y * (1.0 / 24 + y * (1.0 / 120 + y * (1.0 / 720))))))
    return jnp.where(y > -0.1, series, 1.0 - jnp.exp(y))


def _layer_norm(x, g, b):
    mu = jnp.mean(x, axis=-1, keepdims=True)
    xc = x - mu
    var = jnp.mean(xc * xc, axis=-1, keepdims=True)
    return xc * lax.rsqrt(var + LN_EPS) * g + b


def _params(*semantics):
    return pltpu.CompilerParams(dimension_semantics=semantics, vmem_limit_bytes=VMEM_LIMIT_BYTES)


def _const_spec(shape):
    nd = len(shape)
    return pl.BlockSpec(shape, lambda *_: (0,) * nd)


def _inproj_kernel(x_ref, w_ref, qkv_ref, kv_ref, hg_ref, rg_ref, gl_ref):
    xb = x_ref[...].astype(BF16)
    cw = 512

    def mm(c0):
        return _dot(xb, w_ref[:, c0:c0 + cw])

    for j in range(3):
        acc = mm(cw * j)
        qkv_ref[:, cw * j:cw * (j + 1)] = acc.astype(BF16)
        if j >= 1:
            kv_ref[:, cw * (j - 1):cw * j] = acc
    base = 3 * A_WIDTH
    for j in range(4):
        hg_ref[:, cw * j:cw * (j + 1)] = mm(base + cw * j)
    base += 4 * B_WIDTH
    for j in range(2):
        rg_ref[:, cw * j:cw * (j + 1)] = mm(base + cw * j)
    base += 2 * C_WIDTH
    for j in range(N_BRANCH * D_MODEL // cw):
        gl_ref[:, cw * j:cw * (j + 1)] = mm(base + cw * j).astype(BF16)


def _inproj(x, w_in):
    n = x.shape[0]
    tm = 256
    assert n % tm == 0
    row = lambda i: (i, 0)
    return pl.pallas_call(
        _inproj_kernel,
        grid=(n // tm,),
        in_specs=[pl.BlockSpec((tm, D_MODEL), row), _const_spec((D_MODEL, IN_COLS))],
        out_specs=[pl.BlockSpec((tm, 3 * A_WIDTH), row), pl.BlockSpec((tm, 2 * A_WIDTH), row),
                   pl.BlockSpec((tm, 4 * B_WIDTH), row), pl.BlockSpec((tm, 2 * C_WIDTH), row),
                   pl.BlockSpec((tm, N_BRANCH * D_MODEL), row)],
        out_shape=[jax.ShapeDtypeStruct((n, 3 * A_WIDTH), BF16), jax.ShapeDtypeStruct((n, 2 * A_WIDTH), F32),
                   jax.ShapeDtypeStruct((n, 4 * B_WIDTH), F32), jax.ShapeDtypeStruct((n, 2 * C_WIDTH), F32),
                   jax.ShapeDtypeStruct((n, N_BRANCH * D_MODEL), BF16)],
        compiler_params=_params("arbitrary"),
        name="inproj",
    )(x, w_in)


def _matmul_kernel(x_ref, w_ref, o_ref):
    o_ref[...] = _dot(x_ref[...].astype(BF16), w_ref[...])


def _matmul(x, w):
    n, k = x.shape
    m = w.shape[1]
    tm = 256
    return pl.pallas_call(
        _matmul_kernel,
        grid=(n // tm,),
        in_specs=[pl.BlockSpec((tm, k), lambda i: (i, 0)), _const_spec((k, m))],
        out_specs=pl.BlockSpec((tm, m), lambda i: (i, 0)),
        out_shape=jax.ShapeDtypeStruct((n, m), F32),
        compiler_params=_params("arbitrary"),
        name="matmul",
    )(x, w)


def _attn_core(q, k, v, bias_ref, valid):
    rows = q.shape[0]
    lane = lax.broadcasted_iota(jnp.int32, (rows, 2 * A_HEAD_DIM), 1)
    first = lane < A_HEAD_DIM
    scale = A_HEAD_DIM ** -0.5
    outs = []
    for p in range(A_HEADS // 2):
        sl = slice(2 * A_HEAD_DIM * p, 2 * A_HEAD_DIM * (p + 1))
        qp, kp, vp = q[:, sl], k[:, sl], v[:, sl]
        pair = None
        for hh in range(2):
            sel = first if hh == 0 else jnp.logical_not(first)
            qm = jnp.where(sel, qp, jnp.zeros_like(qp))
            s = _dot_nt(qm, kp) * scale + bias_ref[2 * p + hh]
            if valid is not None:
                s = jnp.where(valid, s, NEG_INF)
            m = jnp.max(s, axis=-1, keepdims=True)
            e = jnp.exp(s - m)
            l = jnp.sum(e, axis=-1, keepdims=True)
            o = _dot(e.astype(BF16), vp) * (1.0 / l)
            pair = o if hh == 0 else jnp.where(first, pair, o)
        outs.append(pair)
    return jnp.concatenate(outs, axis=-1)


def _attn_prompt_kernel(q_ref, k0_ref, k1_ref, k2_ref, v0_ref, v1_ref, v2_ref, bias_ref, o_ref):
    i = pl.program_id(1)
    k = jnp.concatenate([k0_ref[...], k1_ref[...], k2_ref[...]], axis=0)
    v = jnp.concatenate([v0_ref[...], v1_ref[...], v2_ref[...]], axis=0)
    col = lax.broadcasted_iota(jnp.int32, (ATTN_Q_TILE, 3 * ATTN_K_TILE), 1)
    valid = col >= (2 - i) * ATTN_K_TILE
    o_ref[...] = _attn_core(q_ref[...], k, v, bias_ref, valid).astype(BF16)


def _attn_prompt(qkv, bias, batch, seq):
    nt = seq // ATTN_Q_TILE
    blk = (ATTN_Q_TILE, A_WIDTH)

    def kv_spec(j, col):
        return pl.BlockSpec(blk, lambda b, i: (b * nt + jnp.maximum(i - 2 + j, 0), col))

    return pl.pallas_call(
        _attn_prompt_kernel,
        grid=(batch, nt),
        in_specs=[pl.BlockSpec(blk, lambda b, i: (b * nt + i, 0))]
        + [kv_spec(j, 1) for j in range(3)] + [kv_spec(j, 2) for j in range(3)]
        + [_const_spec(bias.shape)],
        out_specs=pl.BlockSpec(blk, lambda b, i: (b * nt + i, 0)),
        out_shape=jax.ShapeDtypeStruct((batch * seq, A_WIDTH), BF16),
        compiler_params=_params("arbitrary", "arbitrary"),
        name="attn_prompt",
    )(qkv, qkv, qkv, qkv, qkv, qkv, qkv, bias)


def _attn_sample_kernel(q_ref, kn_ref, vn_ref, ck_ref, cv_ref, bias_ref, o_ref):
    pad = jnp.zeros((CHUNK, A_WIDTH), BF16)
    k = jnp.concatenate([ck_ref[0].astype(BF16), kn_ref[...], pad], axis=0)
    v = jnp.concatenate([cv_ref[0].astype(BF16), vn_ref[...], pad], axis=0)
    o_ref[...] = _attn_core(q_ref[...], k, v, bias_ref, None).astype(BF16)


def _attn_sample(qkv, cache_k, cache_v, bias, batch, seq):
    win = cache_k.shape[1]
    blk = (seq, A_WIDTH)
    return pl.pallas_call(
        _attn_sample_kernel,
        grid=(batch,),
        in_specs=[pl.BlockSpec(blk, lambda b: (b, 0)), pl.BlockSpec(blk, lambda b: (b, 1)),
                  pl.BlockSpec(blk, lambda b: (b, 2)),
                  pl.BlockSpec((1, win, A_WIDTH), lambda b: (b, 0, 0)),
                  pl.BlockSpec((1, win, A_WIDTH), lambda b: (b, 0, 0)),
                  _const_spec(bias.shape)],
        out_specs=pl.BlockSpec(blk, lambda b: (b, 0)),
        out_shape=jax.ShapeDtypeStruct((batch * seq, A_WIDTH), BF16),
        compiler_params=_params("arbitrary"),
        name="attn_sample",
    )(qkv, qkv, qkv, cache_k, cache_v, bias)


def _band_bias(table):
    heads = table.shape[0]
    n_keys = 3 * ATTN_K_TILE
    span = ATTN_Q_TILE + n_keys
    r = np.arange(ATTN_Q_TILE)[:, None]
    j = np.arange(n_keys)[None, :]
    band = (j // CHUNK >= r // CHUNK) & (j // CHUNK <= r // CHUNK + A_PAST_CHUNKS)
    t = table.astype(F32)
    u = jnp.concatenate([t, jnp.broadcast_to(t[:, -1:], (heads, span - t.shape[1]))], axis=1)
    w = u[:, ::-1]
    tiled = jnp.broadcast_to(w[:, None, :], (heads, ATTN_Q_TILE, span)).reshape(heads, ATTN_Q_TILE * span)
    view = tiled[:, :ATTN_Q_TILE * (span - 1)].reshape(heads, ATTN_Q_TILE, span - 1)
    bias = view[:, :, ATTN_Q_TILE - 1:ATTN_Q_TILE - 1 + n_keys]
    return jnp.where(band[None], bias, NEG_INF)


def _head_block_diag(x, head_masks):
    zero = jnp.zeros_like(x)
    return jnp.concatenate([jnp.where(m, x, zero) for m in head_masks], axis=0)


def _hgrn_kernel(hg_ref, lb_ref, ng_ref, s0_ref, o_ref, sfin_ref, st_ref, ot_ref, *, tile):
    i = pl.program_id(1)
    n_groups = B_WIDTH // HG_GROUP

    @pl.when(i == 0)
    def _():
        st_ref[...] = s0_ref[0]

    q = hg_ref[:, 0:B_WIDTH]
    f_logit = hg_ref[:, B_WIDTH:2 * B_WIDTH]
    v_in = hg_ref[:, 2 * B_WIDTH:3 * B_WIDTH].astype(BF16)
    lb = lb_ref[...]
    f = lb + (1.0 - lb) * _sigmoid(f_logit)
    log_f = jnp.log(f)
    kk = 1.0 - f

    r_t = lax.broadcasted_iota(jnp.int32, (tile, tile), 0)
    c_t = lax.broadcasted_iota(jnp.int32, (tile, tile), 1)
    tri = jnp.where((r_t // CHUNK == c_t // CHUNK) & (c_t <= r_t), 1.0, 0.0).astype(BF16)
    hi, mid, lo = _split3(log_f)
    g_all = _dot(tri, lo) + _dot(tri, mid) + _dot(tri, hi)

    row = lax.broadcasted_iota(jnp.int32, (CHUNK, B_WIDTH), 0)
    upper = row >= HG_HALF
    lane_g = lax.broadcasted_iota(jnp.int32, (CHUNK, HG_GROUP), 1)
    row_g = lax.broadcasted_iota(jnp.int32, (CHUNK, HG_GROUP), 0)
    head_masks = [lane_g // B_KEY_DIM == h for h in range(HG_GROUP // B_KEY_DIM)]
    causal = (lane_g % CHUNK) <= row_g
    r_bd = lax.broadcasted_iota(jnp.int32, (HG_GROUP, HG_GROUP), 0)
    c_bd = lax.broadcasted_iota(jnp.int32, (HG_GROUP, HG_GROUP), 1)
    diag_blocks = (r_bd // B_VAL_DIM) == (c_bd // B_KEY_DIM)

    for c in range(tile // CHUNK):
        rs = slice(CHUNK * c, CHUNK * (c + 1))
        g = g_all[rs]
        qc = q[rs]
        kc = kk[rs]
        vc = v_in[rs]
        g_q1 = g[HG_QUARTER - 1:HG_QUARTER]
        g_mid = g[HG_HALF - 1:HG_HALF]
        g_q3 = g[HG_HALF + HG_QUARTER - 1:HG_HALF + HG_QUARTER]
        g_last = g[CHUNK - 1:CHUNK]
        d_diag = g - jnp.where(upper, g_q3, g_q1)
        d_off = jnp.where(upper, g - g_mid, g_mid - g)
        q_in = (qc * jnp.exp(g)).astype(BF16)
        k_st = (kc * jnp.exp(g_last - g)).astype(BF16)
        q_diag = qc * jnp.exp(d_diag)
        k_diag = kc * jnp.exp(-d_diag)
        e_off = jnp.exp(d_off)
        zero = jnp.zeros_like(qc)
        q_lo = jnp.where(upper, zero, q_diag).astype(BF16)
        q_hi = jnp.where(upper, q_diag, zero).astype(BF16)
        q_x = jnp.where(upper, qc * e_off, zero).astype(BF16)
        k_lo = jnp.where(upper, zero, k_diag).astype(BF16)
        k_hi = jnp.where(upper, k_diag, zero).astype(BF16)
        k_x = jnp.where(upper, zero, kc * e_off).astype(BF16)
        decay = jnp.exp(g_last)
        for gi in range(n_groups):
            cs = slice(HG_GROUP * gi, HG_GROUP * (gi + 1))
            a_cat = jnp.concatenate([q_lo[:, cs], q_hi[:, cs], q_x[:, cs]], axis=1)
            k_cat = jnp.concatenate([_head_block_diag(k_lo[:, cs], head_masks),
                                     _head_block_diag(k_hi[:, cs], head_masks),
                                     _head_block_diag(k_x[:, cs], head_masks)], axis=1)
            att = _dot_nt(a_cat, k_cat)
            att = jnp.where(causal, att, 0.0).astype(BF16)
            v_bd = _head_block_diag(vc[:, cs], head_masks)
            st = st_ref[gi]
            o_g = _dot_nt(q_in[:, cs], st.astype(BF16)) + _dot(att, v_bd)
            ot_ref[rs, cs] = o_g
            upd = _dot_tn(vc[:, cs], k_st[:, cs])
            st_ref[gi] = st * decay[:, cs] + jnp.where(diag_blocks, upd, 0.0)

    o = ot_ref[...]
    lane_i = lax.broadcasted_iota(jnp.int32, (B_WIDTH, B_WIDTH), 0)
    lane_j = lax.broadcasted_iota(jnp.int32, (B_WIDTH, B_WIDTH), 1)
    head_ones = jnp.where(lane_i // B_VAL_DIM == lane_j // B_VAL_DIM, 1.0, 0.0).astype(BF16)
    sq_hi, sq_mid, sq_lo = _split3(o * o)
    ms = (_dot(sq_lo, head_ones) + _dot(sq_mid, head_ones) + _dot(sq_hi, head_ones)) * (1.0 / B_VAL_DIM)
    gate = hg_ref[:, 3 * B_WIDTH:4 * B_WIDTH]
    out = o * lax.rsqrt(ms + RMS_EPS) * ng_ref[...] * (gate * _sigmoid(gate))
    o_ref[...] = out.astype(BF16)

    @pl.when(i == pl.num_programs(1) - 1)
    def _():
        sfin_ref[0] = st_ref[...]


def _hgrn(hg, lb, norm_g, s0_bd, batch, seq, tile):
    nt = seq // tile
    n_groups = B_WIDTH // HG_GROUP
    st_blk = (1, n_groups, HG_GROUP, HG_GROUP)
    return pl.pallas_call(
        functools.partial(_hgrn_kernel, tile=tile),
        grid=(batch, nt),
        in_specs=[pl.BlockSpec((tile, 4 * B_WIDTH), lambda b, i: (b * nt + i, 0)),
                  _const_spec((1, B_WIDTH)), _const_spec((1, B_WIDTH)),
                  pl.BlockSpec(st_blk, lambda b, i: (b, 0, 0, 0))],
        out_specs=[pl.BlockSpec((tile, B_WIDTH), lambda b, i: (b * nt + i, 0)),
                   pl.BlockSpec(st_blk, lambda b, i: (b, 0, 0, 0))],
        out_shape=[jax.ShapeDtypeStruct((batch * seq, B_WIDTH), BF16),
                   jax.ShapeDtypeStruct((batch,) + st_blk[1:], F32)],
        scratch_shapes=[pltpu.VMEM((n_groups, HG_GROUP, HG_GROUP), F32), pltpu.VMEM((tile, B_WIDTH), F32)],
        compiler_params=_params("arbitrary", "arbitrary"),
        name="hgrn",
    )(hg, lb, norm_g, s0_bd)


def _state_to_block_diag(s):
    b = s.shape[0]
    hpg = HG_GROUP // B_KEY_DIM
    st = s.astype(F32).reshape(b, B_HEADS // hpg, hpg, B_KEY_DIM, B_VAL_DIM).transpose(0, 1, 2, 4, 3)
    bd = jnp.einsum('bghvc,hk->bghvkc', st, jnp.eye(hpg, dtype=F32))
    return bd.reshape(b, B_HEADS // hpg, HG_GROUP, HG_GROUP)


def _block_diag_to_state(bd):
    b = bd.shape[0]
    hpg = HG_GROUP // B_KEY_DIM
    x = bd.reshape(b, B_HEADS // hpg, hpg, B_VAL_DIM, hpg, B_KEY_DIM)
    st = jnp.einsum('bghvkc,hk->bghvc', x, jnp.eye(hpg, dtype=F32))
    return st.transpose(0, 1, 2, 4, 3).reshape(b, B_HEADS, B_KEY_DIM, B_VAL_DIM)


def _rglru_kernel(rg_ref, conv0_ref, h0_ref, cw_ref, cb_ref, wa_ref, ba_ref, wx_ref, bx_ref, lam_ref,
                  o_ref, hlast_ref, xbuf_ref, hc_ref, *, tile, at_start):
    i = pl.program_id(1)
    pad = 8

    @pl.when(i == 0)
    def _():
        xbuf_ref[0:pad] = conv0_ref[0]
        hc_ref[...] = h0_ref[0]

    xr = rg_ref[:, 0:C_WIDTH]
    gate = rg_ref[:, C_WIDTH:2 * C_WIDTH]
    xbuf_ref[pad:pad + tile] = xr
    xc = cb_ref[...] + cw_ref[C_CONV - 1:C_CONV] * xr
    for j in range(1, C_CONV):
        xc = xc + cw_ref[C_CONV - 1 - j:C_CONV - j] * xbuf_ref[pad - j:pad - j + tile]
    xbuf_ref[0:pad] = xbuf_ref[tile:tile + pad]

    xcb = xc.astype(BF16)
    r = _sigmoid(_dot(xcb, wa_ref[...]) + ba_ref[...])
    ig = _sigmoid(_dot(xcb, wx_ref[...]) + bx_ref[...])
    neg_lam = -lam_ref[...]
    softplus = jnp.maximum(neg_lam, 0.0) + jnp.log(1.0 + jnp.exp(-jnp.abs(neg_lam)))
    log_a = -C_GATE_C * r * softplus
    a = jnp.exp(log_a)
    mult = jnp.sqrt(_neg_expm1(2.0 * log_a))
    row = lax.broadcasted_iota(jnp.int32, (tile, C_WIDTH), 0)
    if at_start:
        mult = jnp.where((row == 0) & (i == 0), 1.0, mult)
    b = mult * ig * xc

    d = 1
    while d < tile:
        a_sh = pltpu.roll(a, d, 0)
        b_sh = pltpu.roll(b, d, 0)
        keep = row >= d
        b = jnp.where(keep, a * b_sh + b, b)
        a = jnp.where(keep, a * a_sh, a)
        d *= 2
    h = a * hc_ref[...] + b
    hc_ref[...] = h[tile - 1:tile]
    hlast_ref[0] = h[tile - 1:tile]
    gelu = 0.5 * gate * (1.0 + jnp.tanh(np.sqrt(2.0 / np.pi).astype(np.float32) * (gate + 0.044715 * gate * gate * gate)))
    o_ref[...] = (h * gelu).astype(BF16)


def _rglru(rg, conv0_pad, h0, w, batch, seq, tile, at_start):
    nt = seq // tile
    vec = _const_spec((1, C_WIDTH))
    return pl.pallas_call(
        functools.partial(_rglru_kernel, tile=tile, at_start=at_start),
        grid=(batch, nt),
        in_specs=[pl.BlockSpec((tile, 2 * C_WIDTH), lambda b, i: (b * nt + i, 0)),
                  pl.BlockSpec((1, 8, C_WIDTH), lambda b, i: (b, 0, 0)),
                  pl.BlockSpec((1, 1, C_WIDTH), lambda b, i: (b, 0, 0)),
                  _const_spec((C_CONV, C_WIDTH)), vec,
                  _const_spec((C_WIDTH, C_WIDTH)), vec, _const_spec((C_WIDTH, C_WIDTH)), vec, vec],
        out_specs=[pl.BlockSpec((tile, C_WIDTH), lambda b, i: (b * nt + i, 0)),
                   pl.BlockSpec((1, 1, C_WIDTH), lambda b, i: (b, 0, 0))],
        out_shape=[jax.ShapeDtypeStruct((batch * seq, C_WIDTH), BF16),
                   jax.ShapeDtypeStruct((batch, 1, C_WIDTH), F32)],
        scratch_shapes=[pltpu.VMEM((tile + 8, C_WIDTH), F32), pltpu.VMEM((1, C_WIDTH), F32)],
        compiler_params=_params("arbitrary", "arbitrary"),
        name="rglru",
    )(rg, conv0_pad, h0, w['conv_w'], w['conv_b'], w['wa_bd'], w['ba'], w['wx_bd'], w['bx'], w['lam'])


def _block_diag_weight(w):
    n, d, e = w.shape
    return jnp.einsum('nde,nm->ndme', w, jnp.eye(n, dtype=w.dtype)).reshape(n * d, n * e)


def _route(logits_t):
    m = jnp.max(logits_t, axis=0, keepdims=True)
    e = jnp.exp(logits_t - m)
    p = e / jnp.sum(e, axis=0, keepdims=True)
    rows = [p[j:j + 1] for j in range(N_EXPERTS)]
    scores = []
    for g in range(N_GROUPS):
        mem = rows[GROUP_SIZE * g:GROUP_SIZE * (g + 1)]
        best = None
        for a in range(GROUP_SIZE):
            for b in range(a + 1, GROUP_SIZE):
                pair = mem[a] + mem[b]
                best = pair if best is None else jnp.maximum(best, pair)
        scores.append(best)
    smax = functools.reduce(jnp.maximum, scores)
    taken = jnp.zeros_like(smax)
    sel = []
    for g in range(N_GROUPS):
        hit = jnp.where(scores[g] == smax, 1.0, 0.0) * (1.0 - taken)
        taken = taken + hit
        sel.append(hit)
    picked = []
    for j in range(N_EXPERTS):
        g = j // GROUP_SIZE
        rank = jnp.zeros_like(smax)
        for o in range(GROUP_SIZE * g, GROUP_SIZE * (g + 1)):
            if o == j:
                continue
            ahead = (rows[o] >= rows[j]) if o < j else (rows[o] > rows[j])
            rank = rank + jnp.where(ahead, 1.0, 0.0)
        picked.append(sel[g] * jnp.where(rank < float(2), 1.0, 0.0))
    denom = functools.reduce(lambda x, y: x + y, [picked[j] * rows[j] for j in range(N_EXPERTS)])
    comb = [picked[j] * rows[j] / denom for j in range(N_EXPERTS)]
    out = [functools.reduce(lambda x, y: x + y, [sel[g] * comb[GROUP_SIZE * g + m] for g in range(N_GROUPS)])
           for m in range(GROUP_SIZE)]
    out.append(functools.reduce(lambda x, y: x + y, [float(g) * sel[g] for g in range(1, N_GROUPS)]))
    out.append(jnp.zeros((ROUTE_ROWS - GROUP_SIZE - 1,) + smax.shape[1:], F32))
    return jnp.concatenate(out, axis=0)


def _mix_kernel(x_ref, oa_ref, ob_ref, oc_ref, gl_ref, mk_ref, mv_ref, wb_ref, wout_ref, wq_ref, wo_ref,
                g1_ref, b1_ref, g2_ref, b2_ref, rt_ref, x2_ref, route_ref):
    x = x_ref[...]
    mixed = None
    for b, o_ref in enumerate((oa_ref, ob_ref, oc_ref)):
        per_branch = _dot(o_ref[...], wb_ref[b])
        gate = _sigmoid(gl_ref[:, D_MODEL * b:D_MODEL * (b + 1)].astype(F32))
        mixed = gate * per_branch if mixed is None else mixed + gate * per_branch
    x1 = _layer_norm(DN_ALPHA * x + _dot(mixed.astype(BF16), wout_ref[...]), g1_ref[...], b1_ref[...])

    q = _dot(x1.astype(BF16), wq_ref[...]).astype(BF16)
    scale = X_HEAD_DIM ** -0.5
    heads = []
    for h in range(X_HEADS):
        sl = slice(X_HEAD_DIM * h, X_HEAD_DIM * (h + 1))
        s = _dot_nt(q[:, sl], mk_ref[0, :, sl]) * scale
        m = jnp.max(s, axis=-1, keepdims=True)
        e = jnp.exp(s - m)
        l = jnp.sum(e, axis=-1, keepdims=True)
        heads.append((_dot(e.astype(BF16), mv_ref[0, :, sl]) * (1.0 / l)).astype(BF16))
    attn = _dot(jnp.concatenate(heads, axis=-1), wo_ref[...])
    x2 = _layer_norm(DN_ALPHA * x1 + attn, g2_ref[...], b2_ref[...])
    x2_ref[...] = x2

    x_hi, x_lo = _split2(x2)
    r_hi, r_lo = _split2(rt_ref[...])
    logits_t = _dot_nt(r_lo, x_hi) + _dot_nt(r_hi, x_lo) + _dot_nt(r_hi, x_hi)
    route_t = _route(logits_t)
    eye_r = lax.broadcasted_iota(jnp.int32, (ROUTE_ROWS, ROUTE_LANES), 0)
    eye_c = lax.broadcasted_iota(jnp.int32, (ROUTE_ROWS, ROUTE_LANES), 1)
    eye = jnp.where(eye_r == eye_c, 1.0, 0.0).astype(BF16)
    hi, mid, lo = _split3(route_t)
    route_ref[...] = _dot_tn(hi, eye) + _dot_tn(mid, eye) + _dot_tn(lo, eye)


def _mix(x, oa, ob, oc, gl, mk, mv, w, batch, seq):
    tm = min(256, seq)
    nt = seq // tm
    row = lambda b, i: (b * nt + i, 0)
    vec = _const_spec((1, D_MODEL))
    n_mem = mk.shape[1]
    mem = pl.BlockSpec((1, n_mem, D_MODEL), lambda b, i: (b, 0, 0))
    sq = _const_spec((D_MODEL, D_MODEL))
    return pl.pallas_call(
        _mix_kernel,
        grid=(batch, nt),
        in_specs=[pl.BlockSpec((tm, D_MODEL), row)] + [pl.BlockSpec((tm, A_WIDTH), row)] * 3
        + [pl.BlockSpec((tm, N_BRANCH * D_MODEL), row), mem, mem,
           _const_spec((N_BRANCH, A_WIDTH, D_MODEL)), sq, sq, sq, vec, vec, vec, vec,
           _const_spec((N_EXPERTS, D_MODEL))],
        out_specs=[pl.BlockSpec((tm, D_MODEL), row), pl.BlockSpec((tm, ROUTE_LANES), row)],
        out_shape=[jax.ShapeDtypeStruct((batch * seq, D_MODEL), F32),
                   jax.ShapeDtypeStruct((batch * seq, ROUTE_LANES), F32)],
        compiler_params=_params("arbitrary", "arbitrary"),
        name="mix",
    )(x, oa, ob, oc, gl, mk, mv, w['w_branch'], w['w_out'], w['xa_wq'], w['xa_wo'],
      w['ln1_g'], w['ln1_b'], w['ln2_g'], w['ln2_b'], w['router_t'])


def _sc_gather_rows(tables, idx):
    n_out = idx.shape[0]
    per_worker = n_out // SC_WORKERS
    window = min(SC_GATHER_WINDOW, per_worker)
    steps = per_worker // window
    assert per_worker * SC_WORKERS == n_out and steps * window == per_worker and window % 8 == 0
    idx3 = idx.astype(jnp.int32).reshape(SC_WORKERS, steps, window)
    mesh = plsc.VectorSubcoreMesh(core_axis_name="core", subcore_axis_name="subcore")
    n_tab = len(tables)

    def body(*refs):
        tab_hbm = refs[:n_tab]
        idx_hbm = refs[n_tab]
        out_hbm = refs[n_tab + 1:2 * n_tab + 1]
        idx_v = refs[2 * n_tab + 1]
        rows_v = refs[2 * n_tab + 2:3 * n_tab + 2]
        sem = refs[3 * n_tab + 2]
        wid = lax.axis_index("subcore") * SC_CORES + lax.axis_index("core")
        pltpu.sync_copy(idx_hbm.at[wid], idx_v)

        @pl.loop(0, steps)
        def _(j):
            base = wid * per_worker + j * window
            for k in range(n_tab):
                pltpu.async_copy(tab_hbm[k].at[idx_v.at[j]], rows_v[k], sem).wait()
                pltpu.sync_copy(rows_v[k], out_hbm[k].at[pl.ds(base, window)])

    call = pl.kernel(
        body,
        out_type=[jax.ShapeDtypeStruct((n_out, t.shape[1]), t.dtype) for t in tables],
        mesh=mesh,
        scratch_types=[pltpu.VMEM((steps, window), jnp.int32)]
        + [pltpu.VMEM((window, t.shape[1]), t.dtype) for t in tables] + [pltpu.SemaphoreType.DMA],
        name="sc_gather",
    )
    return call(*tables, idx3)


def _dispatch_plan(gid, tm):
    n = gid.shape[0]
    n_pad = n + N_GROUPS * tm
    groups = jnp.arange(N_GROUPS, dtype=jnp.int32)
    onehot = (gid[:, None] == groups[None, :]).astype(jnp.int32)
    counts = jnp.sum(onehot, axis=0)
    csum = jnp.cumsum(onehot, axis=0)
    rank = jnp.sum((csum - onehot) * onehot, axis=1)
    padded = ((counts + tm - 1) // tm) * tm
    start_p = jnp.cumsum(padded) - padded
    start_u = jnp.cumsum(counts) - counts
    pos = jnp.sum(onehot * start_p[None, :], axis=1) + rank
    order = jnp.argsort(gid, stable=True).astype(jnp.int32)
    slot = jnp.arange(n_pad, dtype=jnp.int32)
    g_slot = jnp.minimum(jnp.sum((slot[:, None] >= (start_p + padded)[None, :]).astype(jnp.int32), axis=1),
                         N_GROUPS - 1)
    oh_slot = (g_slot[:, None] == groups[None, :]).astype(jnp.int32)
    r_slot = slot - jnp.sum(oh_slot * start_p[None, :], axis=1)
    valid = r_slot < jnp.sum(oh_slot * counts[None, :], axis=1)
    s = jnp.where(valid, jnp.sum(oh_slot * start_u[None, :], axis=1) + r_slot, 0)
    src = jnp.take(order, s)
    return pos.astype(jnp.int32), src, g_slot[::tm]


def _moe_kernel(tg_ref, x_ref, r_ref, w1_ref, w3_ref, w2_ref, g_ref, b_ref, o_ref):
    del tg_ref
    x = x_ref[...]
    xb = x.astype(BF16)
    r = r_ref[...]
    y = None
    for m in range(GROUP_SIZE):
        h1 = _dot(xb, w1_ref[m])
        h3 = _dot(xb, w3_ref[m])
        h = (h1 * _sigmoid(h1) * h3).astype(BF16)
        ym = r[:, m:m + 1] * _dot(h, w2_ref[m])
        y = ym if y is None else y + ym
    o_ref[...] = _layer_norm(DN_ALPHA * x + y, g_ref[...], b_ref[...])


def _moe_sorted(xs, rs, tile_gid, w, tm):
    n_pad = xs.shape[0]
    row = lambda i, tg: (i, 0)
    grp = lambda i, tg: (tg[i], 0, 0)
    vec = pl.BlockSpec((1, D_MODEL), lambda i, tg: (0, 0))
    return pl.pallas_call(
        _moe_kernel,
        grid_spec=pltpu.PrefetchScalarGridSpec(
            num_scalar_prefetch=1,
            grid=(n_pad // tm,),
            in_specs=[pl.BlockSpec((tm, D_MODEL), row), pl.BlockSpec((tm, ROUTE_LANES), row),
                      pl.BlockSpec((GROUP_SIZE, D_MODEL, D_EXPERT), grp),
                      pl.BlockSpec((GROUP_SIZE, D_MODEL, D_EXPERT), grp),
                      pl.BlockSpec((GROUP_SIZE, D_EXPERT, D_MODEL), grp), vec, vec],
            out_specs=pl.BlockSpec((tm, D_MODEL), row),
        ),
        out_shape=jax.ShapeDtypeStruct((n_pad, D_MODEL), F32),
        compiler_params=_params("arbitrary"),
        name="moe",
    )(tile_gid, xs, rs, w['w1'], w['w3'], w['w2'], w['ln3_g'], w['ln3_b'])


def _moe(x2, route, w):
    n = x2.shape[0]
    tm = min(MOE_TILE, n // N_GROUPS)
    gid = route[:, GROUP_SIZE].astype(jnp.int32)
    pos, src, tile_gid = _dispatch_plan(gid, tm)
    xs, rs = _sc_gather_rows([x2, route], src)
    ys = _moe_sorted(xs, rs, tile_gid, w, tm)
    return _sc_gather_rows([ys], pos)[0]


def _trunk_layer(x, w, mem_k, mem_v, cache_k, cache_v, s0, h0, conv0, batch, seq, prompt):
    qkv, kv32, hg, rg, gl = _inproj(x, w['w_in'])
    kv32 = kv32.reshape(batch, seq, 2, A_HEADS, A_HEAD_DIM)
    if prompt:
        oa = _attn_prompt(qkv, w['band_bias'], batch, seq)
        win = min(A_WINDOW, seq)
        new_k, new_v = kv32[:, seq - win:, 0], kv32[:, seq - win:, 1]
        tile = 256
    else:
        win = cache_k.shape[1]
        ck = cache_k.reshape(batch, win, A_WIDTH)
        cv = cache_v.reshape(batch, win, A_WIDTH)
        oa = _attn_sample(qkv, ck, cv, w['band_bias'][:, :seq, :win + 2 * CHUNK], batch, seq)
        new_k = jnp.concatenate([cache_k, kv32[:, :, 0]], axis=1)[:, seq:]
        new_v = jnp.concatenate([cache_v, kv32[:, :, 1]], axis=1)[:, seq:]
        tile = seq
    ob, s_bd = _hgrn(hg, w['lb'], w['hgrn_g'], _state_to_block_diag(s0), batch, seq, tile)
    conv0_pad = jnp.concatenate([jnp.zeros((batch, 8 - (C_CONV - 1), C_WIDTH), F32), conv0.astype(F32)], axis=1)
    oc, h_new = _rglru(rg, conv0_pad, h0.astype(F32).reshape(batch, 1, C_WIDTH), w, batch, seq, tile, prompt)
    conv_new = jnp.concatenate([conv0.astype(F32), rg.reshape(batch, seq, 2 * C_WIDTH)[:, :, :C_WIDTH]],
                               axis=1)[:, seq:]
    x2, route = _mix(x, oa, ob, oc, gl, mem_k, mem_v, w, batch, seq)
    x3 = _moe(x2, route, w)
    return x3, (new_k, new_v, _block_diag_to_state(s_bd), h_new.reshape(batch, C_WIDTH), conv_new)


def kernel(x_prompt, x_sample, cache_attn_k, cache_attn_v, state_hgrn, state_rglru, state_conv, cache_mem_k, cache_mem_v, mem_prompt, w_in, attn_rel_bias, hgrn_lb_logits, hgrn_norm_g, rg_conv_w, rg_conv_b, rg_wa, rg_ba, rg_wx, rg_bx, rg_lambda, w_branch, w_out, ln1_g, ln1_b, xa_wq, xa_wk, xa_wv, xa_wo, ln2_g, ln2_b, moe_router, moe_w1, moe_w3, moe_w2, ln3_g, ln3_b):
    bp, tp, _ = x_prompt.shape
    bs, ts, _ = x_sample.shape
    n_mem = mem_prompt.shape[1]
    depth = w_in.shape[0]

    p = jax.nn.softmax(hgrn_lb_logits.astype(F32), axis=0)
    lb_all = jnp.cumsum(p, axis=0) - p[0:1]
    vec = lambda t: t.astype(F32).reshape(1, -1)

    xp = x_prompt.reshape(bp * tp, D_MODEL)
    xs = x_sample.reshape(bs * ts, D_MODEL)
    mem2d = mem_prompt.reshape(bp * n_mem, D_MODEL)
    s0p = jnp.zeros((bp, B_HEADS, B_KEY_DIM, B_VAL_DIM), F32)
    h0p = jnp.zeros((bp, C_WIDTH), F32)
    conv0p = jnp.zeros((bp, C_CONV - 1, C_WIDTH), F32)
    router_t = moe_router.astype(F32).T

    outs_p, outs_s, mem_ks, mem_vs = [], [], [], []
    for l in range(depth):
        w = {
            'w_in': w_in[l].astype(BF16), 'band_bias': _band_bias(attn_rel_bias[l]),
            'lb': vec(lb_all[l]), 'hgrn_g': vec(jnp.tile(hgrn_norm_g[l], B_HEADS)),
            'conv_w': rg_conv_w[l].astype(F32), 'conv_b': vec(rg_conv_b[l]),
            'wa_bd': _block_diag_weight(rg_wa[l]).astype(BF16), 'ba': vec(rg_ba[l]),
            'wx_bd': _block_diag_weight(rg_wx[l]).astype(BF16), 'bx': vec(rg_bx[l]), 'lam': vec(rg_lambda[l]),
            'w_branch': w_branch[l].astype(BF16), 'w_out': w_out[l].astype(BF16),
            'xa_wq': xa_wq[l].astype(BF16), 'xa_wo': xa_wo[l].astype(BF16),
            'ln1_g': vec(ln1_g[l]), 'ln1_b': vec(ln1_b[l]), 'ln2_g': vec(ln2_g[l]), 'ln2_b': vec(ln2_b[l]),
            'ln3_g': vec(ln3_g[l]), 'ln3_b': vec(ln3_b[l]), 'router_t': router_t,
            'w1': moe_w1[l].astype(BF16), 'w3': moe_w3[l].astype(BF16), 'w2': moe_w2[l].astype(BF16),
        }
        mk_p = _matmul(mem2d, xa_wk[l].astype(BF16))
        mv_p = _matmul(mem2d, xa_wv[l].astype(BF16))
        mem_ks.append(mk_p.reshape(bp, n_mem, X_HEADS, X_HEAD_DIM))
        mem_vs.append(mv_p.reshape(bp, n_mem, X_HEADS, X_HEAD_DIM))
        xp, st_p = _trunk_layer(xp, w, mk_p.reshape(bp, n_mem, D_MODEL).astype(BF16),
                                mv_p.reshape(bp, n_mem, D_MODEL).astype(BF16),
                                None, None, s0p, h0p, conv0p, bp, tp, True)
        xs, st_s = _trunk_layer(xs, w, cache_mem_k[l].reshape(bs, n_mem, D_MODEL).astype(BF16),
                                cache_mem_v[l].reshape(bs, n_mem, D_MODEL).astype(BF16),
                                cache_attn_k[l], cache_attn_v[l], state_hgrn[l], state_rglru[l], state_conv[l],
                                bs, ts, False)
        outs_p.append(st_p)
        outs_s.append(st_s)

    stack = lambda items, j: jnp.stack([it[j] for it in items])
    return (xp.reshape(bp, tp, D_MODEL), xs.reshape(bs, ts, D_MODEL),
            stack(outs_p, 0), stack(outs_p, 1), stack(outs_p, 2), stack(outs_p, 3), stack(outs_p, 4),
            jnp.stack(mem_ks), jnp.stack(mem_vs),
            stack(outs_s, 0), stack(outs_s, 1), stack(outs_s, 2), stack(outs_s, 3), stack(outs_s, 4))
```

```python
import functools

import numpy as np
import jax
import jax.numpy as jnp
from jax import lax
from jax.experimental import pallas as pl
from jax.experimental.pallas import tpu as pltpu
from jax.experimental.pallas import tpu_sc as plsc

F32 = jnp.float32
BF16 = jnp.bfloat16

D_MODEL = 1024
CHUNK = 64
A_HEADS = 8
A_HEAD_DIM = 64
A_WIDTH = A_HEADS * A_HEAD_DIM
A_PAST_CHUNKS = 8
A_WINDOW = A_PAST_CHUNKS * CHUNK
A_MAX_REL = 256
B_HEADS = 8
B_KEY_DIM = 64
B_VAL_DIM = 64
B_WIDTH = B_HEADS * B_KEY_DIM
C_WIDTH = 512
C_CONV = 4
C_GATE_C = 8.0
N_BRANCH = 3
IN_COLS = 3 * A_WIDTH + 4 * B_WIDTH + 2 * C_WIDTH + N_BRANCH * D_MODEL
X_HEADS = 4
X_HEAD_DIM = D_MODEL // X_HEADS
N_EXPERTS = 16
N_GROUPS = 4
GROUP_SIZE = N_EXPERTS // N_GROUPS
D_EXPERT = D_MODEL // 2
DEPTH = 4
DN_ALPHA = (2 * DEPTH) ** 0.25
LN_EPS = 1e-5
RMS_EPS = 1e-6
NEG_INF = -1e30

VMEM_LIMIT_BYTES = 56 * 1024 * 1024

INPROJ_TILE = 512
MIX_TILE = 256
ATTN_Q_TILE = 256
ATTN_K_TILE = 256
HG_GROUP = 256
HG_SEQS_PER_STEP = 2
HG_HALF = CHUNK // 2
HG_QUARTER = CHUNK // 4
ROUTE_ROWS = 16
ROUTE_LANES = 128
MOE_TILE = 512
SC_CORES = 2
SC_SUBCORES = 16
SC_WORKERS = SC_CORES * SC_SUBCORES
SC_GATHER_WINDOW = 32


def _split2(x):
    hi = x.astype(BF16)
    lo = (x - hi.astype(F32)).astype(BF16)
    return hi, lo


def _contract(a, b, dims):
    dg = lambda u, v: lax.dot_general(u, v, (dims, ((), ())), preferred_element_type=F32)
    if a.dtype == F32:
        a_hi, a_lo = _split2(a)
        b_hi, b_lo = _split2(b)
        return dg(a_lo, b_hi) + dg(a_hi, b_lo) + dg(a_hi, b_hi)
    return dg(a, b)


def _dot(a, b):
    return _contract(a, b, ((1,), (0,)))


def _dot_nt(a, b):
    return _contract(a, b, ((1,), (1,)))


def _dot_tn(a, b):
    return _contract(a, b, ((0,), (0,)))


def _split3(x):
    hi = x.astype(BF16)
    r1 = x - hi.astype(F32)
    mid = r1.astype(BF16)
    lo = (r1 - mid.astype(F32)).astype(BF16)
    return hi, mid, lo


def _sigmoid(x):
    return 0.5 * jnp.tanh(0.5 * x) + 0.5


def _layer_norm(x, g, b):
    mu = jnp.mean(x, axis=-1, keepdims=True)
    xc = x - mu
    var = jnp.mean(xc * xc, axis=-1, keepdims=True)
    return xc * lax.rsqrt(var + LN_EPS) * g + b


def _params(*semantics):
    return pltpu.CompilerParams(dimension_semantics=semantics, vmem_limit_bytes=VMEM_LIMIT_BYTES)


def _const_spec(shape):
    nd = len(shape)
    return pl.BlockSpec(shape, lambda *_: (0,) * nd, pipeline_mode=pl.Buffered(1))


def _inproj_kernel(x_ref, w_ref, qkv_ref, hg_ref, rg_ref, gl_ref):
    xb = x_ref[...].astype(w_ref.dtype)
    cw = 512

    def mm(c0):
        return _dot(xb, w_ref[:, c0:c0 + cw])

    for j in range(3):
        qkv_ref[:, cw * j:cw * (j + 1)] = mm(cw * j).astype(qkv_ref.dtype)
    base = 3 * A_WIDTH
    for j in range(4):
        hg_ref[:, cw * j:cw * (j + 1)] = mm(base + cw * j)
    base += 4 * B_WIDTH
    for j in range(2):
        rg_ref[:, cw * j:cw * (j + 1)] = mm(base + cw * j)
    base += 2 * C_WIDTH
    for j in range(N_BRANCH * D_MODEL // cw):
        gl_ref[:, cw * j:cw * (j + 1)] = mm(base + cw * j).astype(gl_ref.dtype)


def _inproj(x, w_in):
    n = x.shape[0]
    cd = w_in.dtype
    tm = INPROJ_TILE if cd == BF16 else INPROJ_TILE // 2
    assert n % tm == 0
    row = lambda i: (i, 0)
    return pl.pallas_call(
        _inproj_kernel,
        grid=(n // tm,),
        in_specs=[pl.BlockSpec((tm, D_MODEL), row), _const_spec((D_MODEL, IN_COLS))],
        out_specs=[pl.BlockSpec((tm, 3 * A_WIDTH), row),
                   pl.BlockSpec((tm, 4 * B_WIDTH), row), pl.BlockSpec((tm, 2 * C_WIDTH), row),
                   pl.BlockSpec((tm, N_BRANCH * D_MODEL), row)],
        out_shape=[jax.ShapeDtypeStruct((n, 3 * A_WIDTH), cd),
                   jax.ShapeDtypeStruct((n, 4 * B_WIDTH), F32), jax.ShapeDtypeStruct((n, 2 * C_WIDTH), F32),
                   jax.ShapeDtypeStruct((n, N_BRANCH * D_MODEL), cd)],
        compiler_params=_params("arbitrary"),
        name="inproj",
    )(x, w_in)


def _matmul_kernel(x_ref, w_ref, o_ref):
    o_ref[...] = _dot(x_ref[...].astype(w_ref.dtype), w_ref[...])


def _matmul(x, w):
    n, k = x.shape
    m = w.shape[1]
    tm = 256
    return pl.pallas_call(
        _matmul_kernel,
        grid=(n // tm,),
        in_specs=[pl.BlockSpec((tm, k), lambda i: (i, 0)), _const_spec((k, m))],
        out_specs=pl.BlockSpec((tm, m), lambda i: (i, 0)),
        out_shape=jax.ShapeDtypeStruct((n, m), F32),
        compiler_params=_params("arbitrary"),
        name="matmul",
    )(x, w)


def _attn_core(q, k, v, bias_ref, valid):
    rows = q.shape[0]
    lane = lax.broadcasted_iota(jnp.int32, (rows, 2 * A_HEAD_DIM), 1)
    first = lane < A_HEAD_DIM
    q = q * jnp.asarray(A_HEAD_DIM ** -0.5, q.dtype)

    def scores(head):
        sl = slice(2 * A_HEAD_DIM * (head // 2), 2 * A_HEAD_DIM * (head // 2 + 1))
        sel = first if head % 2 == 0 else jnp.logical_not(first)
        qm = jnp.where(sel, q[:, sl], jnp.zeros_like(q[:, sl]))
        s = _dot_nt(qm, k[:, sl]) + bias_ref[head]
        return s if valid is None else jnp.where(valid, s, NEG_INF)

    outs = []
    pair = None
    s_next = scores(0)
    for head in range(A_HEADS):
        s = s_next
        if head + 1 < A_HEADS:
            s_next = scores(head + 1)
        m = jnp.max(s, axis=-1, keepdims=True)
        e = jnp.exp(s - m)
        l = jnp.sum(e, axis=-1, keepdims=True)
        sl = slice(2 * A_HEAD_DIM * (head // 2), 2 * A_HEAD_DIM * (head // 2 + 1))
        o = _dot(e.astype(v.dtype), v[:, sl]) * (1.0 / l)
        if head % 2 == 0:
            pair = o
        else:
            outs.append(jnp.where(first, pair, o))
    return jnp.concatenate(outs, axis=-1)


def _attn_prompt_kernel(q_ref, k0_ref, k1_ref, k2_ref, v0_ref, v1_ref, v2_ref, bias_ref, o_ref):
    i = pl.program_id(1)

    def run(masked):
        k = jnp.concatenate([k0_ref[...], k1_ref[...], k2_ref[...]], axis=0)
        v = jnp.concatenate([v0_ref[...], v1_ref[...], v2_ref[...]], axis=0)
        valid = None
        if masked:
            col = lax.broadcasted_iota(jnp.int32, (ATTN_Q_TILE, 3 * ATTN_K_TILE), 1)
            valid = col >= (2 - i) * ATTN_K_TILE
        o_ref[...] = _attn_core(q_ref[...], k, v, bias_ref, valid).astype(o_ref.dtype)

    pl.when(i < 2)(lambda: run(True))
    pl.when(i >= 2)(lambda: run(False))


def _attn_prompt(qkv, bias, batch, seq):
    nt = seq // ATTN_Q_TILE
    blk = (ATTN_Q_TILE, A_WIDTH)

    def kv_spec(j, col):
        return pl.BlockSpec(blk, lambda b, i: (b * nt + jnp.maximum(i - 2 + j, 0), col))

    return pl.pallas_call(
        _attn_prompt_kernel,
        grid=(batch, nt),
        in_specs=[pl.BlockSpec(blk, lambda b, i: (b * nt + i, 0))]
        + [kv_spec(j, 1) for j in range(3)] + [kv_spec(j, 2) for j in range(3)]
        + [_const_spec(bias.shape)],
        out_specs=pl.BlockSpec(blk, lambda b, i: (b * nt + i, 0)),
        out_shape=jax.ShapeDtypeStruct((batch * seq, A_WIDTH), qkv.dtype),
        compiler_params=_params("arbitrary", "arbitrary"),
        name="attn_prompt",
    )(qkv, qkv, qkv, qkv, qkv, qkv, qkv, bias)


def _attn_sample_kernel(q_ref, kn_ref, vn_ref, ck_ref, cv_ref, bias_ref, o_ref):
    cd = q_ref.dtype
    pad = jnp.zeros((CHUNK, A_WIDTH), cd)
    k = jnp.concatenate([ck_ref[0].astype(cd), kn_ref[...], pad], axis=0)
    v = jnp.concatenate([cv_ref[0].astype(cd), vn_ref[...], pad], axis=0)
    o_ref[...] = _attn_core(q_ref[...], k, v, bias_ref, None).astype(cd)


def _attn_sample(qkv, cache_k, cache_v, bias, batch, seq):
    win = cache_k.shape[1]
    blk = (seq, A_WIDTH)
    return pl.pallas_call(
        _attn_sample_kernel,
        grid=(batch,),
        in_specs=[pl.BlockSpec(blk, lambda b: (b, 0)), pl.BlockSpec(blk, lambda b: (b, 1)),
                  pl.BlockSpec(blk, lambda b: (b, 2)),
                  pl.BlockSpec((1, win, A_WIDTH), lambda b: (b, 0, 0)),
                  pl.BlockSpec((1, win, A_WIDTH), lambda b: (b, 0, 0)),
                  _const_spec(bias.shape)],
        out_specs=pl.BlockSpec(blk, lambda b: (b, 0)),
        out_shape=jax.ShapeDtypeStruct((batch * seq, A_WIDTH), qkv.dtype),
        compiler_params=_params("arbitrary"),
        name="attn_sample",
    )(qkv, qkv, qkv, cache_k, cache_v, bias)


def _band_bias(table):
    heads = table.shape[0]
    n_keys = 3 * ATTN_K_TILE
    span = ATTN_Q_TILE + n_keys
    r = np.arange(ATTN_Q_TILE)[:, None]
    j = np.arange(n_keys)[None, :]
    band = (j // CHUNK >= r // CHUNK) & (j // CHUNK <= r // CHUNK + A_PAST_CHUNKS)
    t = table.astype(F32)
    u = jnp.concatenate([t, jnp.broadcast_to(t[:, -1:], (heads, span - t.shape[1]))], axis=1)
    w = u[:, ::-1]
    tiled = jnp.broadcast_to(w[:, None, :], (heads, ATTN_Q_TILE, span)).reshape(heads, ATTN_Q_TILE * span)
    view = tiled[:, :ATTN_Q_TILE * (span - 1)].reshape(heads, ATTN_Q_TILE, span - 1)
    bias = view[:, :, ATTN_Q_TILE - 1:ATTN_Q_TILE - 1 + n_keys]
    return jnp.where(band[None], bias, NEG_INF)


def _head_block_diag(x, head_masks):
    zero = jnp.zeros_like(x)
    return jnp.concatenate([jnp.where(m, x, zero) for m in head_masks], axis=0)


def _hgrn_kernel(hg_ref, lb_ref, ng_ref, s0_ref, o_ref, sfin_ref, st_ref, ot_ref, *, tile):
    i = pl.program_id(1)

    @pl.when(i == 0)
    def _():
        st_ref[...] = s0_ref[...]

    for bb in range(hg_ref.shape[0]):
        _hgrn_rows(hg_ref.at[bb], lb_ref, ng_ref, o_ref.at[bb], st_ref.at[bb], ot_ref.at[bb], tile)

    @pl.when(i == pl.num_programs(1) - 1)
    def _():
        sfin_ref[...] = st_ref[...]


def _hgrn_rows(hg_ref, lb_ref, ng_ref, o_ref, st_ref, ot_ref, tile):
    n_groups = B_WIDTH // HG_GROUP
    cd = o_ref.dtype
    q = hg_ref[:, 0:B_WIDTH]
    f_logit = hg_ref[:, B_WIDTH:2 * B_WIDTH]
    v_in = hg_ref[:, 2 * B_WIDTH:3 * B_WIDTH].astype(cd)
    lb = lb_ref[...]
    f = lb + (1.0 - lb) * (1.0 / (1.0 + jnp.exp(-f_logit)))
    log_f = jnp.log(f)
    kk = 1.0 - f

    r_t = lax.broadcasted_iota(jnp.int32, (tile, tile), 0)
    c_t = lax.broadcasted_iota(jnp.int32, (tile, tile), 1)
    tri = jnp.where((r_t // CHUNK == c_t // CHUNK) & (c_t <= r_t), 1.0, 0.0).astype(BF16)
    hi, mid, lo = _split3(log_f)
    g_all = _dot(tri, lo) + _dot(tri, mid) + _dot(tri, hi)

    row = lax.broadcasted_iota(jnp.int32, (CHUNK, B_WIDTH), 0)
    upper = row >= HG_HALF
    lane_g = lax.broadcasted_iota(jnp.int32, (CHUNK, HG_GROUP), 1)
    row_g = lax.broadcasted_iota(jnp.int32, (CHUNK, HG_GROUP), 0)
    head_masks = [lane_g // B_KEY_DIM == h for h in range(HG_GROUP // B_KEY_DIM)]
    causal = (lane_g % CHUNK) <= row_g
    r_bd = lax.broadcasted_iota(jnp.int32, (HG_GROUP, HG_GROUP), 0)
    c_bd = lax.broadcasted_iota(jnp.int32, (HG_GROUP, HG_GROUP), 1)
    diag_blocks = (r_bd // B_VAL_DIM) == (c_bd // B_KEY_DIM)

    pending = []
    for c in range(tile // CHUNK):
        rs = slice(CHUNK * c, CHUNK * (c + 1))
        g = g_all[rs]
        qc = q[rs]
        kc = kk[rs]
        vc = v_in[rs]
        g_q1 = g[HG_QUARTER - 1:HG_QUARTER]
        g_mid = g[HG_HALF - 1:HG_HALF]
        g_q3 = g[HG_HALF + HG_QUARTER - 1:HG_HALF + HG_QUARTER]
        g_last = g[CHUNK - 1:CHUNK]
        d_diag = g - jnp.where(upper, g_q3, g_q1)
        d_off = jnp.where(upper, g - g_mid, g_mid - g)
        q_in = (qc * jnp.exp(g)).astype(cd)
        k_st = (kc * jnp.exp(g_last - g)).astype(cd)
        q_diag = qc * jnp.exp(d_diag)
        k_diag = kc * jnp.exp(-d_diag)
        e_off = jnp.exp(d_off)
        zero = jnp.zeros_like(qc)
        q_lo = jnp.where(upper, zero, q_diag).astype(cd)
        q_hi = jnp.where(upper, q_diag, zero).astype(cd)
        q_x = jnp.where(upper, qc * e_off, zero).astype(cd)
        k_lo = jnp.where(upper, zero, k_diag).astype(cd)
        k_hi = jnp.where(upper, k_diag, zero).astype(cd)
        k_x = jnp.where(upper, zero, kc * e_off).astype(cd)
        decay = jnp.exp(g_last)
        for gi in range(n_groups):
            cs = slice(HG_GROUP * gi, HG_GROUP * (gi + 1))
            a_cat = jnp.concatenate([q_lo[:, cs], q_hi[:, cs], q_x[:, cs]], axis=1)
            k_cat = jnp.concatenate([_head_block_diag(k_lo[:, cs], head_masks),
                                     _head_block_diag(k_hi[:, cs], head_masks),
                                     _head_block_diag(k_x[:, cs], head_masks)], axis=1)
            att = _dot_nt(a_cat, k_cat)
            att = jnp.where(causal, att, 0.0).astype(cd)
            v_bd = _head_block_diag(vc[:, cs], head_masks)
            upd = jnp.where(diag_blocks, _dot_tn(vc[:, cs], k_st[:, cs]), 0.0)
            pending.append((rs, cs, gi, q_in[:, cs], _dot(att, v_bd), decay[:, cs], upd))

    states = [st_ref[gi] for gi in range(n_groups)]
    before = []
    for rs, cs, gi, q_in_g, o_intra, decay_g, upd in pending:
        before.append(states[gi].astype(cd))
        states[gi] = states[gi] * decay_g + upd
    for gi in range(n_groups):
        st_ref[gi] = states[gi]
    for (rs, cs, gi, q_in_g, o_intra, decay_g, upd), st_b in zip(pending, before):
        ot_ref[rs, cs] = _dot_nt(q_in_g, st_b) + o_intra

    o = ot_ref[...]
    lane_i = lax.broadcasted_iota(jnp.int32, (B_WIDTH, B_WIDTH), 0)
    lane_j = lax.broadcasted_iota(jnp.int32, (B_WIDTH, B_WIDTH), 1)
    head_ones = jnp.where(lane_i // B_VAL_DIM == lane_j // B_VAL_DIM, 1.0, 0.0).astype(BF16)
    sq_hi, sq_mid, sq_lo = _split3(o * o)
    ms = (_dot(sq_lo, head_ones) + _dot(sq_mid, head_ones) + _dot(sq_hi, head_ones)) * (1.0 / B_VAL_DIM)
    gate = hg_ref[:, 3 * B_WIDTH:4 * B_WIDTH]
    out = o * lax.rsqrt(ms + RMS_EPS) * ng_ref[...] * (gate * _sigmoid(gate))
    o_ref[...] = out.astype(cd)


def _hgrn(hg, lb, norm_g, s0_bd, batch, seq, tile, cd):
    nt = seq // tile
    n_groups = B_WIDTH // HG_GROUP
    par = HG_SEQS_PER_STEP
    assert batch % par == 0
    st_blk = (par, n_groups, HG_GROUP, HG_GROUP)
    ob, s_fin = pl.pallas_call(
        functools.partial(_hgrn_kernel, tile=tile),
        grid=(batch // par, nt),
        in_specs=[pl.BlockSpec((par, tile, 4 * B_WIDTH), lambda b, i: (b, i, 0)),
                  _const_spec((1, B_WIDTH)), _const_spec((1, B_WIDTH)),
                  pl.BlockSpec(st_blk, lambda b, i: (b, 0, 0, 0))],
        out_specs=[pl.BlockSpec((par, tile, B_WIDTH), lambda b, i: (b, i, 0)),
                   pl.BlockSpec(st_blk, lambda b, i: (b, 0, 0, 0))],
        out_shape=[jax.ShapeDtypeStruct((batch, seq, B_WIDTH), cd),
                   jax.ShapeDtypeStruct((batch,) + st_blk[1:], F32)],
        scratch_shapes=[pltpu.VMEM(st_blk, F32), pltpu.VMEM((par, tile, B_WIDTH), F32)],
        compiler_params=_params("arbitrary", "arbitrary"),
        name="hgrn",
    )(hg.reshape(batch, seq, 4 * B_WIDTH), lb, norm_g, s0_bd)
    return ob.reshape(batch * seq, B_WIDTH), s_fin


def _state_to_block_diag(s):
    b = s.shape[0]
    hpg = HG_GROUP // B_KEY_DIM
    st = s.astype(F32).reshape(b, B_HEADS // hpg, hpg, B_KEY_DIM, B_VAL_DIM).transpose(0, 1, 2, 4, 3)
    bd = jnp.einsum('bghvc,hk->bghvkc', st, jnp.eye(hpg, dtype=F32))
    return bd.reshape(b, B_HEADS // hpg, HG_GROUP, HG_GROUP)


def _block_diag_to_state(bd):
    b = bd.shape[0]
    hpg = HG_GROUP // B_KEY_DIM
    x = bd.reshape(b, B_HEADS // hpg, hpg, B_VAL_DIM, hpg, B_KEY_DIM)
    st = jnp.einsum('bghvkc,hk->bghvc', x, jnp.eye(hpg, dtype=F32))
    return st.transpose(0, 1, 2, 4, 3).reshape(b, B_HEADS, B_KEY_DIM, B_VAL_DIM)


def _rglru_kernel(rg_ref, conv0_ref, h0_ref, cw_ref, cb_ref, wa_ref, ba_ref, wx_ref, bx_ref, lam_ref,
                  o_ref, hlast_ref, xbuf_ref, hc_ref, *, tile, at_start):
    i = pl.program_id(1)
    pad = 8

    @pl.when(i == 0)
    def _():
        xbuf_ref[0:pad] = conv0_ref[0]
        hc_ref[...] = h0_ref[0]

    xr = rg_ref[:, 0:C_WIDTH]
    gate = rg_ref[:, C_WIDTH:2 * C_WIDTH]
    xbuf_ref[pad:pad + tile] = xr
    xc = cb_ref[...] + cw_ref[C_CONV - 1:C_CONV] * xr
    for j in range(1, C_CONV):
        xc = xc + cw_ref[C_CONV - 1 - j:C_CONV - j] * xbuf_ref[pad - j:pad - j + tile]
    xbuf_ref[0:pad] = xbuf_ref[tile:tile + pad]

    xcb = xc.astype(wa_ref.dtype)
    r = _sigmoid(_dot(xcb, wa_ref[...]) + ba_ref[...])
    ig = _sigmoid(_dot(xcb, wx_ref[...]) + bx_ref[...])
    neg_lam = -lam_ref[...]
    softplus = jnp.maximum(neg_lam, 0.0) + jnp.log(1.0 + jnp.exp(-jnp.abs(neg_lam)))
    a = jnp.exp(r * (-C_GATE_C * softplus))
    mult = jnp.sqrt(1.0 - a * a)
    row = lax.broadcasted_iota(jnp.int32, (tile, C_WIDTH), 0)
    if at_start:
        mult = jnp.where((row == 0) & (i == 0), 1.0, mult)
    b = mult * ig * xc

    d = 1
    while d < tile:
        a_sh = pltpu.roll(a, d, 0)
        b_sh = pltpu.roll(b, d, 0)
        keep = row >= d
        b = jnp.where(keep, a * b_sh + b, b)
        a = jnp.where(keep, a * a_sh, a)
        d *= 2
    h = a * hc_ref[...] + b
    hc_ref[...] = h[tile - 1:tile]
    hlast_ref[0] = h[tile - 1:tile]
    gelu = 0.5 * gate * (1.0 + jnp.tanh(np.sqrt(2.0 / np.pi).astype(np.float32) * (gate + 0.044715 * gate * gate * gate)))
    o_ref[...] = (h * gelu).astype(o_ref.dtype)


def _rglru(rg, conv0_pad, h0, w, batch, seq, tile, at_start):
    nt = seq // tile
    vec = _const_spec((1, C_WIDTH))
    return pl.pallas_call(
        functools.partial(_rglru_kernel, tile=tile, at_start=at_start),
        grid=(batch, nt),
        in_specs=[pl.BlockSpec((tile, 2 * C_WIDTH), lambda b, i: (b * nt + i, 0)),
                  pl.BlockSpec((1, 8, C_WIDTH), lambda b, i: (b, 0, 0)),
                  pl.BlockSpec((1, 1, C_WIDTH), lambda b, i: (b, 0, 0)),
                  _const_spec((C_CONV, C_WIDTH)), vec,
                  _const_spec((C_WIDTH, C_WIDTH)), vec, _const_spec((C_WIDTH, C_WIDTH)), vec, vec],
        out_specs=[pl.BlockSpec((tile, C_WIDTH), lambda b, i: (b * nt + i, 0)),
                   pl.BlockSpec((1, 1, C_WIDTH), lambda b, i: (b, 0, 0))],
        out_shape=[jax.ShapeDtypeStruct((batch * seq, C_WIDTH), w['wa_bd'].dtype),
                   jax.ShapeDtypeStruct((batch, 1, C_WIDTH), F32)],
        scratch_shapes=[pltpu.VMEM((tile + 8, C_WIDTH), F32), pltpu.VMEM((1, C_WIDTH), F32)],
        compiler_params=_params("arbitrary", "arbitrary"),
        name="rglru",
    )(rg, conv0_pad, h0, w['conv_w'], w['conv_b'], w['wa_bd'], w['ba'], w['wx_bd'], w['bx'], w['lam'])


def _block_diag_weight(w):
    n, d, e = w.shape
    return jnp.einsum('nde,nm->ndme', w, jnp.eye(n, dtype=w.dtype)).reshape(n * d, n * e)


def _route(logits_t):
    m = jnp.max(logits_t, axis=0, keepdims=True)
    e = jnp.exp(logits_t - m)
    p = e / jnp.sum(e, axis=0, keepdims=True)
    rows = [p[j:j + 1] for j in range(N_EXPERTS)]
    scores = []
    for g in range(N_GROUPS):
        mem = rows[GROUP_SIZE * g:GROUP_SIZE * (g + 1)]
        best = None
        for a in range(GROUP_SIZE):
            for b in range(a + 1, GROUP_SIZE):
                pair = mem[a] + mem[b]
                best = pair if best is None else jnp.maximum(best, pair)
        scores.append(best)
    smax = functools.reduce(jnp.maximum, scores)
    taken = jnp.zeros_like(smax)
    sel = []
    for g in range(N_GROUPS):
        hit = jnp.where(scores[g] == smax, 1.0, 0.0) * (1.0 - taken)
        taken = taken + hit
        sel.append(hit)
    picked = []
    for j in range(N_EXPERTS):
        g = j // GROUP_SIZE
        rank = jnp.zeros_like(smax)
        for o in range(GROUP_SIZE * g, GROUP_SIZE * (g + 1)):
            if o == j:
                continue
            ahead = (rows[o] >= rows[j]) if o < j else (rows[o] > rows[j])
            rank = rank + jnp.where(ahead, 1.0, 0.0)
        picked.append(sel[g] * jnp.where(rank < float(2), 1.0, 0.0))
    denom = functools.reduce(lambda x, y: x + y, [picked[j] * rows[j] for j in range(N_EXPERTS)])
    comb = [picked[j] * rows[j] / denom for j in range(N_EXPERTS)]
    out = [functools.reduce(lambda x, y: x + y, [sel[g] * comb[GROUP_SIZE * g + m] for g in range(N_GROUPS)])
           for m in range(GROUP_SIZE)]
    out.append(functools.reduce(lambda x, y: x + y, [float(g) * sel[g] for g in range(1, N_GROUPS)]))
    out.append(jnp.zeros((ROUTE_ROWS - GROUP_SIZE - 1,) + smax.shape[1:], F32))
    return jnp.concatenate(out, axis=0)


def _mix_kernel(x_ref, oa_ref, ob_ref, oc_ref, gl_ref, mk_ref, mv_ref, wb_ref, wout_ref, wq_ref, wo_ref,
                g1_ref, b1_ref, g2_ref, b2_ref, rt_ref, x2_ref, route_ref, *, sub):
    tiles = [slice(s0, s0 + sub) for s0 in range(0, x_ref.shape[0], sub)]
    cd = wb_ref.dtype

    mixed = []
    for rows in tiles:
        acc = None
        for b, o_ref in enumerate((oa_ref, ob_ref, oc_ref)):
            per_branch = _dot(o_ref[rows], wb_ref[b])
            gate = _sigmoid(gl_ref[rows, D_MODEL * b:D_MODEL * (b + 1)].astype(F32))
            acc = gate * per_branch if acc is None else acc + gate * per_branch
        mixed.append(acc.astype(cd))

    x1 = [_layer_norm(DN_ALPHA * x_ref[rows] + _dot(m, wout_ref[...]), g1_ref[...], b1_ref[...])
          for rows, m in zip(tiles, mixed)]
    q = [(_dot(t.astype(cd), wq_ref[...]) * (X_HEAD_DIM ** -0.5)).astype(cd) for t in x1]

    heads = [[] for _ in tiles]
    for h in range(X_HEADS):
        sl = slice(X_HEAD_DIM * h, X_HEAD_DIM * (h + 1))
        scores = [_dot_nt(qt[:, sl], mk_ref[0, :, sl]) for qt in q]
        for t, s in enumerate(scores):
            m = jnp.max(s, axis=-1, keepdims=True)
            e = jnp.exp(s - m)
            l = jnp.sum(e, axis=-1, keepdims=True)
            heads[t].append((_dot(e.astype(cd), mv_ref[0, :, sl]) * (1.0 / l)).astype(cd))

    attn = [_dot(jnp.concatenate(hs, axis=-1), wo_ref[...]) for hs in heads]
    x2 = [_layer_norm(DN_ALPHA * a + b, g2_ref[...], b2_ref[...]) for a, b in zip(x1, attn)]

    r_hi, r_lo = _split2(rt_ref[...])
    eye_r = lax.broadcasted_iota(jnp.int32, (ROUTE_ROWS, ROUTE_LANES), 0)
    eye_c = lax.broadcasted_iota(jnp.int32, (ROUTE_ROWS, ROUTE_LANES), 1)
    eye = jnp.where(eye_r == eye_c, 1.0, 0.0).astype(BF16)
    for rows, t in zip(tiles, x2):
        x2_ref[rows] = t
        x_hi, x_lo = _split2(t)
        logits_t = _dot_nt(r_lo, x_hi) + _dot_nt(r_hi, x_lo) + _dot_nt(r_hi, x_hi)
        hi, mid, lo = _split3(_route(logits_t))
        route_ref[rows] = _dot_tn(hi, eye) + _dot_tn(mid, eye) + _dot_tn(lo, eye)


def _mix(x, oa, ob, oc, gl, mk, mv, w, batch, seq):
    sub = min(MIX_TILE, seq)
    tm = min(2 * MIX_TILE, seq)
    nt = seq // tm
    row = lambda b, i: (b * nt + i, 0)
    vec = _const_spec((1, D_MODEL))
    n_mem = mk.shape[1]
    mem = pl.BlockSpec((1, n_mem, D_MODEL), lambda b, i: (b, 0, 0))
    sq = _const_spec((D_MODEL, D_MODEL))
    return pl.pallas_call(
        functools.partial(_mix_kernel, sub=sub),
        grid=(batch, nt),
        in_specs=[pl.BlockSpec((tm, D_MODEL), row)] + [pl.BlockSpec((tm, A_WIDTH), row)] * 3
        + [pl.BlockSpec((tm, N_BRANCH * D_MODEL), row), mem, mem,
           _const_spec((N_BRANCH, A_WIDTH, D_MODEL)), sq, sq, sq, vec, vec, vec, vec,
           _const_spec((N_EXPERTS, D_MODEL))],
        out_specs=[pl.BlockSpec((tm, D_MODEL), row), pl.BlockSpec((tm, ROUTE_LANES), row)],
        out_shape=[jax.ShapeDtypeStruct((batch * seq, D_MODEL), F32),
                   jax.ShapeDtypeStruct((batch * seq, ROUTE_LANES), F32)],
        compiler_params=_params("arbitrary", "arbitrary"),
        name="mix",
    )(x, oa, ob, oc, gl, mk, mv, w['w_branch'], w['w_out'], w['xa_wq'], w['xa_wo'],
      w['ln1_g'], w['ln1_b'], w['ln2_g'], w['ln2_b'], w['router_t'])


def _sc_gather_rows(tables, idx):
    n_out = idx.shape[0]
    per_worker = n_out // SC_WORKERS
    window = min(SC_GATHER_WINDOW, per_worker)
    steps = per_worker // window
    assert per_worker * SC_WORKERS == n_out and steps * window == per_worker and window % 8 == 0
    idx3 = idx.astype(jnp.int32).reshape(SC_WORKERS, steps, window)
    mesh = plsc.VectorSubcoreMesh(core_axis_name="core", subcore_axis_name="subcore")
    n_tab = len(tables)

    def body(*refs):
        tab_hbm = refs[:n_tab]
        idx_hbm = refs[n_tab]
        out_hbm = refs[n_tab + 1:2 * n_tab + 1]
        idx_v = refs[2 * n_tab + 1]
        rows_v = refs[2 * n_tab + 2:3 * n_tab + 2]
        sem = refs[3 * n_tab + 2]
        wid = lax.axis_index("subcore") * SC_CORES + lax.axis_index("core")
        pltpu.sync_copy(idx_hbm.at[wid], idx_v)

        @pl.loop(0, steps)
        def _(j):
            base = wid * per_worker + j * window
            for k in range(n_tab):
                pltpu.async_copy(tab_hbm[k].at[idx_v.at[j]], rows_v[k], sem).wait()
                pltpu.sync_copy(rows_v[k], out_hbm[k].at[pl.ds(base, window)])

    call = pl.kernel(
        body,
        out_type=[jax.ShapeDtypeStruct((n_out, t.shape[1]), t.dtype) for t in tables],
        mesh=mesh,
        scratch_types=[pltpu.VMEM((steps, window), jnp.int32)]
        + [pltpu.VMEM((window, t.shape[1]), t.dtype) for t in tables] + [pltpu.SemaphoreType.DMA],
        name="sc_gather",
    )
    return call(*tables, idx3)


def _dispatch_plan(gid, tm):
    n = gid.shape[0]
    n_pad = n + N_GROUPS * tm
    groups = jnp.arange(N_GROUPS, dtype=jnp.int32)
    onehot = (gid[:, None] == groups[None, :]).astype(jnp.int32)
    counts = jnp.sum(onehot, axis=0)
    csum = jnp.cumsum(onehot, axis=0)
    rank = jnp.sum((csum - onehot) * onehot, axis=1)
    padded = ((counts + tm - 1) // tm) * tm
    start_p = jnp.cumsum(padded) - padded
    start_u = jnp.cumsum(counts) - counts
    pos = jnp.sum(onehot * start_p[None, :], axis=1) + rank
    order = jnp.argsort(gid, stable=True).astype(jnp.int32)
    slot = jnp.arange(n_pad, dtype=jnp.int32)
    g_slot = jnp.minimum(jnp.sum((slot[:, None] >= (start_p + padded)[None, :]).astype(jnp.int32), axis=1),
                         N_GROUPS - 1)
    oh_slot = (g_slot[:, None] == groups[None, :]).astype(jnp.int32)
    r_slot = slot - jnp.sum(oh_slot * start_p[None, :], axis=1)
    valid = r_slot < jnp.sum(oh_slot * counts[None, :], axis=1)
    s = jnp.where(valid, jnp.sum(oh_slot * start_u[None, :], axis=1) + r_slot, 0)
    src = jnp.take(order, s)
    return pos.astype(jnp.int32), src, g_slot[::tm]


def _moe_kernel(tg_ref, x_ref, r_ref, w1_ref, w3_ref, w2_ref, g_ref, b_ref, o_ref):
    del tg_ref
    x = x_ref[...]
    cd = w1_ref.dtype
    xb = x.astype(cd)
    r = r_ref[...]
    y = None
    for m in range(GROUP_SIZE):
        h1 = _dot(xb, w1_ref[m])
        h3 = _dot(xb, w3_ref[m])
        h = (h1 * _sigmoid(h1) * h3).astype(cd)
        ym = r[:, m:m + 1] * _dot(h, w2_ref[m])
        y = ym if y is None else y + ym
    o_ref[...] = _layer_norm(DN_ALPHA * x + y, g_ref[...], b_ref[...])


def _moe_sorted(xs, rs, tile_gid, w, tm):
    n_pad = xs.shape[0]
    row = lambda i, tg: (i, 0)
    grp = lambda i, tg: (tg[i], 0, 0)
    vec = pl.BlockSpec((1, D_MODEL), lambda i, tg: (0, 0))
    mode = dict(pipeline_mode=pl.Buffered(1)) if w['w1'].dtype == F32 else {}
    return pl.pallas_call(
        _moe_kernel,
        grid_spec=pltpu.PrefetchScalarGridSpec(
            num_scalar_prefetch=1,
            grid=(n_pad // tm,),
            in_specs=[pl.BlockSpec((tm, D_MODEL), row), pl.BlockSpec((tm, ROUTE_LANES), row),
                      pl.BlockSpec((GROUP_SIZE, D_MODEL, D_EXPERT), grp, **mode),
                      pl.BlockSpec((GROUP_SIZE, D_MODEL, D_EXPERT), grp, **mode),
                      pl.BlockSpec((GROUP_SIZE, D_EXPERT, D_MODEL), grp, **mode), vec, vec],
            out_specs=pl.BlockSpec((tm, D_MODEL), row),
        ),
        out_shape=jax.ShapeDtypeStruct((n_pad, D_MODEL), F32),
        compiler_params=_params("arbitrary"),
        name="moe",
    )(tile_gid, xs, rs, w['w1'], w['w3'], w['w2'], w['ln3_g'], w['ln3_b'])


def _moe(x2, route, w):
    n = x2.shape[0]
    tm = min(MOE_TILE, n // N_GROUPS)
    gid = route[:, GROUP_SIZE].astype(jnp.int32)
    pos, src, tile_gid = _dispatch_plan(gid, tm)
    xs, rs = _sc_gather_rows([x2, route], src)
    ys = _moe_sorted(xs, rs, tile_gid, w, tm)
    return _sc_gather_rows([ys], pos)[0]


def _trunk_layer(x, w, mem_k, mem_v, cache_k, cache_v, s0, h0, conv0, batch, seq, prompt):
    qkv, hg, rg, gl = _inproj(x, w['w_in'])
    keep = min(A_WINDOW, seq)
    x_keep = x.reshape(batch, seq, D_MODEL)[:, seq - keep:].reshape(batch * keep, D_MODEL)
    kv32 = _matmul(x_keep, w['w_in'][:, A_WIDTH:3 * A_WIDTH]).reshape(batch, keep, 2, A_HEADS, A_HEAD_DIM)
    if prompt:
        oa = _attn_prompt(qkv, w['band_bias'], batch, seq)
        new_k, new_v = kv32[:, :, 0], kv32[:, :, 1]
        tile = 256
    else:
        win = cache_k.shape[1]
        ck = cache_k.reshape(batch, win, A_WIDTH)
        cv = cache_v.reshape(batch, win, A_WIDTH)
        oa = _attn_sample(qkv, ck, cv, w['band_bias'][:, :seq, :win + 2 * CHUNK], batch, seq)
        new_k = jnp.concatenate([cache_k, kv32[:, :, 0]], axis=1)[:, seq:]
        new_v = jnp.concatenate([cache_v, kv32[:, :, 1]], axis=1)[:, seq:]
        tile = seq
    ob, s_bd = _hgrn(hg, w['lb'], w['hgrn_g'], _state_to_block_diag(s0), batch, seq, tile, w['w_in'].dtype)
    conv0_pad = jnp.concatenate([jnp.zeros((batch, 8 - (C_CONV - 1), C_WIDTH), F32), conv0.astype(F32)], axis=1)
    oc, h_new = _rglru(rg, conv0_pad, h0.astype(F32).reshape(batch, 1, C_WIDTH), w, batch, seq, tile, prompt)
    conv_new = jnp.concatenate([conv0.astype(F32), rg.reshape(batch, seq, 2 * C_WIDTH)[:, :, :C_WIDTH]],
                               axis=1)[:, seq:]
    x2, route = _mix(x, oa, ob, oc, gl, mem_k, mem_v, w, batch, seq)
    x3 = _moe(x2, route, w)
    return x3, (new_k, new_v, _block_diag_to_state(s_bd), h_new.reshape(batch, C_WIDTH), conv_new)


def kernel(x_prompt, x_sample, cache_attn_k, cache_attn_v, state_hgrn, state_rglru, state_conv, cache_mem_k, cache_mem_v, mem_prompt, w_in, attn_rel_bias, hgrn_lb_logits, hgrn_norm_g, rg_conv_w, rg_conv_b, rg_wa, rg_ba, rg_wx, rg_bx, rg_lambda, w_branch, w_out, ln1_g, ln1_b, xa_wq, xa_wk, xa_wv, xa_wo, ln2_g, ln2_b, moe_router, moe_w1, moe_w3, moe_w2, ln3_g, ln3_b):
    bp, tp, _ = x_prompt.shape
    bs, ts, _ = x_sample.shape
    n_mem = mem_prompt.shape[1]
    depth = w_in.shape[0]

    p = jax.nn.softmax(hgrn_lb_logits.astype(F32), axis=0)
    lb_all = jnp.cumsum(p, axis=0) - p[0:1]
    vec = lambda t: t.astype(F32).reshape(1, -1)

    xp = x_prompt.reshape(bp * tp, D_MODEL)
    xs = x_sample.reshape(bs * ts, D_MODEL)
    mem2d = mem_prompt.reshape(bp * n_mem, D_MODEL)
    s0p = jnp.zeros((bp, B_HEADS, B_KEY_DIM, B_VAL_DIM), F32)
    h0p = jnp.zeros((bp, C_WIDTH), F32)
    conv0p = jnp.zeros((bp, C_CONV - 1, C_WIDTH), F32)
    router_t = moe_router.astype(F32).T

    outs_p, outs_s, mem_ks, mem_vs = [], [], [], []
    for l in range(depth):
        wf = {
            'w_in': w_in[l], 'wa_bd': _block_diag_weight(rg_wa[l]), 'wx_bd': _block_diag_weight(rg_wx[l]),
            'w_branch': w_branch[l], 'w_out': w_out[l], 'xa_wq': xa_wq[l], 'xa_wo': xa_wo[l],
            'w1': moe_w1[l], 'w3': moe_w3[l], 'w2': moe_w2[l],
        }
        shared = {
            'band_bias': _band_bias(attn_rel_bias[l]),
            'lb': vec(lb_all[l]), 'hgrn_g': vec(jnp.tile(hgrn_norm_g[l], B_HEADS)),
            'conv_w': rg_conv_w[l].astype(F32), 'conv_b': vec(rg_conv_b[l]),
            'ba': vec(rg_ba[l]), 'bx': vec(rg_bx[l]), 'lam': vec(rg_lambda[l]),
            'ln1_g': vec(ln1_g[l]), 'ln1_b': vec(ln1_b[l]), 'ln2_g': vec(ln2_g[l]), 'ln2_b': vec(ln2_b[l]),
            'ln3_g': vec(ln3_g[l]), 'ln3_b': vec(ln3_b[l]), 'router_t': router_t,
        }
        w_fast = dict(shared, **{k: v.astype(BF16) for k, v in wf.items()})
        w_precise = dict(shared, **{k: v.astype(F32) for k, v in wf.items()})
        mk_p = _matmul(mem2d, xa_wk[l].astype(BF16))
        mv_p = _matmul(mem2d, xa_wv[l].astype(BF16))
        mem_ks.append(mk_p.reshape(bp, n_mem, X_HEADS, X_HEAD_DIM))
        mem_vs.append(mv_p.reshape(bp, n_mem, X_HEADS, X_HEAD_DIM))
        xp, st_p = _trunk_layer(xp, w_fast, mk_p.reshape(bp, n_mem, D_MODEL).astype(BF16),
                                mv_p.reshape(bp, n_mem, D_MODEL).astype(BF16),
                                None, None, s0p, h0p, conv0p, bp, tp, True)
        xs, st_s = _trunk_layer(xs, w_precise, cache_mem_k[l].reshape(bs, n_mem, D_MODEL).astype(F32),
                                cache_mem_v[l].reshape(bs, n_mem, D_MODEL).astype(F32),
                                cache_attn_k[l], cache_attn_v[l], state_hgrn[l], state_rglru[l], state_conv[l],
                                bs, ts, False)
        outs_p.append(st_p)
        outs_s.append(st_s)

    stack = lambda items, j: jnp.stack([it[j] for it in items])
    return (xp.reshape(bp, tp, D_MODEL), xs.reshape(bs, ts, D_MODEL),
            stack(outs_p, 0), stack(outs_p, 1), stack(outs_p, 2), stack(outs_p, 3), stack(outs_p, 4),
            jnp.stack(mem_ks), jnp.stack(mem_vs),
            stack(outs_s, 0), stack(outs_s, 1), stack(outs_s, 2), stack(outs_s, 3), stack(outs_s, 4))
```

```python
import functools

import numpy as np
import jax
import jax.numpy as jnp
from jax import lax
from jax.experimental import pallas as pl
from jax.experimental.pallas import tpu as pltpu
from jax.experimental.pallas import tpu_sc as plsc

F32 = jnp.float32
BF16 = jnp.bfloat16

D_MODEL = 1024
CHUNK = 64
A_HEADS = 8
A_HEAD_DIM = 64
A_WIDTH = A_HEADS * A_HEAD_DIM
A_PAST_CHUNKS = 8
A_WINDOW = A_PAST_CHUNKS * CHUNK
A_MAX_REL = 256
B_HEADS = 8
B_KEY_DIM = 64
B_VAL_DIM = 64
B_WIDTH = B_HEADS * B_KEY_DIM
C_WIDTH = 512
C_CONV = 4
C_GATE_C = 8.0
N_BRANCH = 3
IN_COLS = 3 * A_WIDTH + 4 * B_WIDTH + 2 * C_WIDTH + N_BRANCH * D_MODEL
X_HEADS = 4
X_HEAD_DIM = D_MODEL // X_HEADS
N_EXPERTS = 16
N_GROUPS = 4
GROUP_SIZE = N_EXPERTS // N_GROUPS
D_EXPERT = D_MODEL // 2
DEPTH = 4
DN_ALPHA = (2 * DEPTH) ** 0.25
LN_EPS = 1e-5
RMS_EPS = 1e-6
NEG_INF = -1e30

VMEM_LIMIT_BYTES = 56 * 1024 * 1024

INPROJ_TILE = 512
MIX_TILE = 256
ATTN_Q_TILE = 256
ATTN_K_TILE = 256
HG_GROUP = 256
HG_SEQS_PER_STEP = 2
HG_HALF = CHUNK // 2
HG_QUARTER = CHUNK // 4
ROUTE_ROWS = 16
ROUTE_LANES = 128
MOE_TILE = 512
SC_CORES = 2
SC_SUBCORES = 16
SC_WORKERS = SC_CORES * SC_SUBCORES
SC_GATHER_WINDOW = 32


def _split2(x):
    hi = x.astype(BF16)
    lo = (x - hi.astype(F32)).astype(BF16)
    return hi, lo


def _contract(a, b, dims):
    dg = lambda u, v: lax.dot_general(u, v, (dims, ((), ())), preferred_element_type=F32)
    if a.dtype == F32:
        a_hi, a_lo = _split2(a)
        b_hi, b_lo = _split2(b)
        return dg(a_lo, b_hi) + dg(a_hi, b_lo) + dg(a_hi, b_hi)
    return dg(a, b)


def _dot(a, b):
    return _contract(a, b, ((1,), (0,)))


def _dot_nt(a, b):
    return _contract(a, b, ((1,), (1,)))


def _dot_tn(a, b):
    return _contract(a, b, ((0,), (0,)))


def _split3(x):
    hi = x.astype(BF16)
    r1 = x - hi.astype(F32)
    mid = r1.astype(BF16)
    lo = (r1 - mid.astype(F32)).astype(BF16)
    return hi, mid, lo


def _sigmoid(x):
    return 0.5 * jnp.tanh(0.5 * x) + 0.5


def _layer_norm(x, g, b):
    mu = jnp.mean(x, axis=-1, keepdims=True)
    xc = x - mu
    var = jnp.mean(xc * xc, axis=-1, keepdims=True)
    return xc * lax.rsqrt(var + LN_EPS) * g + b


def _params(*semantics):
    return pltpu.CompilerParams(dimension_semantics=semantics, vmem_limit_bytes=VMEM_LIMIT_BYTES)


def _const_spec(shape):
    nd = len(shape)
    return pl.BlockSpec(shape, lambda *_: (0,) * nd, pipeline_mode=pl.Buffered(1))


def _layer_spec(shape, layer):
    nd = len(shape)
    return pl.BlockSpec((None,) + tuple(shape), lambda *_: (layer,) + (0,) * nd, pipeline_mode=pl.Buffered(1))


def _inproj_kernel(x_ref, w_ref, qkv_ref, hg_ref, rg_ref, gl_ref):
    xb = x_ref[...].astype(w_ref.dtype)
    cw = 512

    def mm(c0):
        return _dot(xb, w_ref[:, c0:c0 + cw])

    for j in range(3):
        qkv_ref[:, cw * j:cw * (j + 1)] = mm(cw * j).astype(qkv_ref.dtype)
    base = 3 * A_WIDTH
    for j in range(4):
        hg_ref[:, cw * j:cw * (j + 1)] = mm(base + cw * j)
    base += 4 * B_WIDTH
    for j in range(2):
        rg_ref[:, cw * j:cw * (j + 1)] = mm(base + cw * j)
    base += 2 * C_WIDTH
    for j in range(N_BRANCH * D_MODEL // cw):
        gl_ref[:, cw * j:cw * (j + 1)] = mm(base + cw * j).astype(gl_ref.dtype)


def _inproj(x, w_in, layer):
    n = x.shape[0]
    cd = w_in.dtype
    tm = INPROJ_TILE if cd == BF16 else INPROJ_TILE // 2
    assert n % tm == 0
    row = lambda i: (i, 0)
    return pl.pallas_call(
        _inproj_kernel,
        grid=(n // tm,),
        in_specs=[pl.BlockSpec((tm, D_MODEL), row), _layer_spec((D_MODEL, IN_COLS), layer)],
        out_specs=[pl.BlockSpec((tm, 3 * A_WIDTH), row),
                   pl.BlockSpec((tm, 4 * B_WIDTH), row), pl.BlockSpec((tm, 2 * C_WIDTH), row),
                   pl.BlockSpec((tm, N_BRANCH * D_MODEL), row)],
        out_shape=[jax.ShapeDtypeStruct((n, 3 * A_WIDTH), cd),
                   jax.ShapeDtypeStruct((n, 4 * B_WIDTH), F32), jax.ShapeDtypeStruct((n, 2 * C_WIDTH), F32),
                   jax.ShapeDtypeStruct((n, N_BRANCH * D_MODEL), cd)],
        compiler_params=_params("arbitrary"),
        name="inproj",
    )(x, w_in)


def _matmul_kernel(x_ref, w_ref, o_ref):
    o_ref[...] = _dot(x_ref[...].astype(w_ref.dtype), w_ref[...])


def _matmul(x, w, layer):
    n, k = x.shape
    m = w.shape[2]
    tm = 256
    return pl.pallas_call(
        _matmul_kernel,
        grid=(n // tm,),
        in_specs=[pl.BlockSpec((tm, k), lambda i: (i, 0)), _layer_spec((k, m), layer)],
        out_specs=pl.BlockSpec((tm, m), lambda i: (i, 0)),
        out_shape=jax.ShapeDtypeStruct((n, m), F32),
        compiler_params=_params("arbitrary"),
        name="matmul",
    )(x, w)


def _kv_kernel(x_ref, wk_ref, wv_ref, k_ref, v_ref):
    xb = x_ref[...].astype(wk_ref.dtype)
    k_ref[...] = _dot(xb, wk_ref[...])
    v_ref[...] = _dot(xb, wv_ref[...])


def _kv_rows(x, w_in, layer):
    n = x.shape[0]
    tm = 256
    col = lambda j: pl.BlockSpec((None, D_MODEL, A_WIDTH), lambda i: (layer, 0, j), pipeline_mode=pl.Buffered(1))
    out = pl.BlockSpec((tm, A_WIDTH), lambda i: (i, 0))
    return pl.pallas_call(
        _kv_kernel,
        grid=(n // tm,),
        in_specs=[pl.BlockSpec((tm, D_MODEL), lambda i: (i, 0)), col(1), col(2)],
        out_specs=[out, out],
        out_shape=[jax.ShapeDtypeStruct((n, A_WIDTH), F32)] * 2,
        compiler_params=_params("arbitrary"),
        name="kv_rows",
    )(x, w_in, w_in)


def _attn_core(q, k, v, bias_ref, valid):
    rows = q.shape[0]
    lane = lax.broadcasted_iota(jnp.int32, (rows, 2 * A_HEAD_DIM), 1)
    first = lane < A_HEAD_DIM
    q = q * jnp.asarray(A_HEAD_DIM ** -0.5, q.dtype)

    def scores(head):
        sl = slice(2 * A_HEAD_DIM * (head // 2), 2 * A_HEAD_DIM * (head // 2 + 1))
        sel = first if head % 2 == 0 else jnp.logical_not(first)
        qm = jnp.where(sel, q[:, sl], jnp.zeros_like(q[:, sl]))
        s = _dot_nt(qm, k[:, sl]) + bias_ref[head]
        return s if valid is None else jnp.where(valid, s, NEG_INF)

    outs = []
    pair = None
    s_next = scores(0)
    for head in range(A_HEADS):
        s = s_next
        if head + 1 < A_HEADS:
            s_next = scores(head + 1)
        m = jnp.max(s, axis=-1, keepdims=True)
        e = jnp.exp(s - m)
        l = jnp.sum(e, axis=-1, keepdims=True)
        sl = slice(2 * A_HEAD_DIM * (head // 2), 2 * A_HEAD_DIM * (head // 2 + 1))
        o = _dot(e.astype(v.dtype), v[:, sl]) * (1.0 / l)
        if head % 2 == 0:
            pair = o
        else:
            outs.append(jnp.where(first, pair, o))
    return jnp.concatenate(outs, axis=-1)


def _attn_prompt_kernel(q_ref, k0_ref, k1_ref, k2_ref, v0_ref, v1_ref, v2_ref, bias_ref, o_ref):
    i = pl.program_id(1)

    def run(masked):
        k = jnp.concatenate([k0_ref[...], k1_ref[...], k2_ref[...]], axis=0)
        v = jnp.concatenate([v0_ref[...], v1_ref[...], v2_ref[...]], axis=0)
        valid = None
        if masked:
            col = lax.broadcasted_iota(jnp.int32, (ATTN_Q_TILE, 3 * ATTN_K_TILE), 1)
            valid = col >= (2 - i) * ATTN_K_TILE
        o_ref[...] = _attn_core(q_ref[...], k, v, bias_ref, valid).astype(o_ref.dtype)

    pl.when(i < 2)(lambda: run(True))
    pl.when(i >= 2)(lambda: run(False))


def _attn_prompt(qkv, bias, batch, seq):
    nt = seq // ATTN_Q_TILE
    blk = (ATTN_Q_TILE, A_WIDTH)

    def kv_spec(j, col):
        return pl.BlockSpec(blk, lambda b, i: (b * nt + jnp.maximum(i - 2 + j, 0), col))

    return pl.pallas_call(
        _attn_prompt_kernel,
        grid=(batch, nt),
        in_specs=[pl.BlockSpec(blk, lambda b, i: (b * nt + i, 0))]
        + [kv_spec(j, 1) for j in range(3)] + [kv_spec(j, 2) for j in range(3)]
        + [_const_spec(bias.shape)],
        out_specs=pl.BlockSpec(blk, lambda b, i: (b * nt + i, 0)),
        out_shape=jax.ShapeDtypeStruct((batch * seq, A_WIDTH), qkv.dtype),
        compiler_params=_params("arbitrary", "arbitrary"),
        name="attn_prompt",
    )(qkv, qkv, qkv, qkv, qkv, qkv, qkv, bias)


def _attn_sample_kernel(q_ref, kn_ref, vn_ref, ck_ref, cv_ref, bias_ref, o_ref):
    cd = q_ref.dtype
    pad = jnp.zeros((CHUNK, A_WIDTH), cd)
    k = jnp.concatenate([ck_ref[0].astype(cd), kn_ref[...], pad], axis=0)
    v = jnp.concatenate([cv_ref[0].astype(cd), vn_ref[...], pad], axis=0)
    o_ref[...] = _attn_core(q_ref[...], k, v, bias_ref, None).astype(cd)


def _attn_sample(qkv, cache_k, cache_v, bias, batch, seq):
    win = cache_k.shape[1]
    blk = (seq, A_WIDTH)
    return pl.pallas_call(
        _attn_sample_kernel,
        grid=(batch,),
        in_specs=[pl.BlockSpec(blk, lambda b: (b, 0)), pl.BlockSpec(blk, lambda b: (b, 1)),
                  pl.BlockSpec(blk, lambda b: (b, 2)),
                  pl.BlockSpec((1, win, A_WIDTH), lambda b: (b, 0, 0)),
                  pl.BlockSpec((1, win, A_WIDTH), lambda b: (b, 0, 0)),
                  _const_spec(bias.shape)],
        out_specs=pl.BlockSpec(blk, lambda b: (b, 0)),
        out_shape=jax.ShapeDtypeStruct((batch * seq, A_WIDTH), qkv.dtype),
        compiler_params=_params("arbitrary"),
        name="attn_sample",
    )(qkv, qkv, qkv, cache_k, cache_v, bias)


def _band_bias(table):
    heads = table.shape[0]
    n_keys = 3 * ATTN_K_TILE
    span = ATTN_Q_TILE + n_keys
    r = np.arange(ATTN_Q_TILE)[:, None]
    j = np.arange(n_keys)[None, :]
    band = (j // CHUNK >= r // CHUNK) & (j // CHUNK <= r // CHUNK + A_PAST_CHUNKS)
    t = table.astype(F32)
    u = jnp.concatenate([t, jnp.broadcast_to(t[:, -1:], (heads, span - t.shape[1]))], axis=1)
    w = u[:, ::-1]
    tiled = jnp.broadcast_to(w[:, None, :], (heads, ATTN_Q_TILE, span)).reshape(heads, ATTN_Q_TILE * span)
    view = tiled[:, :ATTN_Q_TILE * (span - 1)].reshape(heads, ATTN_Q_TILE, span - 1)
    bias = view[:, :, ATTN_Q_TILE - 1:ATTN_Q_TILE - 1 + n_keys]
    return jnp.where(band[None], bias, NEG_INF)


def _head_block_diag(x, head_masks):
    zero = jnp.zeros_like(x)
    return jnp.concatenate([jnp.where(m, x, zero) for m in head_masks], axis=0)


def _hgrn_kernel(hg_ref, lb_ref, ng_ref, s0_ref, o_ref, sfin_ref, st_ref, ot_ref, *, tile):
    i = pl.program_id(1)

    @pl.when(i == 0)
    def _():
        st_ref[...] = s0_ref[...]

    for bb in range(hg_ref.shape[0]):
        _hgrn_rows(hg_ref.at[bb], lb_ref, ng_ref, o_ref.at[bb], st_ref.at[bb], ot_ref.at[bb], tile)

    @pl.when(i == pl.num_programs(1) - 1)
    def _():
        sfin_ref[...] = st_ref[...]


def _hgrn_rows(hg_ref, lb_ref, ng_ref, o_ref, st_ref, ot_ref, tile):
    n_groups = B_WIDTH // HG_GROUP
    cd = o_ref.dtype
    q = hg_ref[:, 0:B_WIDTH]
    f_logit = hg_ref[:, B_WIDTH:2 * B_WIDTH]
    v_in = hg_ref[:, 2 * B_WIDTH:3 * B_WIDTH].astype(cd)
    lb = lb_ref[...]
    f = lb + (1.0 - lb) * (1.0 / (1.0 + jnp.exp(-f_logit)))
    log_f = jnp.log(f)
    kk = 1.0 - f

    r_t = lax.broadcasted_iota(jnp.int32, (tile, tile), 0)
    c_t = lax.broadcasted_iota(jnp.int32, (tile, tile), 1)
    tri = jnp.where((r_t // CHUNK == c_t // CHUNK) & (c_t <= r_t), 1.0, 0.0).astype(BF16)
    split = _split3 if cd == F32 else _split2
    g_all = functools.reduce(lambda x, y: x + y, [_dot(tri, part) for part in reversed(split(log_f))])

    row = lax.broadcasted_iota(jnp.int32, (CHUNK, B_WIDTH), 0)
    upper = row >= HG_HALF
    lane_g = lax.broadcasted_iota(jnp.int32, (CHUNK, HG_GROUP), 1)
    row_g = lax.broadcasted_iota(jnp.int32, (CHUNK, HG_GROUP), 0)
    head_masks = [lane_g // B_KEY_DIM == h for h in range(HG_GROUP // B_KEY_DIM)]
    causal = (lane_g % CHUNK) <= row_g
    r_bd = lax.broadcasted_iota(jnp.int32, (HG_GROUP, HG_GROUP), 0)
    c_bd = lax.broadcasted_iota(jnp.int32, (HG_GROUP, HG_GROUP), 1)
    diag_blocks = (r_bd // B_VAL_DIM) == (c_bd // B_KEY_DIM)

    pending = []
    for c in range(tile // CHUNK):
        rs = slice(CHUNK * c, CHUNK * (c + 1))
        g = g_all[rs]
        qc = q[rs]
        kc = kk[rs]
        vc = v_in[rs]
        g_q1 = g[HG_QUARTER - 1:HG_QUARTER]
        g_mid = g[HG_HALF - 1:HG_HALF]
        g_q3 = g[HG_HALF + HG_QUARTER - 1:HG_HALF + HG_QUARTER]
        g_last = g[CHUNK - 1:CHUNK]
        d_diag = g - jnp.where(upper, g_q3, g_q1)
        d_off = jnp.where(upper, g - g_mid, g_mid - g)
        q_in = (qc * jnp.exp(g)).astype(cd)
        k_st = (kc * jnp.exp(g_last - g)).astype(cd)
        q_diag = qc * jnp.exp(d_diag)
        k_diag = kc * jnp.exp(-d_diag)
        e_off = jnp.exp(d_off)
        zero = jnp.zeros_like(qc)
        q_lo = jnp.where(upper, zero, q_diag).astype(cd)
        q_hi = jnp.where(upper, q_diag, zero).astype(cd)
        q_x = jnp.where(upper, qc * e_off, zero).astype(cd)
        k_lo = jnp.where(upper, zero, k_diag).astype(cd)
        k_hi = jnp.where(upper, k_diag, zero).astype(cd)
        k_x = jnp.where(upper, zero, kc * e_off).astype(cd)
        decay = jnp.exp(g_last)
        for gi in range(n_groups):
            cs = slice(HG_GROUP * gi, HG_GROUP * (gi + 1))
            a_cat = jnp.concatenate([q_lo[:, cs], q_hi[:, cs], q_x[:, cs]], axis=1)
            k_cat = jnp.concatenate([_head_block_diag(k_lo[:, cs], head_masks),
                                     _head_block_diag(k_hi[:, cs], head_masks),
                                     _head_block_diag(k_x[:, cs], head_masks)], axis=1)
            att = _dot_nt(a_cat, k_cat)
            att = jnp.where(causal, att, 0.0).astype(cd)
            v_bd = _head_block_diag(vc[:, cs], head_masks)
            upd = jnp.where(diag_blocks, _dot_tn(vc[:, cs], k_st[:, cs]), 0.0)
            pending.append((rs, cs, gi, q_in[:, cs], _dot(att, v_bd), decay[:, cs], upd))

    states = [st_ref[gi] for gi in range(n_groups)]
    before = []
    for rs, cs, gi, q_in_g, o_intra, decay_g, upd in pending:
        before.append(states[gi].astype(cd))
        states[gi] = states[gi] * decay_g + upd
    for gi in range(n_groups):
        st_ref[gi] = states[gi]
    for (rs, cs, gi, q_in_g, o_intra, decay_g, upd), st_b in zip(pending, before):
        ot_ref[rs, cs] = _dot_nt(q_in_g, st_b) + o_intra

    o = ot_ref[...]
    lane_i = lax.broadcasted_iota(jnp.int32, (B_WIDTH, B_WIDTH), 0)
    lane_j = lax.broadcasted_iota(jnp.int32, (B_WIDTH, B_WIDTH), 1)
    head_ones = jnp.where(lane_i // B_VAL_DIM == lane_j // B_VAL_DIM, 1.0, 0.0).astype(BF16)
    sq_parts = split(o * o)
    sums = _dot(jnp.concatenate(sq_parts, axis=0), head_ones)
    ms = functools.reduce(lambda x, y: x + y, [sums[tile * j:tile * (j + 1)] for j in reversed(range(len(sq_parts)))])
    ms = ms * (1.0 / B_VAL_DIM)
    gate = hg_ref[:, 3 * B_WIDTH:4 * B_WIDTH]
    out = o * lax.rsqrt(ms + RMS_EPS) * ng_ref[...] * (gate * _sigmoid(gate))
    o_ref[...] = out.astype(cd)


def _hgrn(hg, lb, norm_g, s0_bd, batch, seq, tile, cd):
    nt = seq // tile
    n_groups = B_WIDTH // HG_GROUP
    par = HG_SEQS_PER_STEP
    assert batch % par == 0
    st_blk = (par, n_groups, HG_GROUP, HG_GROUP)
    ob, s_fin = pl.pallas_call(
        functools.partial(_hgrn_kernel, tile=tile),
        grid=(batch // par, nt),
        in_specs=[pl.BlockSpec((par, tile, 4 * B_WIDTH), lambda b, i: (b, i, 0)),
                  _const_spec((1, B_WIDTH)), _const_spec((1, B_WIDTH)),
                  pl.BlockSpec(st_blk, lambda b, i: (b, 0, 0, 0))],
        out_specs=[pl.BlockSpec((par, tile, B_WIDTH), lambda b, i: (b, i, 0)),
                   pl.BlockSpec(st_blk, lambda b, i: (b, 0, 0, 0))],
        out_shape=[jax.ShapeDtypeStruct((batch, seq, B_WIDTH), cd),
                   jax.ShapeDtypeStruct((batch,) + st_blk[1:], F32)],
        scratch_shapes=[pltpu.VMEM(st_blk, F32), pltpu.VMEM((par, tile, B_WIDTH), F32)],
        compiler_params=_params("arbitrary", "arbitrary"),
        name="hgrn",
    )(hg.reshape(batch, seq, 4 * B_WIDTH), lb, norm_g, s0_bd)
    return ob.reshape(batch * seq, B_WIDTH), s_fin


def _state_to_block_diag(s):
    b = s.shape[0]
    hpg = HG_GROUP // B_KEY_DIM
    st = s.astype(F32).reshape(b, B_HEADS // hpg, hpg, B_KEY_DIM, B_VAL_DIM).transpose(0, 1, 2, 4, 3)
    bd = jnp.einsum('bghvc,hk->bghvkc', st, jnp.eye(hpg, dtype=F32))
    return bd.reshape(b, B_HEADS // hpg, HG_GROUP, HG_GROUP)


def _block_diag_to_state(bd):
    b = bd.shape[0]
    hpg = HG_GROUP // B_KEY_DIM
    x = bd.reshape(b, B_HEADS // hpg, hpg, B_VAL_DIM, hpg, B_KEY_DIM)
    st = jnp.einsum('bghvkc,hk->bghvc', x, jnp.eye(hpg, dtype=F32))
    return st.transpose(0, 1, 2, 4, 3).reshape(b, B_HEADS, B_KEY_DIM, B_VAL_DIM)


def _rglru_kernel(rg_ref, conv0_ref, h0_ref, cw_ref, cb_ref, wa_ref, ba_ref, wx_ref, bx_ref, lam_ref,
                  o_ref, hlast_ref, xbuf_ref, hc_ref, *, tile, at_start):
    i = pl.program_id(1)
    pad = 8

    @pl.when(i == 0)
    def _():
        xbuf_ref[0:pad] = conv0_ref[0]
        hc_ref[...] = h0_ref[0]

    xr = rg_ref[:, 0:C_WIDTH]
    gate = rg_ref[:, C_WIDTH:2 * C_WIDTH]
    xbuf_ref[pad:pad + tile] = xr
    xc = cb_ref[...] + cw_ref[C_CONV - 1:C_CONV] * xr
    for j in range(1, C_CONV):
        xc = xc + cw_ref[C_CONV - 1 - j:C_CONV - j] * xbuf_ref[pad - j:pad - j + tile]
    xbuf_ref[0:pad] = xbuf_ref[tile:tile + pad]

    xcb = xc.astype(wa_ref.dtype)
    r = _sigmoid(_dot(xcb, wa_ref[...]) + ba_ref[...])
    ig = _sigmoid(_dot(xcb, wx_ref[...]) + bx_ref[...])
    neg_lam = -lam_ref[...]
    softplus = jnp.maximum(neg_lam, 0.0) + jnp.log(1.0 + jnp.exp(-jnp.abs(neg_lam)))
    a = jnp.exp(r * (-C_GATE_C * softplus))
    mult = jnp.sqrt(1.0 - a * a)
    row = lax.broadcasted_iota(jnp.int32, (tile, C_WIDTH), 0)
    if at_start:
        mult = jnp.where((row == 0) & (i == 0), 1.0, mult)
    b = mult * ig * xc

    d = 1
    while d < tile:
        a_sh = pltpu.roll(a, d, 0)
        b_sh = pltpu.roll(b, d, 0)
        keep = row >= d
        b = jnp.where(keep, a * b_sh + b, b)
        a = jnp.where(keep, a * a_sh, a)
        d *= 2
    h = a * hc_ref[...] + b
    hc_ref[...] = h[tile - 1:tile]
    hlast_ref[0] = h[tile - 1:tile]
    gelu = 0.5 * gate * (1.0 + jnp.tanh(np.sqrt(2.0 / np.pi).astype(np.float32) * (gate + 0.044715 * gate * gate * gate)))
    o_ref[...] = (h * gelu).astype(o_ref.dtype)


def _rglru(rg, conv0_pad, h0, w, batch, seq, tile, at_start):
    nt = seq // tile
    vec = _const_spec((1, C_WIDTH))
    return pl.pallas_call(
        functools.partial(_rglru_kernel, tile=tile, at_start=at_start),
        grid=(batch, nt),
        in_specs=[pl.BlockSpec((tile, 2 * C_WIDTH), lambda b, i: (b * nt + i, 0)),
                  pl.BlockSpec((1, 8, C_WIDTH), lambda b, i: (b, 0, 0)),
                  pl.BlockSpec((1, 1, C_WIDTH), lambda b, i: (b, 0, 0)),
                  _const_spec((C_CONV, C_WIDTH)), vec,
                  _const_spec((C_WIDTH, C_WIDTH)), vec, _const_spec((C_WIDTH, C_WIDTH)), vec, vec],
        out_specs=[pl.BlockSpec((tile, C_WIDTH), lambda b, i: (b * nt + i, 0)),
                   pl.BlockSpec((1, 1, C_WIDTH), lambda b, i: (b, 0, 0))],
        out_shape=[jax.ShapeDtypeStruct((batch * seq, C_WIDTH), w['wa_bd'].dtype),
                   jax.ShapeDtypeStruct((batch, 1, C_WIDTH), F32)],
        scratch_shapes=[pltpu.VMEM((tile + 8, C_WIDTH), F32), pltpu.VMEM((1, C_WIDTH), F32)],
        compiler_params=_params("arbitrary", "arbitrary"),
        name="rglru",
    )(rg, conv0_pad, h0, w['conv_w'], w['conv_b'], w['wa_bd'], w['ba'], w['wx_bd'], w['bx'], w['lam'])


def _block_diag_weight(w):
    n, d, e = w.shape
    return jnp.einsum('nde,nm->ndme', w, jnp.eye(n, dtype=w.dtype)).reshape(n * d, n * e)


def _route(logits_t):
    m = jnp.max(logits_t, axis=0, keepdims=True)
    e = jnp.exp(logits_t - m)
    p = e / jnp.sum(e, axis=0, keepdims=True)
    rows = [p[j:j + 1] for j in range(N_EXPERTS)]
    scores = []
    for g in range(N_GROUPS):
        mem = rows[GROUP_SIZE * g:GROUP_SIZE * (g + 1)]
        best = None
        for a in range(GROUP_SIZE):
            for b in range(a + 1, GROUP_SIZE):
                pair = mem[a] + mem[b]
                best = pair if best is None else jnp.maximum(best, pair)
        scores.append(best)
    smax = functools.reduce(jnp.maximum, scores)
    taken = jnp.zeros_like(smax)
    sel = []
    for g in range(N_GROUPS):
        hit = jnp.where(scores[g] == smax, 1.0, 0.0) * (1.0 - taken)
        taken = taken + hit
        sel.append(hit)
    picked = []
    for j in range(N_EXPERTS):
        g = j // GROUP_SIZE
        rank = jnp.zeros_like(smax)
        for o in range(GROUP_SIZE * g, GROUP_SIZE * (g + 1)):
            if o == j:
                continue
            ahead = (rows[o] >= rows[j]) if o < j else (rows[o] > rows[j])
            rank = rank + jnp.where(ahead, 1.0, 0.0)
        picked.append(sel[g] * jnp.where(rank < float(2), 1.0, 0.0))
    denom = functools.reduce(lambda x, y: x + y, [picked[j] * rows[j] for j in range(N_EXPERTS)])
    comb = [picked[j] * rows[j] / denom for j in range(N_EXPERTS)]
    out = [functools.reduce(lambda x, y: x + y, [sel[g] * comb[GROUP_SIZE * g + m] for g in range(N_GROUPS)])
           for m in range(GROUP_SIZE)]
    out.append(functools.reduce(lambda x, y: x + y, [float(g) * sel[g] for g in range(1, N_GROUPS)]))
    out.append(jnp.zeros((ROUTE_ROWS - GROUP_SIZE - 1,) + smax.shape[1:], F32))
    return jnp.concatenate(out, axis=0)


def _mix_kernel(x_ref, oa_ref, ob_ref, oc_ref, gl_ref, mk_ref, mv_ref, wb_ref, wout_ref, wq_ref, wo_ref,
                g1_ref, b1_ref, g2_ref, b2_ref, rt_ref, x2_ref, route_ref, *, sub):
    tiles = [slice(s0, s0 + sub) for s0 in range(0, x_ref.shape[0], sub)]
    cd = wb_ref.dtype

    mixed = []
    for rows in tiles:
        acc = None
        for b, o_ref in enumerate((oa_ref, ob_ref, oc_ref)):
            per_branch = _dot(o_ref[rows], wb_ref[b])
            gate = _sigmoid(gl_ref[rows, D_MODEL * b:D_MODEL * (b + 1)].astype(F32))
            acc = gate * per_branch if acc is None else acc + gate * per_branch
        mixed.append(acc.astype(cd))

    x1 = [_layer_norm(DN_ALPHA * x_ref[rows] + _dot(m, wout_ref[...]), g1_ref[...], b1_ref[...])
          for rows, m in zip(tiles, mixed)]
    q = [(_dot(t.astype(cd), wq_ref[...]) * (X_HEAD_DIM ** -0.5)).astype(cd) for t in x1]

    heads = [[] for _ in tiles]
    for h in range(X_HEADS):
        sl = slice(X_HEAD_DIM * h, X_HEAD_DIM * (h + 1))
        scores = [_dot_nt(qt[:, sl], mk_ref[0, :, sl]) for qt in q]
        for t, s in enumerate(scores):
            m = jnp.max(s, axis=-1, keepdims=True)
            e = jnp.exp(s - m)
            l = jnp.sum(e, axis=-1, keepdims=True)
            heads[t].append((_dot(e.astype(cd), mv_ref[0, :, sl]) * (1.0 / l)).astype(cd))

    attn = [_dot(jnp.concatenate(hs, axis=-1), wo_ref[...]) for hs in heads]
    x2 = [_layer_norm(DN_ALPHA * a + b, g2_ref[...], b2_ref[...]) for a, b in zip(x1, attn)]

    r_hi, r_lo = _split2(rt_ref[...])
    r_both = jnp.concatenate([r_hi, r_lo], axis=0)
    eye_r = lax.broadcasted_iota(jnp.int32, (ROUTE_ROWS, ROUTE_LANES), 0)
    eye_c = lax.broadcasted_iota(jnp.int32, (ROUTE_ROWS, ROUTE_LANES), 1)
    eye = jnp.where(eye_r == eye_c, 1.0, 0.0).astype(BF16)
    logits = []
    for rows, t in zip(tiles, x2):
        x2_ref[rows] = t
        x_hi, x_lo = _split2(t)
        by_hi = _dot_nt(r_both, x_hi)
        logits.append(by_hi[N_EXPERTS:] + _dot_nt(r_hi, x_lo) + by_hi[:N_EXPERTS])
    routes = [_split3(_route(lt)) for lt in logits]
    for rows, (hi, mid, lo) in zip(tiles, routes):
        route_ref[rows] = _dot_tn(hi, eye) + _dot_tn(mid, eye) + _dot_tn(lo, eye)


def _mix(x, oa, ob, oc, gl, mk, mv, w, batch, seq):
    sub = min(MIX_TILE, seq)
    tm = min(2 * MIX_TILE, seq)
    nt = seq // tm
    row = lambda b, i: (b * nt + i, 0)
    vec = _const_spec((1, D_MODEL))
    n_mem = mk.shape[1]
    mem = pl.BlockSpec((1, n_mem, D_MODEL), lambda b, i: (b, 0, 0))
    sq = _layer_spec((D_MODEL, D_MODEL), w['layer'])
    return pl.pallas_call(
        functools.partial(_mix_kernel, sub=sub),
        grid=(batch, nt),
        in_specs=[pl.BlockSpec((tm, D_MODEL), row)] + [pl.BlockSpec((tm, A_WIDTH), row)] * 3
        + [pl.BlockSpec((tm, N_BRANCH * D_MODEL), row), mem, mem,
           _layer_spec((N_BRANCH, A_WIDTH, D_MODEL), w['layer']), sq, sq, sq, vec, vec, vec, vec,
           _const_spec((N_EXPERTS, D_MODEL))],
        out_specs=[pl.BlockSpec((tm, D_MODEL), row), pl.BlockSpec((tm, ROUTE_LANES), row)],
        out_shape=[jax.ShapeDtypeStruct((batch * seq, D_MODEL), F32),
                   jax.ShapeDtypeStruct((batch * seq, ROUTE_LANES), F32)],
        compiler_params=_params("arbitrary", "arbitrary"),
        name="mix",
    )(x, oa, ob, oc, gl, mk, mv, w['w_branch'], w['w_out'], w['xa_wq'], w['xa_wo'],
      w['ln1_g'], w['ln1_b'], w['ln2_g'], w['ln2_b'], w['router_t'])


def _sc_gather_rows(tables, idx):
    n_out = idx.shape[0]
    per_worker = n_out // SC_WORKERS
    window = min(SC_GATHER_WINDOW, per_worker)
    steps = per_worker // window
    assert per_worker * SC_WORKERS == n_out and steps * window == per_worker and window % 8 == 0
    idx3 = idx.astype(jnp.int32).reshape(SC_WORKERS, steps, window)
    mesh = plsc.VectorSubcoreMesh(core_axis_name="core", subcore_axis_name="subcore")
    n_tab = len(tables)

    def body(*refs):
        tab_hbm = refs[:n_tab]
        idx_hbm = refs[n_tab]
        out_hbm = refs[n_tab + 1:2 * n_tab + 1]
        idx_v = refs[2 * n_tab + 1]
        rows_v = refs[2 * n_tab + 2:3 * n_tab + 2]
        sem = refs[3 * n_tab + 2]
        wid = lax.axis_index("subcore") * SC_CORES + lax.axis_index("core")
        pltpu.sync_copy(idx_hbm.at[wid], idx_v)

        @pl.loop(0, steps)
        def _(j):
            base = wid * per_worker + j * window
            for k in range(n_tab):
                pltpu.async_copy(tab_hbm[k].at[idx_v.at[j]], rows_v[k], sem).wait()
                pltpu.sync_copy(rows_v[k], out_hbm[k].at[pl.ds(base, window)])

    call = pl.kernel(
        body,
        out_type=[jax.ShapeDtypeStruct((n_out, t.shape[1]), t.dtype) for t in tables],
        mesh=mesh,
        scratch_types=[pltpu.VMEM((steps, window), jnp.int32)]
        + [pltpu.VMEM((window, t.shape[1]), t.dtype) for t in tables] + [pltpu.SemaphoreType.DMA],
        name="sc_gather",
    )
    return call(*tables, idx3)


def _dispatch_plan(gid, tm):
    n = gid.shape[0]
    n_pad = n + N_GROUPS * tm
    groups = jnp.arange(N_GROUPS, dtype=jnp.int32)
    onehot = (gid[:, None] == groups[None, :]).astype(jnp.int32)
    counts = jnp.sum(onehot, axis=0)
    csum = jnp.cumsum(onehot, axis=0)
    rank = jnp.sum((csum - onehot) * onehot, axis=1)
    padded = ((counts + tm - 1) // tm) * tm
    start_p = jnp.cumsum(padded) - padded
    start_u = jnp.cumsum(counts) - counts
    pos = jnp.sum(onehot * start_p[None, :], axis=1) + rank
    order = jnp.argsort(gid, stable=True).astype(jnp.int32)
    slot = jnp.arange(n_pad, dtype=jnp.int32)
    g_slot = jnp.minimum(jnp.sum((slot[:, None] >= (start_p + padded)[None, :]).astype(jnp.int32), axis=1),
                         N_GROUPS - 1)
    oh_slot = (g_slot[:, None] == groups[None, :]).astype(jnp.int32)
    r_slot = slot - jnp.sum(oh_slot * start_p[None, :], axis=1)
    valid = r_slot < jnp.sum(oh_slot * counts[None, :], axis=1)
    s = jnp.where(valid, jnp.sum(oh_slot * start_u[None, :], axis=1) + r_slot, 0)
    src = jnp.take(order, s)
    return pos.astype(jnp.int32), src, g_slot[::tm]


def _moe_kernel(tg_ref, x_ref, r_ref, w1_ref, w3_ref, w2_ref, g_ref, b_ref, o_ref):
    del tg_ref
    x = x_ref[...]
    cd = w1_ref.dtype
    xb = x.astype(cd)
    r = r_ref[...]
    y = None
    for m in range(GROUP_SIZE):
        h1 = _dot(xb, w1_ref[m])
        h3 = _dot(xb, w3_ref[m])
        h = (h1 * _sigmoid(h1) * h3).astype(cd)
        ym = r[:, m:m + 1] * _dot(h, w2_ref[m])
        y = ym if y is None else y + ym
    o_ref[...] = _layer_norm(DN_ALPHA * x + y, g_ref[...], b_ref[...])


def _moe_sorted(xs, rs, tile_gid, w, tm):
    n_pad = xs.shape[0]
    row = lambda i, tg: (i, 0)
    layer = w['layer']
    grp = lambda i, tg: (layer, tg[i], 0, 0)
    vec = pl.BlockSpec((1, D_MODEL), lambda i, tg: (0, 0))
    mode = dict(pipeline_mode=pl.Buffered(1)) if w['w1'].dtype == F32 else {}
    return pl.pallas_call(
        _moe_kernel,
        grid_spec=pltpu.PrefetchScalarGridSpec(
            num_scalar_prefetch=1,
            grid=(n_pad // tm,),
            in_specs=[pl.BlockSpec((tm, D_MODEL), row), pl.BlockSpec((tm, ROUTE_LANES), row),
                      pl.BlockSpec((None, GROUP_SIZE, D_MODEL, D_EXPERT), grp, **mode),
                      pl.BlockSpec((None, GROUP_SIZE, D_MODEL, D_EXPERT), grp, **mode),
                      pl.BlockSpec((None, GROUP_SIZE, D_EXPERT, D_MODEL), grp, **mode), vec, vec],
            out_specs=pl.BlockSpec((tm, D_MODEL), row),
        ),
        out_shape=jax.ShapeDtypeStruct((n_pad, D_MODEL), F32),
        compiler_params=_params("arbitrary"),
        name="moe",
    )(tile_gid, xs, rs, w['w1'], w['w3'], w['w2'], w['ln3_g'], w['ln3_b'])


def _moe(x2, route, w):
    n = x2.shape[0]
    tm = min(MOE_TILE, n // N_GROUPS)
    gid = route[:, GROUP_SIZE].astype(jnp.int32)
    pos, src, tile_gid = _dispatch_plan(gid, tm)
    xs, rs = _sc_gather_rows([x2, route], src)
    ys = _moe_sorted(xs, rs, tile_gid, w, tm)
    return _sc_gather_rows([ys], pos)[0]


def _trunk_layer(x, w, mem_k, mem_v, cache_k, cache_v, s0, h0, conv0, batch, seq, prompt):
    qkv, hg, rg, gl = _inproj(x, w['w_in'], w['layer'])
    keep = min(A_WINDOW, seq)
    x_keep = x.reshape(batch, seq, D_MODEL)[:, seq - keep:].reshape(batch * keep, D_MODEL)
    k32, v32 = [t.reshape(batch, keep, A_HEADS, A_HEAD_DIM) for t in _kv_rows(x_keep, w['w_in'], w['layer'])]
    if prompt:
        oa = _attn_prompt(qkv, w['band_bias'], batch, seq)
        new_k, new_v = k32, v32
        tile = 256
    else:
        win = cache_k.shape[1]
        ck = cache_k.reshape(batch, win, A_WIDTH)
        cv = cache_v.reshape(batch, win, A_WIDTH)
        oa = _attn_sample(qkv, ck, cv, w['band_bias'][:, :seq, :win + 2 * CHUNK], batch, seq)
        new_k = jnp.concatenate([cache_k, k32], axis=1)[:, seq:]
        new_v = jnp.concatenate([cache_v, v32], axis=1)[:, seq:]
        tile = seq
    ob, s_bd = _hgrn(hg, w['lb'], w['hgrn_g'], _state_to_block_diag(s0), batch, seq, tile, w['w_in'].dtype)
    conv0_pad = jnp.concatenate([jnp.zeros((batch, 8 - (C_CONV - 1), C_WIDTH), F32), conv0.astype(F32)], axis=1)
    oc, h_new = _rglru(rg, conv0_pad, h0.astype(F32).reshape(batch, 1, C_WIDTH), w, batch, seq, tile, prompt)
    conv_new = jnp.concatenate([conv0.astype(F32), rg.reshape(batch, seq, 2 * C_WIDTH)[:, :, :C_WIDTH]],
                               axis=1)[:, seq:]
    x2, route = _mix(x, oa, ob, oc, gl, mem_k, mem_v, w, batch, seq)
    x3 = _moe(x2, route, w)
    return x3, (new_k, new_v, _block_diag_to_state(s_bd), h_new.reshape(batch, C_WIDTH), conv_new)


def kernel(x_prompt, x_sample, cache_attn_k, cache_attn_v, state_hgrn, state_rglru, state_conv, cache_mem_k, cache_mem_v, mem_prompt, w_in, attn_rel_bias, hgrn_lb_logits, hgrn_norm_g, rg_conv_w, rg_conv_b, rg_wa, rg_ba, rg_wx, rg_bx, rg_lambda, w_branch, w_out, ln1_g, ln1_b, xa_wq, xa_wk, xa_wv, xa_wo, ln2_g, ln2_b, moe_router, moe_w1, moe_w3, moe_w2, ln3_g, ln3_b):
    bp, tp, _ = x_prompt.shape
    bs, ts, _ = x_sample.shape
    n_mem = mem_prompt.shape[1]
    depth = w_in.shape[0]

    p = jax.nn.softmax(hgrn_lb_logits.astype(F32), axis=0)
    lb_all = jnp.cumsum(p, axis=0) - p[0:1]
    vec = lambda t: t.astype(F32).reshape(1, -1)

    xp = x_prompt.reshape(bp * tp, D_MODEL)
    xs = x_sample.reshape(bs * ts, D_MODEL)
    mem2d = mem_prompt.reshape(bp * n_mem, D_MODEL)
    s0p = jnp.zeros((bp, B_HEADS, B_KEY_DIM, B_VAL_DIM), F32)
    h0p = jnp.zeros((bp, C_WIDTH), F32)
    conv0p = jnp.zeros((bp, C_CONV - 1, C_WIDTH), F32)
    router_t = moe_router.astype(F32).T

    stacked = {'w_in': w_in, 'w_branch': w_branch, 'w_out': w_out, 'xa_wq': xa_wq, 'xa_wo': xa_wo,
               'w1': moe_w1, 'w3': moe_w3, 'w2': moe_w2}
    stacked_fast = {k: v.astype(BF16) for k, v in stacked.items()}
    stacked_precise = {k: v.astype(F32) for k, v in stacked.items()}
    wk_b, wv_b = xa_wk.astype(BF16), xa_wv.astype(BF16)

    outs_p, outs_s, mem_ks, mem_vs = [], [], [], []
    for l in range(depth):
        wf = {'wa_bd': _block_diag_weight(rg_wa[l]), 'wx_bd': _block_diag_weight(rg_wx[l])}
        shared = {
            'layer': l, 'band_bias': _band_bias(attn_rel_bias[l]),
            'lb': vec(lb_all[l]), 'hgrn_g': vec(jnp.tile(hgrn_norm_g[l], B_HEADS)),
            'conv_w': rg_conv_w[l].astype(F32), 'conv_b': vec(rg_conv_b[l]),
            'ba': vec(rg_ba[l]), 'bx': vec(rg_bx[l]), 'lam': vec(rg_lambda[l]),
            'ln1_g': vec(ln1_g[l]), 'ln1_b': vec(ln1_b[l]), 'ln2_g': vec(ln2_g[l]), 'ln2_b': vec(ln2_b[l]),
            'ln3_g': vec(ln3_g[l]), 'ln3_b': vec(ln3_b[l]), 'router_t': router_t,
        }
        w_fast = dict(shared, **stacked_fast, **{k: v.astype(BF16) for k, v in wf.items()})
        w_precise = dict(shared, **stacked_precise, **{k: v.astype(F32) for k, v in wf.items()})
        mk_p = _matmul(mem2d, wk_b, l)
        mv_p = _matmul(mem2d, wv_b, l)
        mem_ks.append(mk_p.reshape(bp, n_mem, X_HEADS, X_HEAD_DIM))
        mem_vs.append(mv_p.reshape(bp, n_mem, X_HEADS, X_HEAD_DIM))
        xp, st_p = _trunk_layer(xp, w_fast, mk_p.reshape(bp, n_mem, D_MODEL).astype(BF16),
                                mv_p.reshape(bp, n_mem, D_MODEL).astype(BF16),
                                None, None, s0p, h0p, conv0p, bp, tp, True)
        xs, st_s = _trunk_layer(xs, w_precise, cache_mem_k[l].reshape(bs, n_mem, D_MODEL).astype(F32),
                                cache_mem_v[l].reshape(bs, n_mem, D_MODEL).astype(F32),
                                cache_attn_k[l], cache_attn_v[l], state_hgrn[l], state_rglru[l], state_conv[l],
                                bs, ts, False)
        outs_p.append(st_p)
        outs_s.append(st_s)

    stack = lambda items, j: jnp.stack([it[j] for it in items])
    return (xp.reshape(bp, tp, D_MODEL), xs.reshape(bs, ts, D_MODEL),
            stack(outs_p, 0), stack(outs_p, 1), stack(outs_p, 2), stack(outs_p, 3), stack(outs_p, 4),
            jnp.stack(mem_ks), jnp.stack(mem_vs),
            stack(outs_s, 0), stack(outs_s, 1), stack(outs_s, 2), stack(outs_s, 3), stack(outs_s, 4))
```

```python
import functools

import numpy as np
import jax
import jax.numpy as jnp
from jax import lax
from jax.experimental import pallas as pl
from jax.experimental.pallas import tpu as pltpu
from jax.experimental.pallas import tpu_sc as plsc

F32 = jnp.float32
BF16 = jnp.bfloat16

D_MODEL = 1024
CHUNK = 64
A_HEADS = 8
A_HEAD_DIM = 64
A_WIDTH = A_HEADS * A_HEAD_DIM
A_PAST_CHUNKS = 8
A_WINDOW = A_PAST_CHUNKS * CHUNK
A_MAX_REL = 256
B_HEADS = 8
B_KEY_DIM = 64
B_VAL_DIM = 64
B_WIDTH = B_HEADS * B_KEY_DIM
C_WIDTH = 512
C_CONV = 4
C_GATE_C = 8.0
N_BRANCH = 3
IN_COLS = 3 * A_WIDTH + 4 * B_WIDTH + 2 * C_WIDTH + N_BRANCH * D_MODEL
X_HEADS = 4
X_HEAD_DIM = D_MODEL // X_HEADS
N_EXPERTS = 16
N_GROUPS = 4
GROUP_SIZE = N_EXPERTS // N_GROUPS
D_EXPERT = D_MODEL // 2
DEPTH = 4
DN_ALPHA = (2 * DEPTH) ** 0.25
LN_EPS = 1e-5
RMS_EPS = 1e-6
NEG_INF = -1e30

VMEM_LIMIT_BYTES = 56 * 1024 * 1024

INPROJ_TILE = 512
MIX_TILE = 256
ATTN_Q_TILE = 256
ATTN_K_TILE = 256
HG_GROUP = 256
HG_SEQS_PER_STEP = 2
HG_HALF = CHUNK // 2
HG_QUARTER = CHUNK // 4
EXPERT_PAIRS = tuple((lo, hi) for lo in range(GROUP_SIZE) for hi in range(lo + 1, GROUP_SIZE))
N_SEGMENTS = N_GROUPS * len(EXPERT_PAIRS)
ROUTE_ROWS = 16
ROUTE_LANES = 128
MOE_TILE = 512
SC_CORES = 2
SC_SUBCORES = 16
SC_WORKERS = SC_CORES * SC_SUBCORES
SC_GATHER_WINDOW = 32


def _split2(x):
    hi = x.astype(BF16)
    lo = (x - hi.astype(F32)).astype(BF16)
    return hi, lo


def _contract(a, b, dims):
    dg = lambda u, v: lax.dot_general(u, v, (dims, ((), ())), preferred_element_type=F32)
    if a.dtype == F32:
        a_hi, a_lo = _split2(a)
        b_hi, b_lo = _split2(b)
        return dg(a_lo, b_hi) + dg(a_hi, b_lo) + dg(a_hi, b_hi)
    return dg(a, b)


def _dot(a, b):
    return _contract(a, b, ((1,), (0,)))


def _dot_nt(a, b):
    return _contract(a, b, ((1,), (1,)))


def _dot_tn(a, b):
    return _contract(a, b, ((0,), (0,)))


def _split3(x):
    hi = x.astype(BF16)
    r1 = x - hi.astype(F32)
    mid = r1.astype(BF16)
    lo = (r1 - mid.astype(F32)).astype(BF16)
    return hi, mid, lo


def _sigmoid(x):
    return 0.5 * jnp.tanh(0.5 * x) + 0.5


def _layer_norm(x, g, b):
    mu = jnp.mean(x, axis=-1, keepdims=True)
    xc = x - mu
    var = jnp.mean(xc * xc, axis=-1, keepdims=True)
    return xc * lax.rsqrt(var + LN_EPS) * g + b


def _params(*semantics):
    return pltpu.CompilerParams(dimension_semantics=semantics, vmem_limit_bytes=VMEM_LIMIT_BYTES)


def _const_spec(shape):
    nd = len(shape)
    return pl.BlockSpec(shape, lambda *_: (0,) * nd, pipeline_mode=pl.Buffered(1))


def _layer_spec(shape, layer):
    nd = len(shape)
    return pl.BlockSpec((None,) + tuple(shape), lambda *_: (layer,) + (0,) * nd, pipeline_mode=pl.Buffered(1))


def _inproj_kernel(x_ref, w_ref, qkv_ref, hg_ref, rg_ref, gl_ref):
    xb = x_ref[...].astype(w_ref.dtype)
    cw = 512

    def mm(c0):
        return _dot(xb, w_ref[:, c0:c0 + cw])

    for j in range(3):
        qkv_ref[:, cw * j:cw * (j + 1)] = mm(cw * j).astype(qkv_ref.dtype)
    base = 3 * A_WIDTH
    for j in range(4):
        hg_ref[:, cw * j:cw * (j + 1)] = mm(base + cw * j)
    base += 4 * B_WIDTH
    for j in range(2):
        rg_ref[:, cw * j:cw * (j + 1)] = mm(base + cw * j)
    base += 2 * C_WIDTH
    for j in range(N_BRANCH * D_MODEL // cw):
        gl_ref[:, cw * j:cw * (j + 1)] = mm(base + cw * j).astype(gl_ref.dtype)


def _inproj(x, w_in, layer):
    n = x.shape[0]
    cd = w_in.dtype
    tm = INPROJ_TILE if cd == BF16 else INPROJ_TILE // 2
    assert n % tm == 0
    row = lambda i: (i, 0)
    return pl.pallas_call(
        _inproj_kernel,
        grid=(n // tm,),
        in_specs=[pl.BlockSpec((tm, D_MODEL), row), _layer_spec((D_MODEL, IN_COLS), layer)],
        out_specs=[pl.BlockSpec((tm, 3 * A_WIDTH), row),
                   pl.BlockSpec((tm, 4 * B_WIDTH), row), pl.BlockSpec((tm, 2 * C_WIDTH), row),
                   pl.BlockSpec((tm, N_BRANCH * D_MODEL), row)],
        out_shape=[jax.ShapeDtypeStruct((n, 3 * A_WIDTH), cd),
                   jax.ShapeDtypeStruct((n, 4 * B_WIDTH), F32), jax.ShapeDtypeStruct((n, 2 * C_WIDTH), F32),
                   jax.ShapeDtypeStruct((n, N_BRANCH * D_MODEL), cd)],
        compiler_params=_params("arbitrary"),
        name="inproj",
    )(x, w_in)


def _matmul_kernel(x_ref, w_ref, o_ref):
    o_ref[...] = _dot(x_ref[...].astype(w_ref.dtype), w_ref[...])


def _matmul(x, w, layer):
    n, k = x.shape
    m = w.shape[2]
    tm = 256
    return pl.pallas_call(
        _matmul_kernel,
        grid=(n // tm,),
        in_specs=[pl.BlockSpec((tm, k), lambda i: (i, 0)), _layer_spec((k, m), layer)],
        out_specs=pl.BlockSpec((tm, m), lambda i: (i, 0)),
        out_shape=jax.ShapeDtypeStruct((n, m), F32),
        compiler_params=_params("arbitrary"),
        name="matmul",
    )(x, w)


def _kv_kernel(x_ref, wk_ref, wv_ref, k_ref, v_ref):
    xb = x_ref[...].astype(wk_ref.dtype)
    k_ref[...] = _dot(xb, wk_ref[...])
    v_ref[...] = _dot(xb, wv_ref[...])


def _kv_rows(x, w_in, layer):
    n = x.shape[0]
    tm = 256
    col = lambda j: pl.BlockSpec((None, D_MODEL, A_WIDTH), lambda i: (layer, 0, j), pipeline_mode=pl.Buffered(1))
    out = pl.BlockSpec((tm, A_WIDTH), lambda i: (i, 0))
    return pl.pallas_call(
        _kv_kernel,
        grid=(n // tm,),
        in_specs=[pl.BlockSpec((tm, D_MODEL), lambda i: (i, 0)), col(1), col(2)],
        out_specs=[out, out],
        out_shape=[jax.ShapeDtypeStruct((n, A_WIDTH), F32)] * 2,
        compiler_params=_params("arbitrary"),
        name="kv_rows",
    )(x, w_in, w_in)


def _attn_core(q, k, v, bias_ref, valid):
    rows = q.shape[0]
    lane = lax.broadcasted_iota(jnp.int32, (rows, 2 * A_HEAD_DIM), 1)
    first = lane < A_HEAD_DIM
    q = q * jnp.asarray(A_HEAD_DIM ** -0.5, q.dtype)

    def scores(head):
        sl = slice(2 * A_HEAD_DIM * (head // 2), 2 * A_HEAD_DIM * (head // 2 + 1))
        sel = first if head % 2 == 0 else jnp.logical_not(first)
        qm = jnp.where(sel, q[:, sl], jnp.zeros_like(q[:, sl]))
        s = _dot_nt(qm, k[:, sl]) + bias_ref[head]
        return s if valid is None else jnp.where(valid, s, NEG_INF)

    outs = []
    pair = None
    s_next = scores(0)
    for head in range(A_HEADS):
        s = s_next
        if head + 1 < A_HEADS:
            s_next = scores(head + 1)
        m = jnp.max(s, axis=-1, keepdims=True)
        e = jnp.exp(s - m)
        l = jnp.sum(e, axis=-1, keepdims=True)
        sl = slice(2 * A_HEAD_DIM * (head // 2), 2 * A_HEAD_DIM * (head // 2 + 1))
        o = _dot(e.astype(v.dtype), v[:, sl]) * (1.0 / l)
        if head % 2 == 0:
            pair = o
        else:
            outs.append(jnp.where(first, pair, o))
    return jnp.concatenate(outs, axis=-1)


def _attn_prompt_kernel(q_ref, k0_ref, k1_ref, k2_ref, v0_ref, v1_ref, v2_ref, bias_ref, o_ref):
    i = pl.program_id(1)

    def run(masked):
        k = jnp.concatenate([k0_ref[...], k1_ref[...], k2_ref[...]], axis=0)
        v = jnp.concatenate([v0_ref[...], v1_ref[...], v2_ref[...]], axis=0)
        valid = None
        if masked:
            col = lax.broadcasted_iota(jnp.int32, (ATTN_Q_TILE, 3 * ATTN_K_TILE), 1)
            valid = col >= (2 - i) * ATTN_K_TILE
        o_ref[...] = _attn_core(q_ref[...], k, v, bias_ref, valid).astype(o_ref.dtype)

    pl.when(i < 2)(lambda: run(True))
    pl.when(i >= 2)(lambda: run(False))


def _attn_prompt(qkv, bias, batch, seq):
    nt = seq // ATTN_Q_TILE
    blk = (ATTN_Q_TILE, A_WIDTH)

    def kv_spec(j, col):
        return pl.BlockSpec(blk, lambda b, i: (b * nt + jnp.maximum(i - 2 + j, 0), col))

    return pl.pallas_call(
        _attn_prompt_kernel,
        grid=(batch, nt),
        in_specs=[pl.BlockSpec(blk, lambda b, i: (b * nt + i, 0))]
        + [kv_spec(j, 1) for j in range(3)] + [kv_spec(j, 2) for j in range(3)]
        + [_const_spec(bias.shape)],
        out_specs=pl.BlockSpec(blk, lambda b, i: (b * nt + i, 0)),
        out_shape=jax.ShapeDtypeStruct((batch * seq, A_WIDTH), qkv.dtype),
        compiler_params=_params("arbitrary", "arbitrary"),
        name="attn_prompt",
    )(qkv, qkv, qkv, qkv, qkv, qkv, qkv, bias)


def _attn_sample_kernel(q_ref, kn_ref, vn_ref, ck_ref, cv_ref, bias_ref, o_ref):
    cd = q_ref.dtype
    pad = jnp.zeros((CHUNK, A_WIDTH), cd)
    k = jnp.concatenate([ck_ref[0].astype(cd), kn_ref[...], pad], axis=0)
    v = jnp.concatenate([cv_ref[0].astype(cd), vn_ref[...], pad], axis=0)
    o_ref[...] = _attn_core(q_ref[...], k, v, bias_ref, None).astype(cd)


def _attn_sample(qkv, cache_k, cache_v, bias, batch, seq):
    win = cache_k.shape[1]
    blk = (seq, A_WIDTH)
    return pl.pallas_call(
        _attn_sample_kernel,
        grid=(batch,),
        in_specs=[pl.BlockSpec(blk, lambda b: (b, 0)), pl.BlockSpec(blk, lambda b: (b, 1)),
                  pl.BlockSpec(blk, lambda b: (b, 2)),
                  pl.BlockSpec((1, win, A_WIDTH), lambda b: (b, 0, 0)),
                  pl.BlockSpec((1, win, A_WIDTH), lambda b: (b, 0, 0)),
                  _const_spec(bias.shape)],
        out_specs=pl.BlockSpec(blk, lambda b: (b, 0)),
        out_shape=jax.ShapeDtypeStruct((batch * seq, A_WIDTH), qkv.dtype),
        compiler_params=_params("arbitrary"),
        name="attn_sample",
    )(qkv, qkv, qkv, cache_k, cache_v, bias)


def _band_bias(table):
    heads = table.shape[0]
    n_keys = 3 * ATTN_K_TILE
    span = ATTN_Q_TILE + n_keys
    r = np.arange(ATTN_Q_TILE)[:, None]
    j = np.arange(n_keys)[None, :]
    band = (j // CHUNK >= r // CHUNK) & (j // CHUNK <= r // CHUNK + A_PAST_CHUNKS)
    t = table.astype(F32)
    u = jnp.concatenate([t, jnp.broadcast_to(t[:, -1:], (heads, span - t.shape[1]))], axis=1)
    w = u[:, ::-1]
    tiled = jnp.broadcast_to(w[:, None, :], (heads, ATTN_Q_TILE, span)).reshape(heads, ATTN_Q_TILE * span)
    view = tiled[:, :ATTN_Q_TILE * (span - 1)].reshape(heads, ATTN_Q_TILE, span - 1)
    bias = view[:, :, ATTN_Q_TILE - 1:ATTN_Q_TILE - 1 + n_keys]
    return jnp.where(band[None], bias, NEG_INF)


def _head_block_diag(x, head_masks):
    zero = jnp.zeros_like(x)
    return jnp.concatenate([jnp.where(m, x, zero) for m in head_masks], axis=0)


def _hgrn_kernel(hg_ref, lb_ref, ng_ref, s0_ref, o_ref, sfin_ref, st_ref, ot_ref, *, tile):
    i = pl.program_id(1)

    @pl.when(i == 0)
    def _():
        st_ref[...] = s0_ref[...]

    for bb in range(hg_ref.shape[0]):
        _hgrn_rows(hg_ref.at[bb], lb_ref, ng_ref, o_ref.at[bb], st_ref.at[bb], ot_ref.at[bb], tile)

    @pl.when(i == pl.num_programs(1) - 1)
    def _():
        sfin_ref[...] = st_ref[...]


def _hgrn_rows(hg_ref, lb_ref, ng_ref, o_ref, st_ref, ot_ref, tile):
    n_groups = B_WIDTH // HG_GROUP
    cd = o_ref.dtype
    q = hg_ref[:, 0:B_WIDTH]
    f_logit = hg_ref[:, B_WIDTH:2 * B_WIDTH]
    v_in = hg_ref[:, 2 * B_WIDTH:3 * B_WIDTH].astype(cd)
    lb = lb_ref[...]
    f = lb + (1.0 - lb) * (1.0 / (1.0 + jnp.exp(-f_logit)))
    log_f = jnp.log(f)
    kk = 1.0 - f

    r_t = lax.broadcasted_iota(jnp.int32, (tile, tile), 0)
    c_t = lax.broadcasted_iota(jnp.int32, (tile, tile), 1)
    tri = jnp.where((r_t // CHUNK == c_t // CHUNK) & (c_t <= r_t), 1.0, 0.0).astype(BF16)
    split = _split3 if cd == F32 else _split2
    g_all = functools.reduce(lambda x, y: x + y, [_dot(tri, part) for part in reversed(split(log_f))])

    row = lax.broadcasted_iota(jnp.int32, (CHUNK, B_WIDTH), 0)
    upper = row >= HG_HALF
    lane_g = lax.broadcasted_iota(jnp.int32, (CHUNK, HG_GROUP), 1)
    row_g = lax.broadcasted_iota(jnp.int32, (CHUNK, HG_GROUP), 0)
    head_masks = [lane_g // B_KEY_DIM == h for h in range(HG_GROUP // B_KEY_DIM)]
    causal = (lane_g % CHUNK) <= row_g
    cross = (row_g >= HG_HALF) & ((lane_g % CHUNK) < HG_HALF)
    r_bd = lax.broadcasted_iota(jnp.int32, (HG_GROUP, HG_GROUP), 0)
    c_bd = lax.broadcasted_iota(jnp.int32, (HG_GROUP, HG_GROUP), 1)
    diag_blocks = (r_bd // B_VAL_DIM) == (c_bd // B_KEY_DIM)

    pending = []
    for c in range(tile // CHUNK):
        rs = slice(CHUNK * c, CHUNK * (c + 1))
        g = g_all[rs]
        qc = q[rs]
        kc = kk[rs]
        vc = v_in[rs]
        g_q1 = g[HG_QUARTER - 1:HG_QUARTER]
        g_mid = g[HG_HALF - 1:HG_HALF]
        g_q3 = g[HG_HALF + HG_QUARTER - 1:HG_HALF + HG_QUARTER]
        g_last = g[CHUNK - 1:CHUNK]
        d_diag = g - jnp.where(upper, g_q3, g_q1)
        d_off = jnp.where(upper, g - g_mid, g_mid - g)
        q_in = (qc * jnp.exp(g)).astype(cd)
        k_st = (kc * jnp.exp(g_last - g)).astype(cd)
        q_diag = qc * jnp.exp(d_diag)
        k_diag = kc * jnp.exp(-d_diag)
        e_off = jnp.exp(d_off)
        zero = jnp.zeros_like(qc)
        q_d = q_diag.astype(cd)
        k_d = k_diag.astype(cd)
        q_x = jnp.where(upper, qc * e_off, zero).astype(cd)
        k_x = jnp.where(upper, zero, kc * e_off).astype(cd)
        decay = jnp.exp(g_last)
        for gi in range(n_groups):
            cs = slice(HG_GROUP * gi, HG_GROUP * (gi + 1))
            att_d = _dot_nt(q_d[:, cs], _head_block_diag(k_d[:, cs], head_masks))
            att_x = _dot_nt(q_x[:, cs], _head_block_diag(k_x[:, cs], head_masks))
            att = jnp.where(cross, att_x, att_d)
            att = jnp.where(causal, att, 0.0).astype(cd)
            v_bd = _head_block_diag(vc[:, cs], head_masks)
            upd = jnp.where(diag_blocks, _dot_tn(vc[:, cs], k_st[:, cs]), 0.0)
            pending.append((rs, cs, gi, q_in[:, cs], _dot(att, v_bd), decay[:, cs], upd))

    states = [st_ref[gi] for gi in range(n_groups)]
    before = []
    for rs, cs, gi, q_in_g, o_intra, decay_g, upd in pending:
        before.append(states[gi].astype(cd))
        states[gi] = states[gi] * decay_g + upd
    for gi in range(n_groups):
        st_ref[gi] = states[gi]
    for (rs, cs, gi, q_in_g, o_intra, decay_g, upd), st_b in zip(pending, before):
        ot_ref[rs, cs] = _dot_nt(q_in_g, st_b) + o_intra

    o = ot_ref[...]
    lane_i = lax.broadcasted_iota(jnp.int32, (B_WIDTH, B_WIDTH), 0)
    lane_j = lax.broadcasted_iota(jnp.int32, (B_WIDTH, B_WIDTH), 1)
    head_ones = jnp.where(lane_i // B_VAL_DIM == lane_j // B_VAL_DIM, 1.0, 0.0).astype(BF16)
    sq_parts = split(o * o)
    sums = _dot(jnp.concatenate(sq_parts, axis=0), head_ones)
    ms = functools.reduce(lambda x, y: x + y, [sums[tile * j:tile * (j + 1)] for j in reversed(range(len(sq_parts)))])
    ms = ms * (1.0 / B_VAL_DIM)
    gate = hg_ref[:, 3 * B_WIDTH:4 * B_WIDTH]
    out = o * lax.rsqrt(ms + RMS_EPS) * ng_ref[...] * (gate * _sigmoid(gate))
    o_ref[...] = out.astype(cd)


def _hgrn(hg, lb, norm_g, s0_bd, batch, seq, tile, cd):
    nt = seq // tile
    n_groups = B_WIDTH // HG_GROUP
    par = HG_SEQS_PER_STEP
    assert batch % par == 0
    st_blk = (par, n_groups, HG_GROUP, HG_GROUP)
    ob, s_fin = pl.pallas_call(
        functools.partial(_hgrn_kernel, tile=tile),
        grid=(batch // par, nt),
        in_specs=[pl.BlockSpec((par, tile, 4 * B_WIDTH), lambda b, i: (b, i, 0)),
                  _const_spec((1, B_WIDTH)), _const_spec((1, B_WIDTH)),
                  pl.BlockSpec(st_blk, lambda b, i: (b, 0, 0, 0))],
        out_specs=[pl.BlockSpec((par, tile, B_WIDTH), lambda b, i: (b, i, 0)),
                   pl.BlockSpec(st_blk, lambda b, i: (b, 0, 0, 0))],
        out_shape=[jax.ShapeDtypeStruct((batch, seq, B_WIDTH), cd),
                   jax.ShapeDtypeStruct((batch,) + st_blk[1:], F32)],
        scratch_shapes=[pltpu.VMEM(st_blk, F32), pltpu.VMEM((par, tile, B_WIDTH), F32)],
        compiler_params=_params("arbitrary", "arbitrary"),
        name="hgrn",
    )(hg.reshape(batch, seq, 4 * B_WIDTH), lb, norm_g, s0_bd)
    return ob.reshape(batch * seq, B_WIDTH), s_fin


def _state_to_block_diag(s):
    b = s.shape[0]
    hpg = HG_GROUP // B_KEY_DIM
    st = s.astype(F32).reshape(b, B_HEADS // hpg, hpg, B_KEY_DIM, B_VAL_DIM).transpose(0, 1, 2, 4, 3)
    bd = jnp.einsum('bghvc,hk->bghvkc', st, jnp.eye(hpg, dtype=F32))
    return bd.reshape(b, B_HEADS // hpg, HG_GROUP, HG_GROUP)


def _block_diag_to_state(bd):
    b = bd.shape[0]
    hpg = HG_GROUP // B_KEY_DIM
    x = bd.reshape(b, B_HEADS // hpg, hpg, B_VAL_DIM, hpg, B_KEY_DIM)
    st = jnp.einsum('bghvkc,hk->bghvc', x, jnp.eye(hpg, dtype=F32))
    return st.transpose(0, 1, 2, 4, 3).reshape(b, B_HEADS, B_KEY_DIM, B_VAL_DIM)


def _rglru_kernel(rg_ref, conv0_ref, h0_ref, cw_ref, cb_ref, wa_ref, ba_ref, wx_ref, bx_ref, lam_ref,
                  o_ref, hlast_ref, xbuf_ref, hc_ref, *, tile, at_start):
    i = pl.program_id(1)
    pad = 8

    @pl.when(i == 0)
    def _():
        xbuf_ref[0:pad] = conv0_ref[0]
        hc_ref[...] = h0_ref[0]

    xr = rg_ref[:, 0:C_WIDTH]
    gate = rg_ref[:, C_WIDTH:2 * C_WIDTH]
    xbuf_ref[pad:pad + tile] = xr
    xc = cb_ref[...] + cw_ref[C_CONV - 1:C_CONV] * xr
    for j in range(1, C_CONV):
        xc = xc + cw_ref[C_CONV - 1 - j:C_CONV - j] * xbuf_ref[pad - j:pad - j + tile]
    xbuf_ref[0:pad] = xbuf_ref[tile:tile + pad]

    xcb = xc.astype(wa_ref.dtype)
    r = _sigmoid(_dot(xcb, wa_ref[...]) + ba_ref[...])
    ig = _sigmoid(_dot(xcb, wx_ref[...]) + bx_ref[...])
    neg_lam = -lam_ref[...]
    softplus = jnp.maximum(neg_lam, 0.0) + jnp.log(1.0 + jnp.exp(-jnp.abs(neg_lam)))
    a = jnp.exp(r * (-C_GATE_C * softplus))
    mult = jnp.sqrt(1.0 - a * a)
    row = lax.broadcasted_iota(jnp.int32, (tile, C_WIDTH), 0)
    if at_start:
        mult = jnp.where((row == 0) & (i == 0), 1.0, mult)
    b = mult * ig * xc

    d = 1
    while d < tile:
        a_sh = pltpu.roll(a, d, 0)
        b_sh = pltpu.roll(b, d, 0)
        keep = row >= d
        b = jnp.where(keep, a * b_sh + b, b)
        a = jnp.where(keep, a * a_sh, a)
        d *= 2
    h = a * hc_ref[...] + b
    hc_ref[...] = h[tile - 1:tile]
    hlast_ref[0] = h[tile - 1:tile]
    gelu = 0.5 * gate * (1.0 + jnp.tanh(np.sqrt(2.0 / np.pi).astype(np.float32) * (gate + 0.044715 * gate * gate * gate)))
    o_ref[...] = (h * gelu).astype(o_ref.dtype)


def _rglru(rg, conv0_pad, h0, w, batch, seq, tile, at_start):
    nt = seq // tile
    vec = _const_spec((1, C_WIDTH))
    return pl.pallas_call(
        functools.partial(_rglru_kernel, tile=tile, at_start=at_start),
        grid=(batch, nt),
        in_specs=[pl.BlockSpec((tile, 2 * C_WIDTH), lambda b, i: (b * nt + i, 0)),
                  pl.BlockSpec((1, 8, C_WIDTH), lambda b, i: (b, 0, 0)),
                  pl.BlockSpec((1, 1, C_WIDTH), lambda b, i: (b, 0, 0)),
                  _const_spec((C_CONV, C_WIDTH)), vec,
                  _const_spec((C_WIDTH, C_WIDTH)), vec, _const_spec((C_WIDTH, C_WIDTH)), vec, vec],
        out_specs=[pl.BlockSpec((tile, C_WIDTH), lambda b, i: (b * nt + i, 0)),
                   pl.BlockSpec((1, 1, C_WIDTH), lambda b, i: (b, 0, 0))],
        out_shape=[jax.ShapeDtypeStruct((batch * seq, C_WIDTH), w['wa_bd'].dtype),
                   jax.ShapeDtypeStruct((batch, 1, C_WIDTH), F32)],
        scratch_shapes=[pltpu.VMEM((tile + 8, C_WIDTH), F32), pltpu.VMEM((1, C_WIDTH), F32)],
        compiler_params=_params("arbitrary", "arbitrary"),
        name="rglru",
    )(rg, conv0_pad, h0, w['conv_w'], w['conv_b'], w['wa_bd'], w['ba'], w['wx_bd'], w['bx'], w['lam'])


def _block_diag_weight(w):
    n, d, e = w.shape
    return jnp.einsum('nde,nm->ndme', w, jnp.eye(n, dtype=w.dtype)).reshape(n * d, n * e)


def _route(logits_t):
    m = jnp.max(logits_t, axis=0, keepdims=True)
    e = jnp.exp(logits_t - m)
    p = e / jnp.sum(e, axis=0, keepdims=True)
    rows = [p[j:j + 1] for j in range(N_EXPERTS)]
    scores = []
    for g in range(N_GROUPS):
        mem = rows[GROUP_SIZE * g:GROUP_SIZE * (g + 1)]
        best = None
        for a in range(GROUP_SIZE):
            for b in range(a + 1, GROUP_SIZE):
                pair = mem[a] + mem[b]
                best = pair if best is None else jnp.maximum(best, pair)
        scores.append(best)
    smax = functools.reduce(jnp.maximum, scores)
    taken = jnp.zeros_like(smax)
    sel = []
    for g in range(N_GROUPS):
        hit = jnp.where(scores[g] == smax, 1.0, 0.0) * (1.0 - taken)
        taken = taken + hit
        sel.append(hit)
    picked = []
    for j in range(N_EXPERTS):
        g = j // GROUP_SIZE
        rank = jnp.zeros_like(smax)
        for o in range(GROUP_SIZE * g, GROUP_SIZE * (g + 1)):
            if o == j:
                continue
            ahead = (rows[o] >= rows[j]) if o < j else (rows[o] > rows[j])
            rank = rank + jnp.where(ahead, 1.0, 0.0)
        picked.append(sel[g] * jnp.where(rank < float(2), 1.0, 0.0))
    denom = functools.reduce(lambda x, y: x + y, [picked[j] * rows[j] for j in range(N_EXPERTS)])
    comb = [picked[j] * rows[j] / denom for j in range(N_EXPERTS)]
    add = lambda items: functools.reduce(lambda x, y: x + y, items)
    cw = [add([sel[g] * comb[GROUP_SIZE * g + m] for g in range(N_GROUPS)]) for m in range(GROUP_SIZE)]
    on = [add([sel[g] * picked[GROUP_SIZE * g + m] for g in range(N_GROUPS)]) for m in range(GROUP_SIZE)]
    gid = add([float(g) * sel[g] for g in range(1, N_GROUPS)])
    pair_on = [on[lo] * on[hi] for lo, hi in EXPERT_PAIRS]
    w_lo = add([p * cw[lo] for p, (lo, hi) in zip(pair_on, EXPERT_PAIRS)])
    w_hi = add([p * cw[hi] for p, (lo, hi) in zip(pair_on, EXPERT_PAIRS)])
    pair = add([float(k) * p for k, p in enumerate(pair_on) if k > 0])
    segment = gid * float(len(EXPERT_PAIRS)) + pair
    pad = jnp.zeros((ROUTE_ROWS - 3,) + smax.shape[1:], F32)
    return jnp.concatenate([w_lo, w_hi, segment, pad], axis=0)


def _mix_kernel(x_ref, oa_ref, ob_ref, oc_ref, gl_ref, mk_ref, mv_ref, wb_ref, wout_ref, wq_ref, wo_ref,
                g1_ref, b1_ref, g2_ref, b2_ref, rt_ref, x2_ref, route_ref, *, sub):
    tiles = [slice(s0, s0 + sub) for s0 in range(0, x_ref.shape[0], sub)]
    cd = wb_ref.dtype

    mixed = []
    for rows in tiles:
        acc = None
        for b, o_ref in enumerate((oa_ref, ob_ref, oc_ref)):
            per_branch = _dot(o_ref[rows], wb_ref[b])
            gate = _sigmoid(gl_ref[rows, D_MODEL * b:D_MODEL * (b + 1)].astype(F32))
            acc = gate * per_branch if acc is None else acc + gate * per_branch
        mixed.append(acc.astype(cd))

    x1 = [_layer_norm(DN_ALPHA * x_ref[rows] + _dot(m, wout_ref[...]), g1_ref[...], b1_ref[...])
          for rows, m in zip(tiles, mixed)]
    q = [(_dot(t.astype(cd), wq_ref[...]) * (X_HEAD_DIM ** -0.5)).astype(cd) for t in x1]

    heads = [[] for _ in tiles]
    for h in range(X_HEADS):
        sl = slice(X_HEAD_DIM * h, X_HEAD_DIM * (h + 1))
        scores = [_dot_nt(qt[:, sl], mk_ref[0, :, sl]) for qt in q]
        for t, s in enumerate(scores):
            m = jnp.max(s, axis=-1, keepdims=True)
            e = jnp.exp(s - m)
            l = jnp.sum(e, axis=-1, keepdims=True)
            heads[t].append((_dot(e.astype(cd), mv_ref[0, :, sl]) * (1.0 / l)).astype(cd))

    attn = [_dot(jnp.concatenate(hs, axis=-1), wo_ref[...]) for hs in heads]
    x2 = [_layer_norm(DN_ALPHA * a + b, g2_ref[...], b2_ref[...]) for a, b in zip(x1, attn)]

    r_hi, r_lo = _split2(rt_ref[...])
    r_both = jnp.concatenate([r_hi, r_lo], axis=0)
    eye_r = lax.broadcasted_iota(jnp.int32, (ROUTE_ROWS, ROUTE_LANES), 0)
    eye_c = lax.broadcasted_iota(jnp.int32, (ROUTE_ROWS, ROUTE_LANES), 1)
    eye = jnp.where(eye_r == eye_c, 1.0, 0.0).astype(BF16)
    logits = []
    for rows, t in zip(tiles, x2):
        x2_ref[rows] = t
        x_hi, x_lo = _split2(t)
        by_hi = _dot_nt(r_both, x_hi)
        logits.append(by_hi[N_EXPERTS:] + _dot_nt(r_hi, x_lo) + by_hi[:N_EXPERTS])
    routes = [_split3(_route(lt)) for lt in logits]
    for rows, (hi, mid, lo) in zip(tiles, routes):
        route_ref[rows] = _dot_tn(hi, eye) + _dot_tn(mid, eye) + _dot_tn(lo, eye)


def _mix(x, oa, ob, oc, gl, mk, mv, w, batch, seq):
    sub = min(MIX_TILE, seq)
    tm = min(2 * MIX_TILE, seq)
    nt = seq // tm
    row = lambda b, i: (b * nt + i, 0)
    vec = _const_spec((1, D_MODEL))
    n_mem = mk.shape[1]
    mem = pl.BlockSpec((1, n_mem, D_MODEL), lambda b, i: (b, 0, 0))
    sq = _layer_spec((D_MODEL, D_MODEL), w['layer'])
    return pl.pallas_call(
        functools.partial(_mix_kernel, sub=sub),
        grid=(batch, nt),
        in_specs=[pl.BlockSpec((tm, D_MODEL), row)] + [pl.BlockSpec((tm, A_WIDTH), row)] * 3
        + [pl.BlockSpec((tm, N_BRANCH * D_MODEL), row), mem, mem,
           _layer_spec((N_BRANCH, A_WIDTH, D_MODEL), w['layer']), sq, sq, sq, vec, vec, vec, vec,
           _const_spec((N_EXPERTS, D_MODEL))],
        out_specs=[pl.BlockSpec((tm, D_MODEL), row), pl.BlockSpec((tm, ROUTE_LANES), row)],
        out_shape=[jax.ShapeDtypeStruct((batch * seq, D_MODEL), F32),
                   jax.ShapeDtypeStruct((batch * seq, ROUTE_LANES), F32)],
        compiler_params=_params("arbitrary", "arbitrary"),
        name="mix",
    )(x, oa, ob, oc, gl, mk, mv, w['w_branch'], w['w_out'], w['xa_wq'], w['xa_wo'],
      w['ln1_g'], w['ln1_b'], w['ln2_g'], w['ln2_b'], w['router_t'])


def _sc_gather_rows(tables, idx):
    n_out = idx.shape[0]
    per_worker = n_out // SC_WORKERS
    window = min(SC_GATHER_WINDOW, per_worker)
    steps = per_worker // window
    assert per_worker * SC_WORKERS == n_out and steps * window == per_worker and window % 8 == 0
    idx3 = idx.astype(jnp.int32).reshape(SC_WORKERS, steps, window)
    mesh = plsc.VectorSubcoreMesh(core_axis_name="core", subcore_axis_name="subcore")
    n_tab = len(tables)

    def body(*refs):
        tab_hbm = refs[:n_tab]
        idx_hbm = refs[n_tab]
        out_hbm = refs[n_tab + 1:2 * n_tab + 1]
        idx_v = refs[2 * n_tab + 1]
        rows_v = refs[2 * n_tab + 2:3 * n_tab + 2]
        sem = refs[3 * n_tab + 2]
        wid = lax.axis_index("subcore") * SC_CORES + lax.axis_index("core")
        pltpu.sync_copy(idx_hbm.at[wid], idx_v)

        @pl.loop(0, steps)
        def _(j):
            base = wid * per_worker + j * window
            for k in range(n_tab):
                pltpu.async_copy(tab_hbm[k].at[idx_v.at[j]], rows_v[k], sem).wait()
                pltpu.sync_copy(rows_v[k], out_hbm[k].at[pl.ds(base, window)])

    call = pl.kernel(
        body,
        out_type=[jax.ShapeDtypeStruct((n_out, t.shape[1]), t.dtype) for t in tables],
        mesh=mesh,
        scratch_types=[pltpu.VMEM((steps, window), jnp.int32)]
        + [pltpu.VMEM((window, t.shape[1]), t.dtype) for t in tables] + [pltpu.SemaphoreType.DMA],
        name="sc_gather",
    )
    return call(*tables, idx3)


def _dispatch_plan(seg, tm):
    n = seg.shape[0]
    n_pad = n + N_SEGMENTS * tm
    ids = jnp.arange(N_SEGMENTS, dtype=jnp.int32)
    onehot = (seg[:, None] == ids[None, :]).astype(jnp.int32)
    counts = jnp.sum(onehot, axis=0)
    padded = ((counts + tm - 1) // tm) * tm
    start_p = jnp.cumsum(padded) - padded
    start_u = jnp.cumsum(counts) - counts
    order = jnp.argsort(seg, stable=True).astype(jnp.int32)
    rank_sorted = jnp.argsort(order).astype(jnp.int32)
    pos = rank_sorted + jnp.sum(onehot * (start_p - start_u)[None, :], axis=1)
    slot = jnp.arange(n_pad, dtype=jnp.int32)
    seg_slot = jnp.minimum(jnp.sum((slot[:, None] >= (start_p + padded)[None, :]).astype(jnp.int32), axis=1),
                           N_SEGMENTS - 1)
    oh_slot = (seg_slot[:, None] == ids[None, :]).astype(jnp.int32)
    r_slot = slot - jnp.sum(oh_slot * start_p[None, :], axis=1)
    valid = r_slot < jnp.sum(oh_slot * counts[None, :], axis=1)
    s = jnp.where(valid, jnp.sum(oh_slot * start_u[None, :], axis=1) + r_slot, 0)
    src = jnp.take(order, s)
    seg_tile = seg_slot[::tm]
    n_pairs = len(EXPERT_PAIRS)
    pair_lo = jnp.asarray([lo for lo, hi in EXPERT_PAIRS], jnp.int32)
    pair_hi = jnp.asarray([hi for lo, hi in EXPERT_PAIRS], jnp.int32)
    base = GROUP_SIZE * (seg_tile // n_pairs)
    return (pos.astype(jnp.int32), src, base + jnp.take(pair_lo, seg_tile % n_pairs),
            base + jnp.take(pair_hi, seg_tile % n_pairs))


def _moe_kernel(ea_ref, eb_ref, x_ref, r_ref, w1a_ref, w3a_ref, w2a_ref, w1b_ref, w3b_ref, w2b_ref,
                g_ref, b_ref, o_ref):
    del ea_ref, eb_ref
    x = x_ref[...]
    cd = w1a_ref.dtype
    xb = x.astype(cd)
    r = r_ref[...]

    def expert(w1_ref, w3_ref, w2_ref):
        h1 = _dot(xb, w1_ref[...])
        h3 = _dot(xb, w3_ref[...])
        return _dot((h1 * _sigmoid(h1) * h3).astype(cd), w2_ref[...])

    y = r[:, 0:1] * expert(w1a_ref, w3a_ref, w2a_ref) + r[:, 1:2] * expert(w1b_ref, w3b_ref, w2b_ref)
    o_ref[...] = _layer_norm(DN_ALPHA * x + y, g_ref[...], b_ref[...])


def _moe_sorted(xs, rs, tile_ea, tile_eb, w, tm):
    n_pad = xs.shape[0]
    row = lambda i, ea, eb: (i, 0)
    layer = w['layer']
    first = lambda i, ea, eb: (layer, ea[i], 0, 0)
    second = lambda i, ea, eb: (layer, eb[i], 0, 0)
    vec = pl.BlockSpec((1, D_MODEL), lambda i, ea, eb: (0, 0))
    up = lambda which: pl.BlockSpec((None, None, D_MODEL, D_EXPERT), which)
    down = lambda which: pl.BlockSpec((None, None, D_EXPERT, D_MODEL), which)
    return pl.pallas_call(
        _moe_kernel,
        grid_spec=pltpu.PrefetchScalarGridSpec(
            num_scalar_prefetch=2,
            grid=(n_pad // tm,),
            in_specs=[pl.BlockSpec((tm, D_MODEL), row), pl.BlockSpec((tm, ROUTE_LANES), row),
                      up(first), up(first), down(first), up(second), up(second), down(second), vec, vec],
            out_specs=pl.BlockSpec((tm, D_MODEL), row),
        ),
        out_shape=jax.ShapeDtypeStruct((n_pad, D_MODEL), F32),
        compiler_params=_params("arbitrary"),
        name="moe",
    )(tile_ea, tile_eb, xs, rs, w['w1'], w['w3'], w['w2'], w['w1'], w['w3'], w['w2'], w['ln3_g'], w['ln3_b'])


def _moe(x2, route, w):
    n = x2.shape[0]
    tm = min(MOE_TILE, n // 8)
    seg = route[:, 2].astype(jnp.int32)
    pos, src, tile_ea, tile_eb = _dispatch_plan(seg, tm)
    xs, rs = _sc_gather_rows([x2, route], src)
    ys = _moe_sorted(xs, rs, tile_ea, tile_eb, w, tm)
    return _sc_gather_rows([ys], pos)[0]


def _trunk_layer(x, w, mem_k, mem_v, cache_k, cache_v, s0, h0, conv0, batch, seq, prompt):
    qkv, hg, rg, gl = _inproj(x, w['w_in'], w['layer'])
    keep = min(A_WINDOW, seq)
    x_keep = x.reshape(batch, seq, D_MODEL)[:, seq - keep:].reshape(batch * keep, D_MODEL)
    k32, v32 = [t.reshape(batch, keep, A_HEADS, A_HEAD_DIM) for t in _kv_rows(x_keep, w['w_in'], w['layer'])]
    if prompt:
        oa = _attn_prompt(qkv, w['band_bias'], batch, seq)
        new_k, new_v = k32, v32
        tile = 256
    else:
        win = cache_k.shape[1]
        ck = cache_k.reshape(batch, win, A_WIDTH)
        cv = cache_v.reshape(batch, win, A_WIDTH)
        oa = _attn_sample(qkv, ck, cv, w['band_bias'][:, :seq, :win + 2 * CHUNK], batch, seq)
        new_k = jnp.concatenate([cache_k, k32], axis=1)[:, seq:]
        new_v = jnp.concatenate([cache_v, v32], axis=1)[:, seq:]
        tile = seq
    ob, s_bd = _hgrn(hg, w['lb'], w['hgrn_g'], _state_to_block_diag(s0), batch, seq, tile, w['w_in'].dtype)
    conv0_pad = jnp.concatenate([jnp.zeros((batch, 8 - (C_CONV - 1), C_WIDTH), F32), conv0.astype(F32)], axis=1)
    oc, h_new = _rglru(rg, conv0_pad, h0.astype(F32).reshape(batch, 1, C_WIDTH), w, batch, seq, tile, prompt)
    conv_new = jnp.concatenate([conv0.astype(F32), rg.reshape(batch, seq, 2 * C_WIDTH)[:, :, :C_WIDTH]],
                               axis=1)[:, seq:]
    x2, route = _mix(x, oa, ob, oc, gl, mem_k, mem_v, w, batch, seq)
    x3 = _moe(x2, route, w)
    return x3, (new_k, new_v, _block_diag_to_state(s_bd), h_new.reshape(batch, C_WIDTH), conv_new)


def kernel(x_prompt, x_sample, cache_attn_k, cache_attn_v, state_hgrn, state_rglru, state_conv, cache_mem_k, cache_mem_v, mem_prompt, w_in, attn_rel_bias, hgrn_lb_logits, hgrn_norm_g, rg_conv_w, rg_conv_b, rg_wa, rg_ba, rg_wx, rg_bx, rg_lambda, w_branch, w_out, ln1_g, ln1_b, xa_wq, xa_wk, xa_wv, xa_wo, ln2_g, ln2_b, moe_router, moe_w1, moe_w3, moe_w2, ln3_g, ln3_b):
    bp, tp, _ = x_prompt.shape
    bs, ts, _ = x_sample.shape
    n_mem = mem_prompt.shape[1]
    depth = w_in.shape[0]

    p = jax.nn.softmax(hgrn_lb_logits.astype(F32), axis=0)
    lb_all = jnp.cumsum(p, axis=0) - p[0:1]
    vec = lambda t: t.astype(F32).reshape(1, -1)

    xp = x_prompt.reshape(bp * tp, D_MODEL)
    xs = x_sample.reshape(bs * ts, D_MODEL)
    mem2d = mem_prompt.reshape(bp * n_mem, D_MODEL)
    s0p = jnp.zeros((bp, B_HEADS, B_KEY_DIM, B_VAL_DIM), F32)
    h0p = jnp.zeros((bp, C_WIDTH), F32)
    conv0p = jnp.zeros((bp, C_CONV - 1, C_WIDTH), F32)
    router_t = moe_router.astype(F32).T

    stacked = {'w_in': w_in, 'w_branch': w_branch, 'w_out': w_out, 'xa_wq': xa_wq, 'xa_wo': xa_wo,
               'w1': moe_w1, 'w3': moe_w3, 'w2': moe_w2}
    stacked_fast = {k: v.astype(BF16) for k, v in stacked.items()}
    stacked_precise = {k: v.astype(F32) for k, v in stacked.items()}
    wk_b, wv_b = xa_wk.astype(BF16), xa_wv.astype(BF16)

    outs_p, outs_s, mem_ks, mem_vs = [], [], [], []
    for l in range(depth):
        wf = {'wa_bd': _block_diag_weight(rg_wa[l]), 'wx_bd': _block_diag_weight(rg_wx[l])}
        shared = {
            'layer': l, 'band_bias': _band_bias(attn_rel_bias[l]),
            'lb': vec(lb_all[l]), 'hgrn_g': vec(jnp.tile(hgrn_norm_g[l], B_HEADS)),
            'conv_w': rg_conv_w[l].astype(F32), 'conv_b': vec(rg_conv_b[l]),
            'ba': vec(rg_ba[l]), 'bx': vec(rg_bx[l]), 'lam': vec(rg_lambda[l]),
            'ln1_g': vec(ln1_g[l]), 'ln1_b': vec(ln1_b[l]), 'ln2_g': vec(ln2_g[l]), 'ln2_b': vec(ln2_b[l]),
            'ln3_g': vec(ln3_g[l]), 'ln3_b': vec(ln3_b[l]), 'router_t': router_t,
        }
        w_fast = dict(shared, **stacked_fast, **{k: v.astype(BF16) for k, v in wf.items()})
        w_precise = dict(shared, **stacked_precise, **{k: v.astype(F32) for k, v in wf.items()})
        mk_p = _matmul(mem2d, wk_b, l)
        mv_p = _matmul(mem2d, wv_b, l)
        mem_ks.append(mk_p.reshape(bp, n_mem, X_HEADS, X_HEAD_DIM))
        mem_vs.append(mv_p.reshape(bp, n_mem, X_HEADS, X_HEAD_DIM))
        xp, st_p = _trunk_layer(xp, w_fast, mk_p.reshape(bp, n_mem, D_MODEL).astype(BF16),
                                mv_p.reshape(bp, n_mem, D_MODEL).astype(BF16),
                                None, None, s0p, h0p, conv0p, bp, tp, True)
        xs, st_s = _trunk_layer(xs, w_precise, cache_mem_k[l].reshape(bs, n_mem, D_MODEL).astype(F32),
                                cache_mem_v[l].reshape(bs, n_mem, D_MODEL).astype(F32),
                                cache_attn_k[l], cache_attn_v[l], state_hgrn[l], state_rglru[l], state_conv[l],
                                bs, ts, False)
        outs_p.append(st_p)
        outs_s.append(st_s)

    stack = lambda items, j: jnp.stack([it[j] for it in items])
    return (xp.reshape(bp, tp, D_MODEL), xs.reshape(bs, ts, D_MODEL),
            stack(outs_p, 0), stack(outs_p, 1), stack(outs_p, 2), stack(outs_p, 3), stack(outs_p, 4),
            jnp.stack(mem_ks), jnp.stack(mem_vs),
            stack(outs_s, 0), stack(outs_s, 1), stack(outs_s, 2), stack(outs_s, 3), stack(outs_s, 4))
```

```python
import functools

import numpy as np
import jax
import jax.numpy as jnp
from jax import lax
from jax.experimental import pallas as pl
from jax.experimental.pallas import tpu as pltpu
from jax.experimental.pallas import tpu_sc as plsc

F32 = jnp.float32
BF16 = jnp.bfloat16

D_MODEL = 1024
CHUNK = 64
A_HEADS = 8
A_HEAD_DIM = 64
A_WIDTH = A_HEADS * A_HEAD_DIM
A_PAST_CHUNKS = 8
A_WINDOW = A_PAST_CHUNKS * CHUNK
A_MAX_REL = 256
B_HEADS = 8
B_KEY_DIM = 64
B_VAL_DIM = 64
B_WIDTH = B_HEADS * B_KEY_DIM
C_WIDTH = 512
C_CONV = 4
C_GATE_C = 8.0
N_BRANCH = 3
IN_COLS = 3 * A_WIDTH + 4 * B_WIDTH + 2 * C_WIDTH + N_BRANCH * D_MODEL
X_HEADS = 4
X_HEAD_DIM = D_MODEL // X_HEADS
N_EXPERTS = 16
N_GROUPS = 4
GROUP_SIZE = N_EXPERTS // N_GROUPS
D_EXPERT = D_MODEL // 2
DEPTH = 4
DN_ALPHA = (2 * DEPTH) ** 0.25
LN_EPS = 1e-5
RMS_EPS = 1e-6
NEG_INF = -1e30

VMEM_LIMIT_BYTES = 56 * 1024 * 1024

INPROJ_TILE = 512
MIX_TILE = 256
ATTN_Q_TILE = 256
ATTN_K_TILE = 256
HG_GROUP = 256
HG_SEQS_PER_STEP = 2
HG_HALF = CHUNK // 2
HG_QUARTER = CHUNK // 4
EXPERT_PAIRS = tuple((lo, hi) for lo in range(GROUP_SIZE) for hi in range(lo + 1, GROUP_SIZE))
N_SEGMENTS = N_GROUPS * len(EXPERT_PAIRS)
ROUTE_ROWS = 16
ROUTE_PAIR_COL = 2
ROUTE_GROUP_COL = 7
ROUTE_LANES = 128
MOE_TILE = 512
SC_CORES = 2
SC_SUBCORES = 16
SC_WORKERS = SC_CORES * SC_SUBCORES
SC_GATHER_WINDOW = 32


def _split2(x):
    hi = x.astype(BF16)
    lo = (x - hi.astype(F32)).astype(BF16)
    return hi, lo


def _contract(a, b, dims):
    dg = lambda u, v: lax.dot_general(u, v, (dims, ((), ())), preferred_element_type=F32)
    if a.dtype == F32:
        a_hi, a_lo = _split2(a)
        b_hi, b_lo = _split2(b)
        return dg(a_lo, b_hi) + dg(a_hi, b_lo) + dg(a_hi, b_hi)
    return dg(a, b)


def _dot(a, b):
    return _contract(a, b, ((1,), (0,)))


def _dot_nt(a, b):
    return _contract(a, b, ((1,), (1,)))


def _dot_tn(a, b):
    return _contract(a, b, ((0,), (0,)))


def _split3(x):
    hi = x.astype(BF16)
    r1 = x - hi.astype(F32)
    mid = r1.astype(BF16)
    lo = (r1 - mid.astype(F32)).astype(BF16)
    return hi, mid, lo


def _sigmoid(x):
    return 0.5 * jnp.tanh(0.5 * x) + 0.5


def _layer_norm(x, g, b):
    mu = jnp.mean(x, axis=-1, keepdims=True)
    xc = x - mu
    var = jnp.mean(xc * xc, axis=-1, keepdims=True)
    return xc * lax.rsqrt(var + LN_EPS) * g + b


def _params(*semantics):
    return pltpu.CompilerParams(dimension_semantics=semantics, vmem_limit_bytes=VMEM_LIMIT_BYTES)


def _const_spec(shape):
    nd = len(shape)
    return pl.BlockSpec(shape, lambda *_: (0,) * nd, pipeline_mode=pl.Buffered(1))


def _layer_spec(shape, layer):
    nd = len(shape)
    return pl.BlockSpec((None,) + tuple(shape), lambda *_: (layer,) + (0,) * nd, pipeline_mode=pl.Buffered(1))


def _inproj_kernel(x_ref, w_ref, qkv_ref, hg_ref, rg_ref, gl_ref):
    xb = x_ref[...].astype(w_ref.dtype)
    cw = 512

    def mm(c0):
        return _dot(xb, w_ref[:, c0:c0 + cw])

    for j in range(3):
        qkv_ref[:, cw * j:cw * (j + 1)] = mm(cw * j).astype(qkv_ref.dtype)
    base = 3 * A_WIDTH
    for j in range(4):
        hg_ref[:, cw * j:cw * (j + 1)] = mm(base + cw * j)
    base += 4 * B_WIDTH
    for j in range(2):
        rg_ref[:, cw * j:cw * (j + 1)] = mm(base + cw * j)
    base += 2 * C_WIDTH
    for j in range(N_BRANCH * D_MODEL // cw):
        gl_ref[:, cw * j:cw * (j + 1)] = mm(base + cw * j).astype(gl_ref.dtype)


def _inproj(x, w_in, layer):
    n = x.shape[0]
    cd = w_in.dtype
    tm = INPROJ_TILE if cd == BF16 else INPROJ_TILE // 2
    assert n % tm == 0
    row = lambda i: (i, 0)
    return pl.pallas_call(
        _inproj_kernel,
        grid=(n // tm,),
        in_specs=[pl.BlockSpec((tm, D_MODEL), row), _layer_spec((D_MODEL, IN_COLS), layer)],
        out_specs=[pl.BlockSpec((tm, 3 * A_WIDTH), row),
                   pl.BlockSpec((tm, 4 * B_WIDTH), row), pl.BlockSpec((tm, 2 * C_WIDTH), row),
                   pl.BlockSpec((tm, N_BRANCH * D_MODEL), row)],
        out_shape=[jax.ShapeDtypeStruct((n, 3 * A_WIDTH), cd),
                   jax.ShapeDtypeStruct((n, 4 * B_WIDTH), F32), jax.ShapeDtypeStruct((n, 2 * C_WIDTH), F32),
                   jax.ShapeDtypeStruct((n, N_BRANCH * D_MODEL), cd)],
        compiler_params=_params("arbitrary"),
        name="inproj",
    )(x, w_in)


def _matmul_kernel(x_ref, w_ref, o_ref):
    o_ref[...] = _dot(x_ref[...].astype(w_ref.dtype), w_ref[...])


def _matmul(x, w, layer):
    n, k = x.shape
    m = w.shape[2]
    tm = 256
    return pl.pallas_call(
        _matmul_kernel,
        grid=(n // tm,),
        in_specs=[pl.BlockSpec((tm, k), lambda i: (i, 0)), _layer_spec((k, m), layer)],
        out_specs=pl.BlockSpec((tm, m), lambda i: (i, 0)),
        out_shape=jax.ShapeDtypeStruct((n, m), F32),
        compiler_params=_params("arbitrary"),
        name="matmul",
    )(x, w)


def _kv_kernel(x_ref, wk_ref, wv_ref, k_ref, v_ref):
    xb = x_ref[...].astype(wk_ref.dtype)
    k_ref[...] = _dot(xb, wk_ref[...])
    v_ref[...] = _dot(xb, wv_ref[...])


def _kv_rows(x, w_in, layer):
    n = x.shape[0]
    tm = 256
    col = lambda j: pl.BlockSpec((None, D_MODEL, A_WIDTH), lambda i: (layer, 0, j), pipeline_mode=pl.Buffered(1))
    out = pl.BlockSpec((tm, A_WIDTH), lambda i: (i, 0))
    return pl.pallas_call(
        _kv_kernel,
        grid=(n // tm,),
        in_specs=[pl.BlockSpec((tm, D_MODEL), lambda i: (i, 0)), col(1), col(2)],
        out_specs=[out, out],
        out_shape=[jax.ShapeDtypeStruct((n, A_WIDTH), F32)] * 2,
        compiler_params=_params("arbitrary"),
        name="kv_rows",
    )(x, w_in, w_in)


def _attn_core(q, k, v, bias_ref, valid):
    rows = q.shape[0]
    lane = lax.broadcasted_iota(jnp.int32, (rows, 2 * A_HEAD_DIM), 1)
    first = lane < A_HEAD_DIM
    q = q * jnp.asarray(A_HEAD_DIM ** -0.5, q.dtype)

    def scores(head):
        sl = slice(2 * A_HEAD_DIM * (head // 2), 2 * A_HEAD_DIM * (head // 2 + 1))
        sel = first if head % 2 == 0 else jnp.logical_not(first)
        qm = jnp.where(sel, q[:, sl], jnp.zeros_like(q[:, sl]))
        s = _dot_nt(qm, k[:, sl]) + bias_ref[head]
        return s if valid is None else jnp.where(valid, s, NEG_INF)

    outs = []
    pair = None
    s_next = scores(0)
    for head in range(A_HEADS):
        s = s_next
        if head + 1 < A_HEADS:
            s_next = scores(head + 1)
        m = jnp.max(s, axis=-1, keepdims=True)
        e = jnp.exp(s - m)
        l = jnp.sum(e, axis=-1, keepdims=True)
        sl = slice(2 * A_HEAD_DIM * (head // 2), 2 * A_HEAD_DIM * (head // 2 + 1))
        o = _dot(e.astype(v.dtype), v[:, sl]) * (1.0 / l)
        if head % 2 == 0:
            pair = o
        else:
            outs.append(jnp.where(first, pair, o))
    return jnp.concatenate(outs, axis=-1)


def _attn_prompt_kernel(q_ref, k0_ref, k1_ref, k2_ref, v0_ref, v1_ref, v2_ref, bias_ref, o_ref):
    i = pl.program_id(1)

    def run(masked):
        k = jnp.concatenate([k0_ref[...], k1_ref[...], k2_ref[...]], axis=0)
        v = jnp.concatenate([v0_ref[...], v1_ref[...], v2_ref[...]], axis=0)
        valid = None
        if masked:
            col = lax.broadcasted_iota(jnp.int32, (ATTN_Q_TILE, 3 * ATTN_K_TILE), 1)
            valid = col >= (2 - i) * ATTN_K_TILE
        o_ref[...] = _attn_core(q_ref[...], k, v, bias_ref, valid).astype(o_ref.dtype)

    pl.when(i < 2)(lambda: run(True))
    pl.when(i >= 2)(lambda: run(False))


def _attn_prompt(qkv, bias, batch, seq):
    nt = seq // ATTN_Q_TILE
    blk = (ATTN_Q_TILE, A_WIDTH)

    def kv_spec(j, col):
        return pl.BlockSpec(blk, lambda b, i: (b * nt + jnp.maximum(i - 2 + j, 0), col))

    return pl.pallas_call(
        _attn_prompt_kernel,
        grid=(batch, nt),
        in_specs=[pl.BlockSpec(blk, lambda b, i: (b * nt + i, 0))]
        + [kv_spec(j, 1) for j in range(3)] + [kv_spec(j, 2) for j in range(3)]
        + [_const_spec(bias.shape)],
        out_specs=pl.BlockSpec(blk, lambda b, i: (b * nt + i, 0)),
        out_shape=jax.ShapeDtypeStruct((batch * seq, A_WIDTH), qkv.dtype),
        compiler_params=_params("arbitrary", "arbitrary"),
        name="attn_prompt",
    )(qkv, qkv, qkv, qkv, qkv, qkv, qkv, bias)


def _attn_sample_kernel(q_ref, kn_ref, vn_ref, ck_ref, cv_ref, bias_ref, o_ref):
    cd = q_ref.dtype
    pad = jnp.zeros((CHUNK, A_WIDTH), cd)
    k = jnp.concatenate([ck_ref[0].astype(cd), kn_ref[...], pad], axis=0)
    v = jnp.concatenate([cv_ref[0].astype(cd), vn_ref[...], pad], axis=0)
    o_ref[...] = _attn_core(q_ref[...], k, v, bias_ref, None).astype(cd)


def _attn_sample(qkv, cache_k, cache_v, bias, batch, seq):
    win = cache_k.shape[1]
    blk = (seq, A_WIDTH)
    return pl.pallas_call(
        _attn_sample_kernel,
        grid=(batch,),
        in_specs=[pl.BlockSpec(blk, lambda b: (b, 0)), pl.BlockSpec(blk, lambda b: (b, 1)),
                  pl.BlockSpec(blk, lambda b: (b, 2)),
                  pl.BlockSpec((1, win, A_WIDTH), lambda b: (b, 0, 0)),
                  pl.BlockSpec((1, win, A_WIDTH), lambda b: (b, 0, 0)),
                  _const_spec(bias.shape)],
        out_specs=pl.BlockSpec(blk, lambda b: (b, 0)),
        out_shape=jax.ShapeDtypeStruct((batch * seq, A_WIDTH), qkv.dtype),
        compiler_params=_params("arbitrary"),
        name="attn_sample",
    )(qkv, qkv, qkv, cache_k, cache_v, bias)


def _band_bias(table):
    heads = table.shape[0]
    n_keys = 3 * ATTN_K_TILE
    span = ATTN_Q_TILE + n_keys
    r = np.arange(ATTN_Q_TILE)[:, None]
    j = np.arange(n_keys)[None, :]
    band = (j // CHUNK >= r // CHUNK) & (j // CHUNK <= r // CHUNK + A_PAST_CHUNKS)
    t = table.astype(F32)
    u = jnp.concatenate([t, jnp.broadcast_to(t[:, -1:], (heads, span - t.shape[1]))], axis=1)
    w = u[:, ::-1]
    tiled = jnp.broadcast_to(w[:, None, :], (heads, ATTN_Q_TILE, span)).reshape(heads, ATTN_Q_TILE * span)
    view = tiled[:, :ATTN_Q_TILE * (span - 1)].reshape(heads, ATTN_Q_TILE, span - 1)
    bias = view[:, :, ATTN_Q_TILE - 1:ATTN_Q_TILE - 1 + n_keys]
    return jnp.where(band[None], bias, NEG_INF)


def _head_block_diag(x, head_masks):
    zero = jnp.zeros_like(x)
    return jnp.concatenate([jnp.where(m, x, zero) for m in head_masks], axis=0)


def _hgrn_kernel(hg_ref, lb_ref, ng_ref, s0_ref, o_ref, sfin_ref, st_ref, ot_ref, *, tile):
    i = pl.program_id(1)

    @pl.when(i == 0)
    def _():
        st_ref[...] = s0_ref[...]

    for bb in range(hg_ref.shape[0]):
        _hgrn_rows(hg_ref.at[bb], lb_ref, ng_ref, o_ref.at[bb], st_ref.at[bb], ot_ref.at[bb], tile)

    @pl.when(i == pl.num_programs(1) - 1)
    def _():
        sfin_ref[...] = st_ref[...]


def _hgrn_rows(hg_ref, lb_ref, ng_ref, o_ref, st_ref, ot_ref, tile):
    n_groups = B_WIDTH // HG_GROUP
    cd = o_ref.dtype
    q = hg_ref[:, 0:B_WIDTH]
    f_logit = hg_ref[:, B_WIDTH:2 * B_WIDTH]
    v_in = hg_ref[:, 2 * B_WIDTH:3 * B_WIDTH].astype(cd)
    lb = lb_ref[...]
    f = lb + (1.0 - lb) * (1.0 / (1.0 + jnp.exp(-f_logit)))
    log_f = jnp.log(f)
    kk = 1.0 - f

    r_t = lax.broadcasted_iota(jnp.int32, (tile, tile), 0)
    c_t = lax.broadcasted_iota(jnp.int32, (tile, tile), 1)
    tri = jnp.where((r_t // CHUNK == c_t // CHUNK) & (c_t <= r_t), 1.0, 0.0).astype(BF16)
    split = _split3 if cd == F32 else _split2
    g_all = functools.reduce(lambda x, y: x + y, [_dot(tri, part) for part in reversed(split(log_f))])

    row = lax.broadcasted_iota(jnp.int32, (CHUNK, B_WIDTH), 0)
    upper = row >= HG_HALF
    lane_g = lax.broadcasted_iota(jnp.int32, (CHUNK, HG_GROUP), 1)
    row_g = lax.broadcasted_iota(jnp.int32, (CHUNK, HG_GROUP), 0)
    head_masks = [lane_g // B_KEY_DIM == h for h in range(HG_GROUP // B_KEY_DIM)]
    causal = (lane_g % CHUNK) <= row_g
    cross = (row_g >= HG_HALF) & ((lane_g % CHUNK) < HG_HALF)
    r_bd = lax.broadcasted_iota(jnp.int32, (HG_GROUP, HG_GROUP), 0)
    c_bd = lax.broadcasted_iota(jnp.int32, (HG_GROUP, HG_GROUP), 1)
    diag_blocks = (r_bd // B_VAL_DIM) == (c_bd // B_KEY_DIM)

    pending = []
    for c in range(tile // CHUNK):
        rs = slice(CHUNK * c, CHUNK * (c + 1))
        g = g_all[rs]
        qc = q[rs]
        kc = kk[rs]
        vc = v_in[rs]
        g_q1 = g[HG_QUARTER - 1:HG_QUARTER]
        g_mid = g[HG_HALF - 1:HG_HALF]
        g_q3 = g[HG_HALF + HG_QUARTER - 1:HG_HALF + HG_QUARTER]
        g_last = g[CHUNK - 1:CHUNK]
        d_diag = g - jnp.where(upper, g_q3, g_q1)
        d_off = jnp.where(upper, g - g_mid, g_mid - g)
        q_in = (qc * jnp.exp(g)).astype(cd)
        k_st = (kc * jnp.exp(g_last - g)).astype(cd)
        q_diag = qc * jnp.exp(d_diag)
        k_diag = kc * jnp.exp(-d_diag)
        e_off = jnp.exp(d_off)
        zero = jnp.zeros_like(qc)
        q_d = q_diag.astype(cd)
        k_d = k_diag.astype(cd)
        q_x = jnp.where(upper, qc * e_off, zero).astype(cd)
        k_x = jnp.where(upper, zero, kc * e_off).astype(cd)
        decay = jnp.exp(g_last)
        for gi in range(n_groups):
            cs = slice(HG_GROUP * gi, HG_GROUP * (gi + 1))
            att_d = _dot_nt(q_d[:, cs], _head_block_diag(k_d[:, cs], head_masks))
            att_x = _dot_nt(q_x[:, cs], _head_block_diag(k_x[:, cs], head_masks))
            att = jnp.where(cross, att_x, att_d)
            att = jnp.where(causal, att, 0.0).astype(cd)
            v_bd = _head_block_diag(vc[:, cs], head_masks)
            upd = jnp.where(diag_blocks, _dot_tn(vc[:, cs], k_st[:, cs]), 0.0)
            pending.append((rs, cs, gi, q_in[:, cs], _dot(att, v_bd), decay[:, cs], upd))

    states = [st_ref[gi] for gi in range(n_groups)]
    before = []
    for rs, cs, gi, q_in_g, o_intra, decay_g, upd in pending:
        before.append(states[gi].astype(cd))
        states[gi] = states[gi] * decay_g + upd
    for gi in range(n_groups):
        st_ref[gi] = states[gi]
    for (rs, cs, gi, q_in_g, o_intra, decay_g, upd), st_b in zip(pending, before):
        ot_ref[rs, cs] = _dot_nt(q_in_g, st_b) + o_intra

    o = ot_ref[...]
    lane_i = lax.broadcasted_iota(jnp.int32, (B_WIDTH, B_WIDTH), 0)
    lane_j = lax.broadcasted_iota(jnp.int32, (B_WIDTH, B_WIDTH), 1)
    head_ones = jnp.where(lane_i // B_VAL_DIM == lane_j // B_VAL_DIM, 1.0, 0.0).astype(BF16)
    sq_parts = split(o * o)
    sums = _dot(jnp.concatenate(sq_parts, axis=0), head_ones)
    ms = functools.reduce(lambda x, y: x + y, [sums[tile * j:tile * (j + 1)] for j in reversed(range(len(sq_parts)))])
    ms = ms * (1.0 / B_VAL_DIM)
    gate = hg_ref[:, 3 * B_WIDTH:4 * B_WIDTH]
    out = o * lax.rsqrt(ms + RMS_EPS) * ng_ref[...] * (gate * _sigmoid(gate))
    o_ref[...] = out.astype(cd)


def _hgrn(hg, lb, norm_g, s0_bd, batch, seq, tile, cd):
    nt = seq // tile
    n_groups = B_WIDTH // HG_GROUP
    par = HG_SEQS_PER_STEP
    assert batch % par == 0
    st_blk = (par, n_groups, HG_GROUP, HG_GROUP)
    ob, s_fin = pl.pallas_call(
        functools.partial(_hgrn_kernel, tile=tile),
        grid=(batch // par, nt),
        in_specs=[pl.BlockSpec((par, tile, 4 * B_WIDTH), lambda b, i: (b, i, 0)),
                  _const_spec((1, B_WIDTH)), _const_spec((1, B_WIDTH)),
                  pl.BlockSpec(st_blk, lambda b, i: (b, 0, 0, 0))],
        out_specs=[pl.BlockSpec((par, tile, B_WIDTH), lambda b, i: (b, i, 0)),
                   pl.BlockSpec(st_blk, lambda b, i: (b, 0, 0, 0))],
        out_shape=[jax.ShapeDtypeStruct((batch, seq, B_WIDTH), cd),
                   jax.ShapeDtypeStruct((batch,) + st_blk[1:], F32)],
        scratch_shapes=[pltpu.VMEM(st_blk, F32), pltpu.VMEM((par, tile, B_WIDTH), F32)],
        compiler_params=_params("arbitrary", "arbitrary"),
        name="hgrn",
    )(hg.reshape(batch, seq, 4 * B_WIDTH), lb, norm_g, s0_bd)
    return ob.reshape(batch * seq, B_WIDTH), s_fin


def _state_to_block_diag(s):
    b = s.shape[0]
    hpg = HG_GROUP // B_KEY_DIM
    st = s.astype(F32).reshape(b, B_HEADS // hpg, hpg, B_KEY_DIM, B_VAL_DIM).transpose(0, 1, 2, 4, 3)
    bd = jnp.einsum('bghvc,hk->bghvkc', st, jnp.eye(hpg, dtype=F32))
    return bd.reshape(b, B_HEADS // hpg, HG_GROUP, HG_GROUP)


def _block_diag_to_state(bd):
    b = bd.shape[0]
    hpg = HG_GROUP // B_KEY_DIM
    x = bd.reshape(b, B_HEADS // hpg, hpg, B_VAL_DIM, hpg, B_KEY_DIM)
    st = jnp.einsum('bghvkc,hk->bghvc', x, jnp.eye(hpg, dtype=F32))
    return st.transpose(0, 1, 2, 4, 3).reshape(b, B_HEADS, B_KEY_DIM, B_VAL_DIM)


def _rglru_kernel(rg_ref, conv0_ref, h0_ref, cw_ref, cb_ref, wa_ref, ba_ref, wx_ref, bx_ref, lam_ref,
                  o_ref, hlast_ref, xbuf_ref, hc_ref, *, tile, at_start):
    i = pl.program_id(1)
    pad = 8

    @pl.when(i == 0)
    def _():
        xbuf_ref[0:pad] = conv0_ref[0]
        hc_ref[...] = h0_ref[0]

    xr = rg_ref[:, 0:C_WIDTH]
    gate = rg_ref[:, C_WIDTH:2 * C_WIDTH]
    xbuf_ref[pad:pad + tile] = xr
    xc = cb_ref[...] + cw_ref[C_CONV - 1:C_CONV] * xr
    for j in range(1, C_CONV):
        xc = xc + cw_ref[C_CONV - 1 - j:C_CONV - j] * xbuf_ref[pad - j:pad - j + tile]
    xbuf_ref[0:pad] = xbuf_ref[tile:tile + pad]

    xcb = xc.astype(wa_ref.dtype)
    r = _sigmoid(_dot(xcb, wa_ref[...]) + ba_ref[...])
    ig = _sigmoid(_dot(xcb, wx_ref[...]) + bx_ref[...])
    neg_lam = -lam_ref[...]
    softplus = jnp.maximum(neg_lam, 0.0) + jnp.log(1.0 + jnp.exp(-jnp.abs(neg_lam)))
    a = jnp.exp(r * (-C_GATE_C * softplus))
    mult = jnp.sqrt(1.0 - a * a)
    row = lax.broadcasted_iota(jnp.int32, (tile, C_WIDTH), 0)
    if at_start:
        mult = jnp.where((row == 0) & (i == 0), 1.0, mult)
    b = mult * ig * xc

    d = 1
    while d < tile:
        a_sh = pltpu.roll(a, d, 0)
        b_sh = pltpu.roll(b, d, 0)
        keep = row >= d
        b = jnp.where(keep, a * b_sh + b, b)
        a = jnp.where(keep, a * a_sh, a)
        d *= 2
    h = a * hc_ref[...] + b
    hc_ref[...] = h[tile - 1:tile]
    hlast_ref[0] = h[tile - 1:tile]
    gelu = 0.5 * gate * (1.0 + jnp.tanh(np.sqrt(2.0 / np.pi).astype(np.float32) * (gate + 0.044715 * gate * gate * gate)))
    o_ref[...] = (h * gelu).astype(o_ref.dtype)


def _rglru(rg, conv0_pad, h0, w, batch, seq, tile, at_start):
    nt = seq // tile
    vec = _const_spec((1, C_WIDTH))
    return pl.pallas_call(
        functools.partial(_rglru_kernel, tile=tile, at_start=at_start),
        grid=(batch, nt),
        in_specs=[pl.BlockSpec((tile, 2 * C_WIDTH), lambda b, i: (b * nt + i, 0)),
                  pl.BlockSpec((1, 8, C_WIDTH), lambda b, i: (b, 0, 0)),
                  pl.BlockSpec((1, 1, C_WIDTH), lambda b, i: (b, 0, 0)),
                  _const_spec((C_CONV, C_WIDTH)), vec,
                  _const_spec((C_WIDTH, C_WIDTH)), vec, _const_spec((C_WIDTH, C_WIDTH)), vec, vec],
        out_specs=[pl.BlockSpec((tile, C_WIDTH), lambda b, i: (b * nt + i, 0)),
                   pl.BlockSpec((1, 1, C_WIDTH), lambda b, i: (b, 0, 0))],
        out_shape=[jax.ShapeDtypeStruct((batch * seq, C_WIDTH), w['wa_bd'].dtype),
                   jax.ShapeDtypeStruct((batch, 1, C_WIDTH), F32)],
        scratch_shapes=[pltpu.VMEM((tile + 8, C_WIDTH), F32), pltpu.VMEM((1, C_WIDTH), F32)],
        compiler_params=_params("arbitrary", "arbitrary"),
        name="rglru",
    )(rg, conv0_pad, h0, w['conv_w'], w['conv_b'], w['wa_bd'], w['ba'], w['wx_bd'], w['bx'], w['lam'])


def _block_diag_weight(w):
    n, d, e = w.shape
    return jnp.einsum('nde,nm->ndme', w, jnp.eye(n, dtype=w.dtype)).reshape(n * d, n * e)


def _route(logits_t):
    m = jnp.max(logits_t, axis=0, keepdims=True)
    e = jnp.exp(logits_t - m)
    p = e / jnp.sum(e, axis=0, keepdims=True)
    rows = [p[j:j + 1] for j in range(N_EXPERTS)]
    scores = []
    for g in range(N_GROUPS):
        mem = rows[GROUP_SIZE * g:GROUP_SIZE * (g + 1)]
        best = None
        for a in range(GROUP_SIZE):
            for b in range(a + 1, GROUP_SIZE):
                pair = mem[a] + mem[b]
                best = pair if best is None else jnp.maximum(best, pair)
        scores.append(best)
    smax = functools.reduce(jnp.maximum, scores)
    taken = jnp.zeros_like(smax)
    sel = []
    for g in range(N_GROUPS):
        hit = jnp.where(scores[g] == smax, 1.0, 0.0) * (1.0 - taken)
        taken = taken + hit
        sel.append(hit)
    picked = []
    for j in range(N_EXPERTS):
        g = j // GROUP_SIZE
        rank = jnp.zeros_like(smax)
        for o in range(GROUP_SIZE * g, GROUP_SIZE * (g + 1)):
            if o == j:
                continue
            ahead = (rows[o] >= rows[j]) if o < j else (rows[o] > rows[j])
            rank = rank + jnp.where(ahead, 1.0, 0.0)
        picked.append(sel[g] * jnp.where(rank < float(2), 1.0, 0.0))
    denom = functools.reduce(lambda x, y: x + y, [picked[j] * rows[j] for j in range(N_EXPERTS)])
    comb = [picked[j] * rows[j] / denom for j in range(N_EXPERTS)]
    add = lambda items: functools.reduce(lambda x, y: x + y, items)
    cw = [add([sel[g] * comb[GROUP_SIZE * g + m] for g in range(N_GROUPS)]) for m in range(GROUP_SIZE)]
    on = [add([sel[g] * picked[GROUP_SIZE * g + m] for g in range(N_GROUPS)]) for m in range(GROUP_SIZE)]
    gid = add([float(g) * sel[g] for g in range(1, N_GROUPS)])
    pair_on = [on[lo] * on[hi] for lo, hi in EXPERT_PAIRS]
    w_lo = add([p * cw[lo] for p, (lo, hi) in zip(pair_on, EXPERT_PAIRS)])
    w_hi = add([p * cw[hi] for p, (lo, hi) in zip(pair_on, EXPERT_PAIRS)])
    pair = add([float(k) * p for k, p in enumerate(pair_on) if k > 0])
    segment = gid * float(len(EXPERT_PAIRS)) + pair
    pad = jnp.zeros((ROUTE_ROWS - ROUTE_GROUP_COL - 1,) + smax.shape[1:], F32)
    return jnp.concatenate([w_lo, w_hi, segment] + cw + [gid, pad], axis=0)


def _mix_kernel(x_ref, oa_ref, ob_ref, oc_ref, gl_ref, mk_ref, mv_ref, wb_ref, wout_ref, wq_ref, wo_ref,
                g1_ref, b1_ref, g2_ref, b2_ref, rt_ref, x2_ref, route_ref, *, sub):
    tiles = [slice(s0, s0 + sub) for s0 in range(0, x_ref.shape[0], sub)]
    cd = wb_ref.dtype

    mixed = []
    for rows in tiles:
        acc = None
        for b, o_ref in enumerate((oa_ref, ob_ref, oc_ref)):
            per_branch = _dot(o_ref[rows], wb_ref[b])
            gate = _sigmoid(gl_ref[rows, D_MODEL * b:D_MODEL * (b + 1)].astype(F32))
            acc = gate * per_branch if acc is None else acc + gate * per_branch
        mixed.append(acc.astype(cd))

    x1 = [_layer_norm(DN_ALPHA * x_ref[rows] + _dot(m, wout_ref[...]), g1_ref[...], b1_ref[...])
          for rows, m in zip(tiles, mixed)]
    q = [(_dot(t.astype(cd), wq_ref[...]) * (X_HEAD_DIM ** -0.5)).astype(cd) for t in x1]

    heads = [[] for _ in tiles]
    for h in range(X_HEADS):
        sl = slice(X_HEAD_DIM * h, X_HEAD_DIM * (h + 1))
        scores = [_dot_nt(qt[:, sl], mk_ref[0, :, sl]) for qt in q]
        for t, s in enumerate(scores):
            m = jnp.max(s, axis=-1, keepdims=True)
            e = jnp.exp(s - m)
            l = jnp.sum(e, axis=-1, keepdims=True)
            heads[t].append((_dot(e.astype(cd), mv_ref[0, :, sl]) * (1.0 / l)).astype(cd))

    attn = [_dot(jnp.concatenate(hs, axis=-1), wo_ref[...]) for hs in heads]
    x2 = [_layer_norm(DN_ALPHA * a + b, g2_ref[...], b2_ref[...]) for a, b in zip(x1, attn)]

    r_hi, r_lo = _split2(rt_ref[...])
    r_both = jnp.concatenate([r_hi, r_lo], axis=0)
    eye_r = lax.broadcasted_iota(jnp.int32, (ROUTE_ROWS, ROUTE_LANES), 0)
    eye_c = lax.broadcasted_iota(jnp.int32, (ROUTE_ROWS, ROUTE_LANES), 1)
    eye = jnp.where(eye_r == eye_c, 1.0, 0.0).astype(BF16)
    logits = []
    for rows, t in zip(tiles, x2):
        x2_ref[rows] = t
        x_hi, x_lo = _split2(t)
        by_hi = _dot_nt(r_both, x_hi)
        logits.append(by_hi[N_EXPERTS:] + _dot_nt(r_hi, x_lo) + by_hi[:N_EXPERTS])
    routes = [_split3(_route(lt)) for lt in logits]
    for rows, (hi, mid, lo) in zip(tiles, routes):
        route_ref[rows] = _dot_tn(hi, eye) + _dot_tn(mid, eye) + _dot_tn(lo, eye)


def _mix(x, oa, ob, oc, gl, mk, mv, w, batch, seq):
    sub = min(MIX_TILE, seq)
    tm = min(2 * MIX_TILE, seq)
    nt = seq // tm
    row = lambda b, i: (b * nt + i, 0)
    vec = _const_spec((1, D_MODEL))
    n_mem = mk.shape[1]
    mem = pl.BlockSpec((1, n_mem, D_MODEL), lambda b, i: (b, 0, 0))
    sq = _layer_spec((D_MODEL, D_MODEL), w['layer'])
    return pl.pallas_call(
        functools.partial(_mix_kernel, sub=sub),
        grid=(batch, nt),
        in_specs=[pl.BlockSpec((tm, D_MODEL), row)] + [pl.BlockSpec((tm, A_WIDTH), row)] * 3
        + [pl.BlockSpec((tm, N_BRANCH * D_MODEL), row), mem, mem,
           _layer_spec((N_BRANCH, A_WIDTH, D_MODEL), w['layer']), sq, sq, sq, vec, vec, vec, vec,
           _const_spec((N_EXPERTS, D_MODEL))],
        out_specs=[pl.BlockSpec((tm, D_MODEL), row), pl.BlockSpec((tm, ROUTE_LANES), row)],
        out_shape=[jax.ShapeDtypeStruct((batch * seq, D_MODEL), F32),
                   jax.ShapeDtypeStruct((batch * seq, ROUTE_LANES), F32)],
        compiler_params=_params("arbitrary", "arbitrary"),
        name="mix",
    )(x, oa, ob, oc, gl, mk, mv, w['w_branch'], w['w_out'], w['xa_wq'], w['xa_wo'],
      w['ln1_g'], w['ln1_b'], w['ln2_g'], w['ln2_b'], w['router_t'])


def _sc_gather_rows(tables, idx):
    n_out = idx.shape[0]
    per_worker = n_out // SC_WORKERS
    window = min(SC_GATHER_WINDOW, per_worker)
    steps = per_worker // window
    assert per_worker * SC_WORKERS == n_out and steps * window == per_worker and window % 8 == 0
    idx3 = idx.astype(jnp.int32).reshape(SC_WORKERS, steps, window)
    mesh = plsc.VectorSubcoreMesh(core_axis_name="core", subcore_axis_name="subcore")
    n_tab = len(tables)

    def body(*refs):
        tab_hbm = refs[:n_tab]
        idx_hbm = refs[n_tab]
        out_hbm = refs[n_tab + 1:2 * n_tab + 1]
        idx_v = refs[2 * n_tab + 1]
        rows_v = refs[2 * n_tab + 2:3 * n_tab + 2]
        sem = refs[3 * n_tab + 2]
        wid = lax.axis_index("subcore") * SC_CORES + lax.axis_index("core")
        pltpu.sync_copy(idx_hbm.at[wid], idx_v)

        @pl.loop(0, steps)
        def _(j):
            base = wid * per_worker + j * window
            for k in range(n_tab):
                pltpu.async_copy(tab_hbm[k].at[idx_v.at[j]], rows_v[k], sem).wait()
                pltpu.sync_copy(rows_v[k], out_hbm[k].at[pl.ds(base, window)])

    call = pl.kernel(
        body,
        out_type=[jax.ShapeDtypeStruct((n_out, t.shape[1]), t.dtype) for t in tables],
        mesh=mesh,
        scratch_types=[pltpu.VMEM((steps, window), jnp.int32)]
        + [pltpu.VMEM((window, t.shape[1]), t.dtype) for t in tables] + [pltpu.SemaphoreType.DMA],
        name="sc_gather",
    )
    return call(*tables, idx3)


def _dispatch_plan(seg, tm, n_seg):
    n = seg.shape[0]
    n_pad = n + n_seg * tm
    ids = jnp.arange(n_seg, dtype=jnp.int32)
    onehot = (seg[:, None] == ids[None, :]).astype(jnp.int32)
    counts = jnp.sum(onehot, axis=0)
    padded = ((counts + tm - 1) // tm) * tm
    start_p = jnp.cumsum(padded) - padded
    start_u = jnp.cumsum(counts) - counts
    order = jnp.argsort(seg, stable=True).astype(jnp.int32)
    rank_sorted = jnp.argsort(order).astype(jnp.int32)
    pos = rank_sorted + jnp.sum(onehot * (start_p - start_u)[None, :], axis=1)
    slot = jnp.arange(n_pad, dtype=jnp.int32)
    seg_slot = jnp.minimum(jnp.sum((slot[:, None] >= (start_p + padded)[None, :]).astype(jnp.int32), axis=1),
                           n_seg - 1)
    oh_slot = (seg_slot[:, None] == ids[None, :]).astype(jnp.int32)
    r_slot = slot - jnp.sum(oh_slot * start_p[None, :], axis=1)
    valid = r_slot < jnp.sum(oh_slot * counts[None, :], axis=1)
    s = jnp.where(valid, jnp.sum(oh_slot * start_u[None, :], axis=1) + r_slot, 0)
    src = jnp.where(valid, jnp.take(order, s), slot % n)
    return pos.astype(jnp.int32), src, seg_slot[::tm]


def _moe_kernel(ex_ref, x_ref, r_ref, *refs, n_exp, col0):
    del ex_ref
    w_refs, (g_ref, b_ref, o_ref) = refs[:3 * n_exp], refs[3 * n_exp:]
    x = x_ref[...]
    cd = w_refs[0].dtype
    xb = x.astype(cd)
    r = r_ref[...]
    y = None
    for e in range(n_exp):
        w1_ref, w3_ref, w2_ref = w_refs[3 * e:3 * e + 3]
        h1 = _dot(xb, w1_ref[...])
        h3 = _dot(xb, w3_ref[...])
        ye = r[:, col0 + e:col0 + e + 1] * _dot((h1 * _sigmoid(h1) * h3).astype(cd), w2_ref[...])
        y = ye if y is None else y + ye
    o_ref[...] = _layer_norm(DN_ALPHA * x + y, g_ref[...], b_ref[...])


def _moe_sorted(xs, rs, experts, w, tm, col0):
    n_pad = xs.shape[0]
    n_exp, nt = experts.shape
    layer = w['layer']
    row = lambda i, ex: (i, 0)
    vec = pl.BlockSpec((1, D_MODEL), lambda i, ex: (0, 0))
    mode = dict(pipeline_mode=pl.Buffered(1)) if w['w1'].dtype == F32 else {}
    w_specs, w_args = [], []
    for e in range(n_exp):
        pick = lambda i, ex, e=e: (layer, ex[e * nt + i], 0, 0)
        w_specs += [pl.BlockSpec((None, None, D_MODEL, D_EXPERT), pick, **mode),
                    pl.BlockSpec((None, None, D_MODEL, D_EXPERT), pick, **mode),
                    pl.BlockSpec((None, None, D_EXPERT, D_MODEL), pick, **mode)]
        w_args += [w['w1'], w['w3'], w['w2']]
    return pl.pallas_call(
        functools.partial(_moe_kernel, n_exp=n_exp, col0=col0),
        grid_spec=pltpu.PrefetchScalarGridSpec(
            num_scalar_prefetch=1,
            grid=(nt,),
            in_specs=[pl.BlockSpec((tm, D_MODEL), row), pl.BlockSpec((tm, ROUTE_LANES), row)] + w_specs + [vec, vec],
            out_specs=pl.BlockSpec((tm, D_MODEL), row),
        ),
        out_shape=jax.ShapeDtypeStruct((n_pad, D_MODEL), F32),
        compiler_params=_params("arbitrary"),
        name="moe",
    )(experts.reshape(n_exp * nt), xs, rs, *w_args, w['ln3_g'], w['ln3_b'])


def _moe(x2, route, w):
    n = x2.shape[0]
    n_pairs = len(EXPERT_PAIRS)
    by_pair = n >= N_SEGMENTS * MOE_TILE
    if by_pair:
        tm, n_seg, col0 = MOE_TILE, N_SEGMENTS, 0
        seg = route[:, ROUTE_PAIR_COL].astype(jnp.int32)
    else:
        tm, n_seg, col0 = min(MOE_TILE, n // N_GROUPS), N_GROUPS, ROUTE_PAIR_COL + 1
        seg = route[:, ROUTE_GROUP_COL].astype(jnp.int32)
    pos, src, seg_tile = _dispatch_plan(seg, tm, n_seg)
    if by_pair:
        p = seg_tile % n_pairs
        member = lambda side: functools.reduce(
            lambda x, y: x + y, [jnp.where(p == k, pr[side], 0) for k, pr in enumerate(EXPERT_PAIRS)])
        experts = GROUP_SIZE * (seg_tile // n_pairs) + jnp.stack([member(0), member(1)])
    else:
        experts = GROUP_SIZE * seg_tile[None, :] + jnp.arange(GROUP_SIZE, dtype=jnp.int32)[:, None]
    xs, rs = _sc_gather_rows([x2, route], src)
    ys = _moe_sorted(xs, rs, experts.astype(jnp.int32), w, tm, col0)
    return _sc_gather_rows([ys], pos)[0]


def _trunk_layer(x, w, mem_k, mem_v, cache_k, cache_v, s0, h0, conv0, batch, seq, prompt):
    qkv, hg, rg, gl = _inproj(x, w['w_in'], w['layer'])
    keep = min(A_WINDOW, seq)
    x_keep = x.reshape(batch, seq, D_MODEL)[:, seq - keep:].reshape(batch * keep, D_MODEL)
    k32, v32 = [t.reshape(batch, keep, A_HEADS, A_HEAD_DIM) for t in _kv_rows(x_keep, w['w_in'], w['layer'])]
    if prompt:
        oa = _attn_prompt(qkv, w['band_bias'], batch, seq)
        new_k, new_v = k32, v32
        tile = 256
    else:
        win = cache_k.shape[1]
        ck = cache_k.reshape(batch, win, A_WIDTH)
        cv = cache_v.reshape(batch, win, A_WIDTH)
        oa = _attn_sample(qkv, ck, cv, w['band_bias'][:, :seq, :win + 2 * CHUNK], batch, seq)
        new_k = jnp.concatenate([cache_k, k32], axis=1)[:, seq:]
        new_v = jnp.concatenate([cache_v, v32], axis=1)[:, seq:]
        tile = seq
    ob, s_bd = _hgrn(hg, w['lb'], w['hgrn_g'], _state_to_block_diag(s0), batch, seq, tile, w['w_in'].dtype)
    conv0_pad = jnp.concatenate([jnp.zeros((batch, 8 - (C_CONV - 1), C_WIDTH), F32), conv0.astype(F32)], axis=1)
    oc, h_new = _rglru(rg, conv0_pad, h0.astype(F32).reshape(batch, 1, C_WIDTH), w, batch, seq, tile, prompt)
    conv_new = jnp.concatenate([conv0.astype(F32), rg.reshape(batch, seq, 2 * C_WIDTH)[:, :, :C_WIDTH]],
                               axis=1)[:, seq:]
    x2, route = _mix(x, oa, ob, oc, gl, mem_k, mem_v, w, batch, seq)
    x3 = _moe(x2, route, w)
    return x3, (new_k, new_v, _block_diag_to_state(s_bd), h_new.reshape(batch, C_WIDTH), conv_new)


def kernel(x_prompt, x_sample, cache_attn_k, cache_attn_v, state_hgrn, state_rglru, state_conv, cache_mem_k, cache_mem_v, mem_prompt, w_in, attn_rel_bias, hgrn_lb_logits, hgrn_norm_g, rg_conv_w, rg_conv_b, rg_wa, rg_ba, rg_wx, rg_bx, rg_lambda, w_branch, w_out, ln1_g, ln1_b, xa_wq, xa_wk, xa_wv, xa_wo, ln2_g, ln2_b, moe_router, moe_w1, moe_w3, moe_w2, ln3_g, ln3_b):
    bp, tp, _ = x_prompt.shape
    bs, ts, _ = x_sample.shape
    n_mem = mem_prompt.shape[1]
    depth = w_in.shape[0]

    p = jax.nn.softmax(hgrn_lb_logits.astype(F32), axis=0)
    lb_all = jnp.cumsum(p, axis=0) - p[0:1]
    vec = lambda t: t.astype(F32).reshape(1, -1)

    xp = x_prompt.reshape(bp * tp, D_MODEL)
    xs = x_sample.reshape(bs * ts, D_MODEL)
    mem2d = mem_prompt.reshape(bp * n_mem, D_MODEL)
    s0p = jnp.zeros((bp, B_HEADS, B_KEY_DIM, B_VAL_DIM), F32)
    h0p = jnp.zeros((bp, C_WIDTH), F32)
    conv0p = jnp.zeros((bp, C_CONV - 1, C_WIDTH), F32)
    router_t = moe_router.astype(F32).T

    stacked = {'w_in': w_in, 'w_branch': w_branch, 'w_out': w_out, 'xa_wq': xa_wq, 'xa_wo': xa_wo,
               'w1': moe_w1, 'w3': moe_w3, 'w2': moe_w2}
    stacked_fast = {k: v.astype(BF16) for k, v in stacked.items()}
    stacked_precise = {k: v.astype(F32) for k, v in stacked.items()}
    wk_b, wv_b = xa_wk.astype(BF16), xa_wv.astype(BF16)

    outs_p, outs_s, mem_ks, mem_vs = [], [], [], []
    for l in range(depth):
        wf = {'wa_bd': _block_diag_weight(rg_wa[l]), 'wx_bd': _block_diag_weight(rg_wx[l])}
        shared = {
            'layer': l, 'band_bias': _band_bias(attn_rel_bias[l]),
            'lb': vec(lb_all[l]), 'hgrn_g': vec(jnp.tile(hgrn_norm_g[l], B_HEADS)),
            'conv_w': rg_conv_w[l].astype(F32), 'conv_b': vec(rg_conv_b[l]),
            'ba': vec(rg_ba[l]), 'bx': vec(rg_bx[l]), 'lam': vec(rg_lambda[l]),
            'ln1_g': vec(ln1_g[l]), 'ln1_b': vec(ln1_b[l]), 'ln2_g': vec(ln2_g[l]), 'ln2_b': vec(ln2_b[l]),
            'ln3_g': vec(ln3_g[l]), 'ln3_b': vec(ln3_b[l]), 'router_t': router_t,
        }
        w_fast = dict(shared, **stacked_fast, **{k: v.astype(BF16) for k, v in wf.items()})
        w_precise = dict(shared, **stacked_precise, **{k: v.astype(F32) for k, v in wf.items()})
        mk_p = _matmul(mem2d, wk_b, l)
        mv_p = _matmul(mem2d, wv_b, l)
        mem_ks.append(mk_p.reshape(bp, n_mem, X_HEADS, X_HEAD_DIM))
        mem_vs.append(mv_p.reshape(bp, n_mem, X_HEADS, X_HEAD_DIM))
        xp, st_p = _trunk_layer(xp, w_fast, mk_p.reshape(bp, n_mem, D_MODEL).astype(BF16),
                                mv_p.reshape(bp, n_mem, D_MODEL).astype(BF16),
                                None, None, s0p, h0p, conv0p, bp, tp, True)
        xs, st_s = _trunk_layer(xs, w_precise, cache_mem_k[l].reshape(bs, n_mem, D_MODEL).astype(F32),
                                cache_mem_v[l].reshape(bs, n_mem, D_MODEL).astype(F32),
                                cache_attn_k[l], cache_attn_v[l], state_hgrn[l], state_rglru[l], state_conv[l],
                                bs, ts, False)
        outs_p.append(st_p)
        outs_s.append(st_s)

    stack = lambda items, j: jnp.stack([it[j] for it in items])
    return (xp.reshape(bp, tp, D_MODEL), xs.reshape(bs, ts, D_MODEL),
            stack(outs_p, 0), stack(outs_p, 1), stack(outs_p, 2), stack(outs_p, 3), stack(outs_p, 4),
            jnp.stack(mem_ks), jnp.stack(mem_vs),
            stack(outs_s, 0), stack(outs_s, 1), stack(outs_s, 2), stack(outs_s, 3), stack(outs_s, 4))
```

```python
import functools

import numpy as np
import jax
import jax.numpy as jnp
from jax import lax
from jax.experimental import pallas as pl
from jax.experimental.pallas import tpu as pltpu
from jax.experimental.pallas import tpu_sc as plsc

F32 = jnp.float32
BF16 = jnp.bfloat16

D_MODEL = 1024
CHUNK = 64
A_HEADS = 8
A_HEAD_DIM = 64
A_WIDTH = A_HEADS * A_HEAD_DIM
A_PAST_CHUNKS = 8
A_WINDOW = A_PAST_CHUNKS * CHUNK
A_MAX_REL = 256
B_HEADS = 8
B_KEY_DIM = 64
B_VAL_DIM = 64
B_WIDTH = B_HEADS * B_KEY_DIM
C_WIDTH = 512
C_CONV = 4
C_GATE_C = 8.0
N_BRANCH = 3
IN_COLS = 3 * A_WIDTH + 4 * B_WIDTH + 2 * C_WIDTH + N_BRANCH * D_MODEL
X_HEADS = 4
X_HEAD_DIM = D_MODEL // X_HEADS
N_EXPERTS = 16
N_GROUPS = 4
GROUP_SIZE = N_EXPERTS // N_GROUPS
D_EXPERT = D_MODEL // 2
DEPTH = 4
DN_ALPHA = (2 * DEPTH) ** 0.25
LN_EPS = 1e-5
RMS_EPS = 1e-6
NEG_INF = -1e30

VMEM_LIMIT_BYTES = 56 * 1024 * 1024

INPROJ_TILE = 512
MIX_TILE = 256
ATTN_Q_TILE = 256
ATTN_K_TILE = 256
HG_GROUP = 256
RG_SCAN_GROUP = 8
HG_SEQS_PER_STEP = 2
HG_HALF = CHUNK // 2
HG_QUARTER = CHUNK // 4
EXPERT_PAIRS = tuple((lo, hi) for lo in range(GROUP_SIZE) for hi in range(lo + 1, GROUP_SIZE))
N_SEGMENTS = N_GROUPS * len(EXPERT_PAIRS)
ROUTE_ROWS = 16
ROUTE_PAIR_COL = 2
ROUTE_GROUP_COL = 7
ROUTE_LANES = 128
MOE_TILE = 512
SC_CORES = 2
SC_SUBCORES = 16
SC_WORKERS = SC_CORES * SC_SUBCORES
SC_GATHER_WINDOW = 32


def _split2(x):
    hi = x.astype(BF16)
    lo = (x - hi.astype(F32)).astype(BF16)
    return hi, lo


def _contract(a, b, dims):
    dg = lambda u, v: lax.dot_general(u, v, (dims, ((), ())), preferred_element_type=F32)
    if a.dtype == F32:
        a_hi, a_lo = _split2(a)
        b_hi, b_lo = _split2(b)
        return dg(a_lo, b_hi) + dg(a_hi, b_lo) + dg(a_hi, b_hi)
    return dg(a, b)


def _dot(a, b):
    return _contract(a, b, ((1,), (0,)))


def _dot_nt(a, b):
    return _contract(a, b, ((1,), (1,)))


def _dot_tn(a, b):
    return _contract(a, b, ((0,), (0,)))


def _split3(x):
    hi = x.astype(BF16)
    r1 = x - hi.astype(F32)
    mid = r1.astype(BF16)
    lo = (r1 - mid.astype(F32)).astype(BF16)
    return hi, mid, lo


def _sigmoid(x):
    return 0.5 * jnp.tanh(0.5 * x) + 0.5


def _layer_norm(x, g, b):
    mu = jnp.mean(x, axis=-1, keepdims=True)
    xc = x - mu
    var = jnp.mean(xc * xc, axis=-1, keepdims=True)
    return xc * lax.rsqrt(var + LN_EPS) * g + b


def _params(*semantics):
    return pltpu.CompilerParams(dimension_semantics=semantics, vmem_limit_bytes=VMEM_LIMIT_BYTES)


def _const_spec(shape):
    nd = len(shape)
    return pl.BlockSpec(shape, lambda *_: (0,) * nd, pipeline_mode=pl.Buffered(1))


def _layer_spec(shape, layer):
    nd = len(shape)
    return pl.BlockSpec((None,) + tuple(shape), lambda *_: (layer,) + (0,) * nd, pipeline_mode=pl.Buffered(1))


def _inproj_kernel(x_ref, w_ref, qkv_ref, hg_ref, rg_ref, gl_ref):
    xb = x_ref[...].astype(w_ref.dtype)
    cw = 512

    def mm(c0):
        return _dot(xb, w_ref[:, c0:c0 + cw])

    for j in range(3):
        qkv_ref[:, cw * j:cw * (j + 1)] = mm(cw * j).astype(qkv_ref.dtype)
    base = 3 * A_WIDTH
    for j in range(4):
        hg_ref[:, cw * j:cw * (j + 1)] = mm(base + cw * j)
    base += 4 * B_WIDTH
    for j in range(2):
        rg_ref[:, cw * j:cw * (j + 1)] = mm(base + cw * j)
    base += 2 * C_WIDTH
    for j in range(N_BRANCH * D_MODEL // cw):
        gl_ref[:, cw * j:cw * (j + 1)] = mm(base + cw * j).astype(gl_ref.dtype)


def _inproj(x, w_in, layer):
    n = x.shape[0]
    cd = w_in.dtype
    tm = INPROJ_TILE if cd == BF16 else INPROJ_TILE // 2
    assert n % tm == 0
    row = lambda i: (i, 0)
    return pl.pallas_call(
        _inproj_kernel,
        grid=(n // tm,),
        in_specs=[pl.BlockSpec((tm, D_MODEL), row), _layer_spec((D_MODEL, IN_COLS), layer)],
        out_specs=[pl.BlockSpec((tm, 3 * A_WIDTH), row),
                   pl.BlockSpec((tm, 4 * B_WIDTH), row), pl.BlockSpec((tm, 2 * C_WIDTH), row),
                   pl.BlockSpec((tm, N_BRANCH * D_MODEL), row)],
        out_shape=[jax.ShapeDtypeStruct((n, 3 * A_WIDTH), cd),
                   jax.ShapeDtypeStruct((n, 4 * B_WIDTH), F32), jax.ShapeDtypeStruct((n, 2 * C_WIDTH), F32),
                   jax.ShapeDtypeStruct((n, N_BRANCH * D_MODEL), cd)],
        compiler_params=_params("arbitrary"),
        name="inproj",
    )(x, w_in)


def _matmul_kernel(x_ref, w_ref, o_ref):
    o_ref[...] = _dot(x_ref[...].astype(w_ref.dtype), w_ref[...])


def _matmul(x, w, layer):
    n, k = x.shape
    m = w.shape[2]
    tm = 256
    return pl.pallas_call(
        _matmul_kernel,
        grid=(n // tm,),
        in_specs=[pl.BlockSpec((tm, k), lambda i: (i, 0)), _layer_spec((k, m), layer)],
        out_specs=pl.BlockSpec((tm, m), lambda i: (i, 0)),
        out_shape=jax.ShapeDtypeStruct((n, m), F32),
        compiler_params=_params("arbitrary"),
        name="matmul",
    )(x, w)


def _kv_kernel(x_ref, wk_ref, wv_ref, k_ref, v_ref):
    xb = x_ref[...].astype(wk_ref.dtype)
    k_ref[...] = _dot(xb, wk_ref[...])
    v_ref[...] = _dot(xb, wv_ref[...])


def _kv_rows(x, w_in, layer):
    n = x.shape[0]
    tm = 256
    col = lambda j: pl.BlockSpec((None, D_MODEL, A_WIDTH), lambda i: (layer, 0, j), pipeline_mode=pl.Buffered(1))
    out = pl.BlockSpec((tm, A_WIDTH), lambda i: (i, 0))
    return pl.pallas_call(
        _kv_kernel,
        grid=(n // tm,),
        in_specs=[pl.BlockSpec((tm, D_MODEL), lambda i: (i, 0)), col(1), col(2)],
        out_specs=[out, out],
        out_shape=[jax.ShapeDtypeStruct((n, A_WIDTH), F32)] * 2,
        compiler_params=_params("arbitrary"),
        name="kv_rows",
    )(x, w_in, w_in)


def _attn_core(q, k, v, bias_ref, valid):
    rows = q.shape[0]
    lane = lax.broadcasted_iota(jnp.int32, (rows, 2 * A_HEAD_DIM), 1)
    first = lane < A_HEAD_DIM
    q = q * jnp.asarray(A_HEAD_DIM ** -0.5, q.dtype)

    def scores(head):
        sl = slice(2 * A_HEAD_DIM * (head // 2), 2 * A_HEAD_DIM * (head // 2 + 1))
        sel = first if head % 2 == 0 else jnp.logical_not(first)
        qm = jnp.where(sel, q[:, sl], jnp.zeros_like(q[:, sl]))
        s = _dot_nt(qm, k[:, sl]) + bias_ref[head]
        return s if valid is None else jnp.where(valid, s, NEG_INF)

    outs = []
    pair = None
    s_next = scores(0)
    for head in range(A_HEADS):
        s = s_next
        if head + 1 < A_HEADS:
            s_next = scores(head + 1)
        m = jnp.max(s, axis=-1, keepdims=True)
        e = jnp.exp(s - m)
        l = jnp.sum(e, axis=-1, keepdims=True)
        sl = slice(2 * A_HEAD_DIM * (head // 2), 2 * A_HEAD_DIM * (head // 2 + 1))
        o = _dot(e.astype(v.dtype), v[:, sl]) * (1.0 / l)
        if head % 2 == 0:
            pair = o
        else:
            outs.append(jnp.where(first, pair, o))
    return jnp.concatenate(outs, axis=-1)


def _attn_prompt_kernel(q_ref, k0_ref, k1_ref, k2_ref, v0_ref, v1_ref, v2_ref, bias_ref, o_ref):
    i = pl.program_id(1)

    def run(masked):
        k = jnp.concatenate([k0_ref[...], k1_ref[...], k2_ref[...]], axis=0)
        v = jnp.concatenate([v0_ref[...], v1_ref[...], v2_ref[...]], axis=0)
        valid = None
        if masked:
            col = lax.broadcasted_iota(jnp.int32, (ATTN_Q_TILE, 3 * ATTN_K_TILE), 1)
            valid = col >= (2 - i) * ATTN_K_TILE
        o_ref[...] = _attn_core(q_ref[...], k, v, bias_ref, valid).astype(o_ref.dtype)

    pl.when(i < 2)(lambda: run(True))
    pl.when(i >= 2)(lambda: run(False))


def _attn_prompt(qkv, bias, batch, seq):
    nt = seq // ATTN_Q_TILE
    blk = (ATTN_Q_TILE, A_WIDTH)

    def kv_spec(j, col):
        return pl.BlockSpec(blk, lambda b, i: (b * nt + jnp.maximum(i - 2 + j, 0), col))

    return pl.pallas_call(
        _attn_prompt_kernel,
        grid=(batch, nt),
        in_specs=[pl.BlockSpec(blk, lambda b, i: (b * nt + i, 0))]
        + [kv_spec(j, 1) for j in range(3)] + [kv_spec(j, 2) for j in range(3)]
        + [_const_spec(bias.shape)],
        out_specs=pl.BlockSpec(blk, lambda b, i: (b * nt + i, 0)),
        out_shape=jax.ShapeDtypeStruct((batch * seq, A_WIDTH), qkv.dtype),
        compiler_params=_params("arbitrary", "arbitrary"),
        name="attn_prompt",
    )(qkv, qkv, qkv, qkv, qkv, qkv, qkv, bias)


def _attn_sample_kernel(q_ref, kn_ref, vn_ref, ck_ref, cv_ref, bias_ref, o_ref):
    cd = q_ref.dtype
    pad = jnp.zeros((CHUNK, A_WIDTH), cd)
    k = jnp.concatenate([ck_ref[0].astype(cd), kn_ref[...], pad], axis=0)
    v = jnp.concatenate([cv_ref[0].astype(cd), vn_ref[...], pad], axis=0)
    o_ref[...] = _attn_core(q_ref[...], k, v, bias_ref, None).astype(cd)


def _attn_sample(qkv, cache_k, cache_v, bias, batch, seq):
    win = cache_k.shape[1]
    blk = (seq, A_WIDTH)
    return pl.pallas_call(
        _attn_sample_kernel,
        grid=(batch,),
        in_specs=[pl.BlockSpec(blk, lambda b: (b, 0)), pl.BlockSpec(blk, lambda b: (b, 1)),
                  pl.BlockSpec(blk, lambda b: (b, 2)),
                  pl.BlockSpec((1, win, A_WIDTH), lambda b: (b, 0, 0)),
                  pl.BlockSpec((1, win, A_WIDTH), lambda b: (b, 0, 0)),
                  _const_spec(bias.shape)],
        out_specs=pl.BlockSpec(blk, lambda b: (b, 0)),
        out_shape=jax.ShapeDtypeStruct((batch * seq, A_WIDTH), qkv.dtype),
        compiler_params=_params("arbitrary"),
        name="attn_sample",
    )(qkv, qkv, qkv, cache_k, cache_v, bias)


def _band_bias(table):
    heads = table.shape[0]
    n_keys = 3 * ATTN_K_TILE
    span = ATTN_Q_TILE + n_keys
    r = np.arange(ATTN_Q_TILE)[:, None]
    j = np.arange(n_keys)[None, :]
    band = (j // CHUNK >= r // CHUNK) & (j // CHUNK <= r // CHUNK + A_PAST_CHUNKS)
    t = table.astype(F32)
    u = jnp.concatenate([t, jnp.broadcast_to(t[:, -1:], (heads, span - t.shape[1]))], axis=1)
    w = u[:, ::-1]
    tiled = jnp.broadcast_to(w[:, None, :], (heads, ATTN_Q_TILE, span)).reshape(heads, ATTN_Q_TILE * span)
    view = tiled[:, :ATTN_Q_TILE * (span - 1)].reshape(heads, ATTN_Q_TILE, span - 1)
    bias = view[:, :, ATTN_Q_TILE - 1:ATTN_Q_TILE - 1 + n_keys]
    return jnp.where(band[None], bias, NEG_INF)


def _head_block_diag(x, head_masks):
    zero = jnp.zeros_like(x)
    return jnp.concatenate([jnp.where(m, x, zero) for m in head_masks], axis=0)


def _hgrn_kernel(hg_ref, lb_ref, ng_ref, s0_ref, o_ref, sfin_ref, st_ref, ot_ref, *, tile):
    i = pl.program_id(1)

    @pl.when(i == 0)
    def _():
        st_ref[...] = s0_ref[...]

    for bb in range(hg_ref.shape[0]):
        _hgrn_rows(hg_ref.at[bb], lb_ref, ng_ref, o_ref.at[bb], st_ref.at[bb], ot_ref.at[bb], tile)

    @pl.when(i == pl.num_programs(1) - 1)
    def _():
        sfin_ref[...] = st_ref[...]


def _hgrn_rows(hg_ref, lb_ref, ng_ref, o_ref, st_ref, ot_ref, tile):
    n_groups = B_WIDTH // HG_GROUP
    cd = o_ref.dtype
    q = hg_ref[:, 0:B_WIDTH]
    f_logit = hg_ref[:, B_WIDTH:2 * B_WIDTH]
    v_in = hg_ref[:, 2 * B_WIDTH:3 * B_WIDTH].astype(cd)
    lb = lb_ref[...]
    f = lb + (1.0 - lb) * (1.0 / (1.0 + jnp.exp(-f_logit)))
    log_f = jnp.log(f)
    kk = 1.0 - f

    r_t = lax.broadcasted_iota(jnp.int32, (tile, tile), 0)
    c_t = lax.broadcasted_iota(jnp.int32, (tile, tile), 1)
    tri = jnp.where((r_t // CHUNK == c_t // CHUNK) & (c_t <= r_t), 1.0, 0.0).astype(BF16)
    split = _split3 if cd == F32 else _split2
    g_all = functools.reduce(lambda x, y: x + y, [_dot(tri, part) for part in reversed(split(log_f))])

    row = lax.broadcasted_iota(jnp.int32, (CHUNK, B_WIDTH), 0)
    upper = row >= HG_HALF
    lane_g = lax.broadcasted_iota(jnp.int32, (CHUNK, HG_GROUP), 1)
    row_g = lax.broadcasted_iota(jnp.int32, (CHUNK, HG_GROUP), 0)
    head_masks = [lane_g // B_KEY_DIM == h for h in range(HG_GROUP // B_KEY_DIM)]
    causal = (lane_g % CHUNK) <= row_g
    cross = (row_g >= HG_HALF) & ((lane_g % CHUNK) < HG_HALF)
    r_bd = lax.broadcasted_iota(jnp.int32, (HG_GROUP, HG_GROUP), 0)
    c_bd = lax.broadcasted_iota(jnp.int32, (HG_GROUP, HG_GROUP), 1)
    diag_blocks = (r_bd // B_VAL_DIM) == (c_bd // B_KEY_DIM)

    pending = []
    for c in range(tile // CHUNK):
        rs = slice(CHUNK * c, CHUNK * (c + 1))
        g = g_all[rs]
        qc = q[rs]
        kc = kk[rs]
        vc = v_in[rs]
        g_q1 = g[HG_QUARTER - 1:HG_QUARTER]
        g_mid = g[HG_HALF - 1:HG_HALF]
        g_q3 = g[HG_HALF + HG_QUARTER - 1:HG_HALF + HG_QUARTER]
        g_last = g[CHUNK - 1:CHUNK]
        d_diag = g - jnp.where(upper, g_q3, g_q1)
        d_off = jnp.where(upper, g - g_mid, g_mid - g)
        q_in = (qc * jnp.exp(g)).astype(cd)
        k_st = (kc * jnp.exp(g_last - g)).astype(cd)
        q_diag = qc * jnp.exp(d_diag)
        k_diag = kc * jnp.exp(-d_diag)
        e_off = jnp.exp(d_off)
        zero = jnp.zeros_like(qc)
        q_d = q_diag.astype(cd)
        k_d = k_diag.astype(cd)
        q_x = jnp.where(upper, qc * e_off, zero).astype(cd)
        k_x = jnp.where(upper, zero, kc * e_off).astype(cd)
        decay = jnp.exp(g_last)
        for gi in range(n_groups):
            cs = slice(HG_GROUP * gi, HG_GROUP * (gi + 1))
            att_d = _dot_nt(q_d[:, cs], _head_block_diag(k_d[:, cs], head_masks))
            att_x = _dot_nt(q_x[:, cs], _head_block_diag(k_x[:, cs], head_masks))
            att = jnp.where(cross, att_x, att_d)
            att = jnp.where(causal, att, 0.0).astype(cd)
            v_bd = _head_block_diag(vc[:, cs], head_masks)
            upd = jnp.where(diag_blocks, _dot_tn(vc[:, cs], k_st[:, cs]), 0.0)
            pending.append((rs, cs, gi, q_in[:, cs], _dot(att, v_bd), decay[:, cs], upd))

    states = [st_ref[gi] for gi in range(n_groups)]
    before = []
    for rs, cs, gi, q_in_g, o_intra, decay_g, upd in pending:
        before.append(states[gi].astype(cd))
        states[gi] = states[gi] * decay_g + upd
    for gi in range(n_groups):
        st_ref[gi] = states[gi]
    for (rs, cs, gi, q_in_g, o_intra, decay_g, upd), st_b in zip(pending, before):
        ot_ref[rs, cs] = _dot_nt(q_in_g, st_b) + o_intra

    o = ot_ref[...]
    lane_i = lax.broadcasted_iota(jnp.int32, (B_WIDTH, B_WIDTH), 0)
    lane_j = lax.broadcasted_iota(jnp.int32, (B_WIDTH, B_WIDTH), 1)
    head_ones = jnp.where(lane_i // B_VAL_DIM == lane_j // B_VAL_DIM, 1.0, 0.0).astype(BF16)
    sq_parts = split(o * o)
    sums = _dot(jnp.concatenate(sq_parts, axis=0), head_ones)
    ms = functools.reduce(lambda x, y: x + y, [sums[tile * j:tile * (j + 1)] for j in reversed(range(len(sq_parts)))])
    ms = ms * (1.0 / B_VAL_DIM)
    gate = hg_ref[:, 3 * B_WIDTH:4 * B_WIDTH]
    out = o * lax.rsqrt(ms + RMS_EPS) * ng_ref[...] * (gate * _sigmoid(gate))
    o_ref[...] = out.astype(cd)


def _hgrn(hg, lb, norm_g, s0_bd, batch, seq, tile, cd):
    nt = seq // tile
    n_groups = B_WIDTH // HG_GROUP
    par = HG_SEQS_PER_STEP
    assert batch % par == 0
    st_blk = (par, n_groups, HG_GROUP, HG_GROUP)
    ob, s_fin = pl.pallas_call(
        functools.partial(_hgrn_kernel, tile=tile),
        grid=(batch // par, nt),
        in_specs=[pl.BlockSpec((par, tile, 4 * B_WIDTH), lambda b, i: (b, i, 0)),
                  _const_spec((1, B_WIDTH)), _const_spec((1, B_WIDTH)),
                  pl.BlockSpec(st_blk, lambda b, i: (b, 0, 0, 0))],
        out_specs=[pl.BlockSpec((par, tile, B_WIDTH), lambda b, i: (b, i, 0)),
                   pl.BlockSpec(st_blk, lambda b, i: (b, 0, 0, 0))],
        out_shape=[jax.ShapeDtypeStruct((batch, seq, B_WIDTH), cd),
                   jax.ShapeDtypeStruct((batch,) + st_blk[1:], F32)],
        scratch_shapes=[pltpu.VMEM(st_blk, F32), pltpu.VMEM((par, tile, B_WIDTH), F32)],
        compiler_params=_params("arbitrary", "arbitrary"),
        name="hgrn",
    )(hg.reshape(batch, seq, 4 * B_WIDTH), lb, norm_g, s0_bd)
    return ob.reshape(batch * seq, B_WIDTH), s_fin


def _state_to_block_diag(s):
    b = s.shape[0]
    hpg = HG_GROUP // B_KEY_DIM
    st = s.astype(F32).reshape(b, B_HEADS // hpg, hpg, B_KEY_DIM, B_VAL_DIM).transpose(0, 1, 2, 4, 3)
    bd = jnp.einsum('bghvc,hk->bghvkc', st, jnp.eye(hpg, dtype=F32))
    return bd.reshape(b, B_HEADS // hpg, HG_GROUP, HG_GROUP)


def _block_diag_to_state(bd):
    b = bd.shape[0]
    hpg = HG_GROUP // B_KEY_DIM
    x = bd.reshape(b, B_HEADS // hpg, hpg, B_VAL_DIM, hpg, B_KEY_DIM)
    st = jnp.einsum('bghvkc,hk->bghvc', x, jnp.eye(hpg, dtype=F32))
    return st.transpose(0, 1, 2, 4, 3).reshape(b, B_HEADS, B_KEY_DIM, B_VAL_DIM)


def _rglru_kernel(rg_ref, conv0_ref, h0_ref, cw_ref, cb_ref, wa_ref, ba_ref, wx_ref, bx_ref, lam_ref,
                  o_ref, hlast_ref, xbuf_ref, hc_ref, *, tile, at_start):
    i = pl.program_id(1)
    pad = 8

    @pl.when(i == 0)
    def _():
        xbuf_ref[0:pad] = conv0_ref[0]
        hc_ref[...] = h0_ref[0]

    xr = rg_ref[:, 0:C_WIDTH]
    gate = rg_ref[:, C_WIDTH:2 * C_WIDTH]
    xbuf_ref[pad:pad + tile] = xr
    xc = cb_ref[...] + cw_ref[C_CONV - 1:C_CONV] * xr
    for j in range(1, C_CONV):
        xc = xc + cw_ref[C_CONV - 1 - j:C_CONV - j] * xbuf_ref[pad - j:pad - j + tile]
    xbuf_ref[0:pad] = xbuf_ref[tile:tile + pad]

    xcb = xc.astype(wa_ref.dtype)
    r = _sigmoid(_dot(xcb, wa_ref[...]) + ba_ref[...])
    ig = _sigmoid(_dot(xcb, wx_ref[...]) + bx_ref[...])
    neg_lam = -lam_ref[...]
    softplus = jnp.maximum(neg_lam, 0.0) + jnp.log(1.0 + jnp.exp(-jnp.abs(neg_lam)))
    a = jnp.exp(r * (-C_GATE_C * softplus))
    mult = jnp.sqrt(1.0 - a * a)
    row = lax.broadcasted_iota(jnp.int32, (tile, C_WIDTH), 0)
    if at_start:
        mult = jnp.where((row == 0) & (i == 0), 1.0, mult)
    b = mult * ig * xc

    in_group = row % RG_SCAN_GROUP
    d = 1
    while d < RG_SCAN_GROUP:
        a_sh = pltpu.roll(a, d, 0)
        b_sh = pltpu.roll(b, d, 0)
        keep = in_group >= d
        b = jnp.where(keep, a * b_sh + b, b)
        a = jnp.where(keep, a * a_sh, a)
        d *= 2
    carry = hc_ref[...]
    groups = []
    for j in range(tile // RG_SCAN_GROUP):
        rs = slice(RG_SCAN_GROUP * j, RG_SCAN_GROUP * (j + 1))
        hj = a[rs] * carry + b[rs]
        carry = hj[RG_SCAN_GROUP - 1:RG_SCAN_GROUP]
        groups.append(hj)
    h = jnp.concatenate(groups, axis=0)
    hc_ref[...] = h[tile - 1:tile]
    hlast_ref[0] = h[tile - 1:tile]
    gelu = 0.5 * gate * (1.0 + jnp.tanh(np.sqrt(2.0 / np.pi).astype(np.float32) * (gate + 0.044715 * gate * gate * gate)))
    o_ref[...] = (h * gelu).astype(o_ref.dtype)


def _rglru(rg, conv0_pad, h0, w, batch, seq, tile, at_start):
    nt = seq // tile
    vec = _const_spec((1, C_WIDTH))
    return pl.pallas_call(
        functools.partial(_rglru_kernel, tile=tile, at_start=at_start),
        grid=(batch, nt),
        in_specs=[pl.BlockSpec((tile, 2 * C_WIDTH), lambda b, i: (b * nt + i, 0)),
                  pl.BlockSpec((1, 8, C_WIDTH), lambda b, i: (b, 0, 0)),
                  pl.BlockSpec((1, 1, C_WIDTH), lambda b, i: (b, 0, 0)),
                  _const_spec((C_CONV, C_WIDTH)), vec,
                  _const_spec((C_WIDTH, C_WIDTH)), vec, _const_spec((C_WIDTH, C_WIDTH)), vec, vec],
        out_specs=[pl.BlockSpec((tile, C_WIDTH), lambda b, i: (b * nt + i, 0)),
                   pl.BlockSpec((1, 1, C_WIDTH), lambda b, i: (b, 0, 0))],
        out_shape=[jax.ShapeDtypeStruct((batch * seq, C_WIDTH), w['wa_bd'].dtype),
                   jax.ShapeDtypeStruct((batch, 1, C_WIDTH), F32)],
        scratch_shapes=[pltpu.VMEM((tile + 8, C_WIDTH), F32), pltpu.VMEM((1, C_WIDTH), F32)],
        compiler_params=_params("arbitrary", "arbitrary"),
        name="rglru",
    )(rg, conv0_pad, h0, w['conv_w'], w['conv_b'], w['wa_bd'], w['ba'], w['wx_bd'], w['bx'], w['lam'])


def _block_diag_weight(w):
    n, d, e = w.shape
    return jnp.einsum('nde,nm->ndme', w, jnp.eye(n, dtype=w.dtype)).reshape(n * d, n * e)


def _route(logits_t):
    m = jnp.max(logits_t, axis=0, keepdims=True)
    e = jnp.exp(logits_t - m)
    p = e / jnp.sum(e, axis=0, keepdims=True)
    rows = [p[j:j + 1] for j in range(N_EXPERTS)]
    scores = []
    for g in range(N_GROUPS):
        mem = rows[GROUP_SIZE * g:GROUP_SIZE * (g + 1)]
        best = None
        for a in range(GROUP_SIZE):
            for b in range(a + 1, GROUP_SIZE):
                pair = mem[a] + mem[b]
                best = pair if best is None else jnp.maximum(best, pair)
        scores.append(best)
    smax = functools.reduce(jnp.maximum, scores)
    taken = jnp.zeros_like(smax)
    sel = []
    for g in range(N_GROUPS):
        hit = jnp.where(scores[g] == smax, 1.0, 0.0) * (1.0 - taken)
        taken = taken + hit
        sel.append(hit)
    picked = []
    for j in range(N_EXPERTS):
        g = j // GROUP_SIZE
        rank = jnp.zeros_like(smax)
        for o in range(GROUP_SIZE * g, GROUP_SIZE * (g + 1)):
            if o == j:
                continue
            ahead = (rows[o] >= rows[j]) if o < j else (rows[o] > rows[j])
            rank = rank + jnp.where(ahead, 1.0, 0.0)
        picked.append(sel[g] * jnp.where(rank < float(2), 1.0, 0.0))
    denom = functools.reduce(lambda x, y: x + y, [picked[j] * rows[j] for j in range(N_EXPERTS)])
    comb = [picked[j] * rows[j] / denom for j in range(N_EXPERTS)]
    add = lambda items: functools.reduce(lambda x, y: x + y, items)
    cw = [add([sel[g] * comb[GROUP_SIZE * g + m] for g in range(N_GROUPS)]) for m in range(GROUP_SIZE)]
    on = [add([sel[g] * picked[GROUP_SIZE * g + m] for g in range(N_GROUPS)]) for m in range(GROUP_SIZE)]
    gid = add([float(g) * sel[g] for g in range(1, N_GROUPS)])
    pair_on = [on[lo] * on[hi] for lo, hi in EXPERT_PAIRS]
    w_lo = add([p * cw[lo] for p, (lo, hi) in zip(pair_on, EXPERT_PAIRS)])
    w_hi = add([p * cw[hi] for p, (lo, hi) in zip(pair_on, EXPERT_PAIRS)])
    pair = add([float(k) * p for k, p in enumerate(pair_on) if k > 0])
    segment = gid * float(len(EXPERT_PAIRS)) + pair
    pad = jnp.zeros((ROUTE_ROWS - ROUTE_GROUP_COL - 1,) + smax.shape[1:], F32)
    return jnp.concatenate([w_lo, w_hi, segment] + cw + [gid, pad], axis=0)


def _mix_kernel(x_ref, oa_ref, ob_ref, oc_ref, gl_ref, mk_ref, mv_ref, wb_ref, wout_ref, wq_ref, wo_ref,
                g1_ref, b1_ref, g2_ref, b2_ref, rt_ref, x2_ref, route_ref, *, sub):
    tiles = [slice(s0, s0 + sub) for s0 in range(0, x_ref.shape[0], sub)]
    cd = wb_ref.dtype

    mixed = []
    for rows in tiles:
        acc = None
        for b, o_ref in enumerate((oa_ref, ob_ref, oc_ref)):
            per_branch = _dot(o_ref[rows], wb_ref[b])
            gate = _sigmoid(gl_ref[rows, D_MODEL * b:D_MODEL * (b + 1)].astype(F32))
            acc = gate * per_branch if acc is None else acc + gate * per_branch
        mixed.append(acc.astype(cd))

    x1 = [_layer_norm(DN_ALPHA * x_ref[rows] + _dot(m, wout_ref[...]), g1_ref[...], b1_ref[...])
          for rows, m in zip(tiles, mixed)]
    q = [(_dot(t.astype(cd), wq_ref[...]) * (X_HEAD_DIM ** -0.5)).astype(cd) for t in x1]

    heads = [[] for _ in tiles]
    for h in range(X_HEADS):
        sl = slice(X_HEAD_DIM * h, X_HEAD_DIM * (h + 1))
        scores = [_dot_nt(qt[:, sl], mk_ref[0, :, sl]) for qt in q]
        for t, s in enumerate(scores):
            m = jnp.max(s, axis=-1, keepdims=True)
            e = jnp.exp(s - m)
            l = jnp.sum(e, axis=-1, keepdims=True)
            heads[t].append((_dot(e.astype(cd), mv_ref[0, :, sl]) * (1.0 / l)).astype(cd))

    attn = [_dot(jnp.concatenate(hs, axis=-1), wo_ref[...]) for hs in heads]
    x2 = [_layer_norm(DN_ALPHA * a + b, g2_ref[...], b2_ref[...]) for a, b in zip(x1, attn)]

    r_hi, r_lo = _split2(rt_ref[...])
    r_both = jnp.concatenate([r_hi, r_lo], axis=0)
    eye_r = lax.broadcasted_iota(jnp.int32, (ROUTE_ROWS, ROUTE_LANES), 0)
    eye_c = lax.broadcasted_iota(jnp.int32, (ROUTE_ROWS, ROUTE_LANES), 1)
    eye = jnp.where(eye_r == eye_c, 1.0, 0.0).astype(BF16)
    logits = []
    for rows, t in zip(tiles, x2):
        x2_ref[rows] = t
        x_hi, x_lo = _split2(t)
        by_hi = _dot_nt(r_both, x_hi)
        logits.append(by_hi[N_EXPERTS:] + _dot_nt(r_hi, x_lo) + by_hi[:N_EXPERTS])
    routes = [_split3(_route(lt)) for lt in logits]
    for rows, (hi, mid, lo) in zip(tiles, routes):
        route_ref[rows] = _dot_tn(hi, eye) + _dot_tn(mid, eye) + _dot_tn(lo, eye)


def _mix(x, oa, ob, oc, gl, mk, mv, w, batch, seq):
    sub = min(MIX_TILE, seq)
    tm = min(2 * MIX_TILE, seq)
    nt = seq // tm
    row = lambda b, i: (b * nt + i, 0)
    vec = _const_spec((1, D_MODEL))
    n_mem = mk.shape[1]
    mem = pl.BlockSpec((1, n_mem, D_MODEL), lambda b, i: (b, 0, 0))
    sq = _layer_spec((D_MODEL, D_MODEL), w['layer'])
    return pl.pallas_call(
        functools.partial(_mix_kernel, sub=sub),
        grid=(batch, nt),
        in_specs=[pl.BlockSpec((tm, D_MODEL), row)] + [pl.BlockSpec((tm, A_WIDTH), row)] * 3
        + [pl.BlockSpec((tm, N_BRANCH * D_MODEL), row), mem, mem,
           _layer_spec((N_BRANCH, A_WIDTH, D_MODEL), w['layer']), sq, sq, sq, vec, vec, vec, vec,
           _const_spec((N_EXPERTS, D_MODEL))],
        out_specs=[pl.BlockSpec((tm, D_MODEL), row), pl.BlockSpec((tm, ROUTE_LANES), row)],
        out_shape=[jax.ShapeDtypeStruct((batch * seq, D_MODEL), F32),
                   jax.ShapeDtypeStruct((batch * seq, ROUTE_LANES), F32)],
        compiler_params=_params("arbitrary", "arbitrary"),
        name="mix",
    )(x, oa, ob, oc, gl, mk, mv, w['w_branch'], w['w_out'], w['xa_wq'], w['xa_wo'],
      w['ln1_g'], w['ln1_b'], w['ln2_g'], w['ln2_b'], w['router_t'])


def _sc_gather_rows(tables, idx):
    n_out = idx.shape[0]
    per_worker = n_out // SC_WORKERS
    window = min(SC_GATHER_WINDOW, per_worker)
    steps = per_worker // window
    assert per_worker * SC_WORKERS == n_out and steps * window == per_worker and window % 8 == 0
    idx3 = idx.astype(jnp.int32).reshape(SC_WORKERS, steps, window)
    mesh = plsc.VectorSubcoreMesh(core_axis_name="core", subcore_axis_name="subcore")
    n_tab = len(tables)

    def body(*refs):
        tab_hbm = refs[:n_tab]
        idx_hbm = refs[n_tab]
        out_hbm = refs[n_tab + 1:2 * n_tab + 1]
        idx_v = refs[2 * n_tab + 1]
        rows_v = refs[2 * n_tab + 2:3 * n_tab + 2]
        sem = refs[3 * n_tab + 2]
        wid = lax.axis_index("subcore") * SC_CORES + lax.axis_index("core")
        pltpu.sync_copy(idx_hbm.at[wid], idx_v)

        @pl.loop(0, steps)
        def _(j):
            base = wid * per_worker + j * window
            for k in range(n_tab):
                pltpu.async_copy(tab_hbm[k].at[idx_v.at[j]], rows_v[k], sem).wait()
                pltpu.sync_copy(rows_v[k], out_hbm[k].at[pl.ds(base, window)])

    call = pl.kernel(
        body,
        out_type=[jax.ShapeDtypeStruct((n_out, t.shape[1]), t.dtype) for t in tables],
        mesh=mesh,
        scratch_types=[pltpu.VMEM((steps, window), jnp.int32)]
        + [pltpu.VMEM((window, t.shape[1]), t.dtype) for t in tables] + [pltpu.SemaphoreType.DMA],
        name="sc_gather",
    )
    return call(*tables, idx3)


def _dispatch_plan(seg, tm, n_seg):
    n = seg.shape[0]
    n_pad = n + n_seg * tm
    ids = jnp.arange(n_seg, dtype=jnp.int32)
    onehot = (seg[:, None] == ids[None, :]).astype(jnp.int32)
    counts = jnp.sum(onehot, axis=0)
    padded = ((counts + tm - 1) // tm) * tm
    start_p = jnp.cumsum(padded) - padded
    start_u = jnp.cumsum(counts) - counts
    order = jnp.argsort(seg, stable=True).astype(jnp.int32)
    rank_sorted = jnp.argsort(order).astype(jnp.int32)
    pos = rank_sorted + jnp.sum(onehot * (start_p - start_u)[None, :], axis=1)
    slot = jnp.arange(n_pad, dtype=jnp.int32)
    seg_slot = jnp.minimum(jnp.sum((slot[:, None] >= (start_p + padded)[None, :]).astype(jnp.int32), axis=1),
                           n_seg - 1)
    oh_slot = (seg_slot[:, None] == ids[None, :]).astype(jnp.int32)
    r_slot = slot - jnp.sum(oh_slot * start_p[None, :], axis=1)
    valid = r_slot < jnp.sum(oh_slot * counts[None, :], axis=1)
    s = jnp.where(valid, jnp.sum(oh_slot * start_u[None, :], axis=1) + r_slot, 0)
    src = jnp.where(valid, jnp.take(order, s), slot % n)
    return pos.astype(jnp.int32), src, seg_slot[::tm], jnp.sum(padded) // tm


def _moe_kernel(ex_ref, x_ref, r_ref, *refs, n_exp, col0):
    w_refs, (g_ref, b_ref, o_ref) = refs[:3 * n_exp], refs[3 * n_exp:]

    @pl.when(pl.program_id(0) < ex_ref[n_exp * pl.num_programs(0)])
    def _():
        x = x_ref[...]
        cd = w_refs[0].dtype
        xb = x.astype(cd)
        r = r_ref[...]
        y = None
        for e in range(n_exp):
            w1_ref, w3_ref, w2_ref = w_refs[3 * e:3 * e + 3]
            h1 = _dot(xb, w1_ref[...])
            h3 = _dot(xb, w3_ref[...])
            ye = r[:, col0 + e:col0 + e + 1] * _dot((h1 * _sigmoid(h1) * h3).astype(cd), w2_ref[...])
            y = ye if y is None else y + ye
        o_ref[...] = _layer_norm(DN_ALPHA * x + y, g_ref[...], b_ref[...])


def _moe_sorted(xs, rs, experts, tiles_used, w, tm, col0):
    n_pad = xs.shape[0]
    n_exp, nt = experts.shape
    layer = w['layer']
    row = lambda i, ex: (i, 0)
    vec = pl.BlockSpec((1, D_MODEL), lambda i, ex: (0, 0))
    mode = dict(pipeline_mode=pl.Buffered(1)) if w['w1'].dtype == F32 else {}
    w_specs, w_args = [], []
    for e in range(n_exp):
        pick = lambda i, ex, e=e: (layer, ex[e * nt + i], 0, 0)
        w_specs += [pl.BlockSpec((None, None, D_MODEL, D_EXPERT), pick, **mode),
                    pl.BlockSpec((None, None, D_MODEL, D_EXPERT), pick, **mode),
                    pl.BlockSpec((None, None, D_EXPERT, D_MODEL), pick, **mode)]
        w_args += [w['w1'], w['w3'], w['w2']]
    return pl.pallas_call(
        functools.partial(_moe_kernel, n_exp=n_exp, col0=col0),
        grid_spec=pltpu.PrefetchScalarGridSpec(
            num_scalar_prefetch=1,
            grid=(nt,),
            in_specs=[pl.BlockSpec((tm, D_MODEL), row), pl.BlockSpec((tm, ROUTE_LANES), row)] + w_specs + [vec, vec],
            out_specs=pl.BlockSpec((tm, D_MODEL), row),
        ),
        out_shape=jax.ShapeDtypeStruct((n_pad, D_MODEL), F32),
        compiler_params=_params("arbitrary"),
        name="moe",
    )(jnp.concatenate([experts.reshape(n_exp * nt), tiles_used.astype(jnp.int32).reshape(1)]),
      xs, rs, *w_args, w['ln3_g'], w['ln3_b'])


def _moe(x2, route, w):
    n = x2.shape[0]
    n_pairs = len(EXPERT_PAIRS)
    by_pair = n >= N_SEGMENTS * MOE_TILE
    if by_pair:
        tm, n_seg, col0 = MOE_TILE, N_SEGMENTS, 0
        seg = route[:, ROUTE_PAIR_COL].astype(jnp.int32)
    else:
        tm, n_seg, col0 = min(MOE_TILE, n // N_GROUPS), N_GROUPS, ROUTE_PAIR_COL + 1
        seg = route[:, ROUTE_GROUP_COL].astype(jnp.int32)
    pos, src, seg_tile, tiles_used = _dispatch_plan(seg, tm, n_seg)
    if by_pair:
        p = seg_tile % n_pairs
        member = lambda side: functools.reduce(
            lambda x, y: x + y, [jnp.where(p == k, pr[side], 0) for k, pr in enumerate(EXPERT_PAIRS)])
        experts = GROUP_SIZE * (seg_tile // n_pairs) + jnp.stack([member(0), member(1)])
    else:
        experts = GROUP_SIZE * seg_tile[None, :] + jnp.arange(GROUP_SIZE, dtype=jnp.int32)[:, None]
    xs, rs = _sc_gather_rows([x2, route], src)
    ys = _moe_sorted(xs, rs, experts.astype(jnp.int32), tiles_used, w, tm, col0)
    return _sc_gather_rows([ys], pos)[0]


def _trunk_layer(x, w, mem_k, mem_v, cache_k, cache_v, s0, h0, conv0, batch, seq, prompt):
    qkv, hg, rg, gl = _inproj(x, w['w_in'], w['layer'])
    keep = min(A_WINDOW, seq)
    x_keep = x.reshape(batch, seq, D_MODEL)[:, seq - keep:].reshape(batch * keep, D_MODEL)
    k32, v32 = [t.reshape(batch, keep, A_HEADS, A_HEAD_DIM) for t in _kv_rows(x_keep, w['w_in'], w['layer'])]
    if prompt:
        oa = _attn_prompt(qkv, w['band_bias'], batch, seq)
        new_k, new_v = k32, v32
        tile = 256
    else:
        win = cache_k.shape[1]
        ck = cache_k.reshape(batch, win, A_WIDTH)
        cv = cache_v.reshape(batch, win, A_WIDTH)
        oa = _attn_sample(qkv, ck, cv, w['band_bias'][:, :seq, :win + 2 * CHUNK], batch, seq)
        new_k = jnp.concatenate([cache_k, k32], axis=1)[:, seq:]
        new_v = jnp.concatenate([cache_v, v32], axis=1)[:, seq:]
        tile = seq
    ob, s_bd = _hgrn(hg, w['lb'], w['hgrn_g'], _state_to_block_diag(s0), batch, seq, tile, w['w_in'].dtype)
    conv0_pad = jnp.concatenate([jnp.zeros((batch, 8 - (C_CONV - 1), C_WIDTH), F32), conv0.astype(F32)], axis=1)
    oc, h_new = _rglru(rg, conv0_pad, h0.astype(F32).reshape(batch, 1, C_WIDTH), w, batch, seq, tile, prompt)
    conv_new = jnp.concatenate([conv0.astype(F32), rg.reshape(batch, seq, 2 * C_WIDTH)[:, :, :C_WIDTH]],
                               axis=1)[:, seq:]
    x2, route = _mix(x, oa, ob, oc, gl, mem_k, mem_v, w, batch, seq)
    x3 = _moe(x2, route, w)
    return x3, (new_k, new_v, _block_diag_to_state(s_bd), h_new.reshape(batch, C_WIDTH), conv_new)


def kernel(x_prompt, x_sample, cache_attn_k, cache_attn_v, state_hgrn, state_rglru, state_conv, cache_mem_k, cache_mem_v, mem_prompt, w_in, attn_rel_bias, hgrn_lb_logits, hgrn_norm_g, rg_conv_w, rg_conv_b, rg_wa, rg_ba, rg_wx, rg_bx, rg_lambda, w_branch, w_out, ln1_g, ln1_b, xa_wq, xa_wk, xa_wv, xa_wo, ln2_g, ln2_b, moe_router, moe_w1, moe_w3, moe_w2, ln3_g, ln3_b):
    bp, tp, _ = x_prompt.shape
    bs, ts, _ = x_sample.shape
    n_mem = mem_prompt.shape[1]
    depth = w_in.shape[0]

    p = jax.nn.softmax(hgrn_lb_logits.astype(F32), axis=0)
    lb_all = jnp.cumsum(p, axis=0) - p[0:1]
    vec = lambda t: t.astype(F32).reshape(1, -1)

    xp = x_prompt.reshape(bp * tp, D_MODEL)
    xs = x_sample.reshape(bs * ts, D_MODEL)
    mem2d = mem_prompt.reshape(bp * n_mem, D_MODEL)
    s0p = jnp.zeros((bp, B_HEADS, B_KEY_DIM, B_VAL_DIM), F32)
    h0p = jnp.zeros((bp, C_WIDTH), F32)
    conv0p = jnp.zeros((bp, C_CONV - 1, C_WIDTH), F32)
    router_t = moe_router.astype(F32).T

    stacked = {'w_in': w_in, 'w_branch': w_branch, 'w_out': w_out, 'xa_wq': xa_wq, 'xa_wo': xa_wo,
               'w1': moe_w1, 'w3': moe_w3, 'w2': moe_w2}
    stacked_fast = {k: v.astype(BF16) for k, v in stacked.items()}
    stacked_precise = {k: v.astype(F32) for k, v in stacked.items()}
    wk_b, wv_b = xa_wk.astype(BF16), xa_wv.astype(BF16)

    outs_p, outs_s, mem_ks, mem_vs = [], [], [], []
    for l in range(depth):
        wf = {'wa_bd': _block_diag_weight(rg_wa[l]), 'wx_bd': _block_diag_weight(rg_wx[l])}
        shared = {
            'layer': l, 'band_bias': _band_bias(attn_rel_bias[l]),
            'lb': vec(lb_all[l]), 'hgrn_g': vec(jnp.tile(hgrn_norm_g[l], B_HEADS)),
            'conv_w': rg_conv_w[l].astype(F32), 'conv_b': vec(rg_conv_b[l]),
            'ba': vec(rg_ba[l]), 'bx': vec(rg_bx[l]), 'lam': vec(rg_lambda[l]),
            'ln1_g': vec(ln1_g[l]), 'ln1_b': vec(ln1_b[l]), 'ln2_g': vec(ln2_g[l]), 'ln2_b': vec(ln2_b[l]),
            'ln3_g': vec(ln3_g[l]), 'ln3_b': vec(ln3_b[l]), 'router_t': router_t,
        }
        w_fast = dict(shared, **stacked_fast, **{k: v.astype(BF16) for k, v in wf.items()})
        w_precise = dict(shared, **stacked_precise, **{k: v.astype(F32) for k, v in wf.items()})
        mk_p = _matmul(mem2d, wk_b, l)
        mv_p = _matmul(mem2d, wv_b, l)
        mem_ks.append(mk_p.reshape(bp, n_mem, X_HEADS, X_HEAD_DIM))
        mem_vs.append(mv_p.reshape(bp, n_mem, X_HEADS, X_HEAD_DIM))
        xp, st_p = _trunk_layer(xp, w_fast, mk_p.reshape(bp, n_mem, D_MODEL).astype(BF16),
                                mv_p.reshape(bp, n_mem, D_MODEL).astype(BF16),
                                None, None, s0p, h0p, conv0p, bp, tp, True)
        xs, st_s = _trunk_layer(xs, w_precise, cache_mem_k[l].reshape(bs, n_mem, D_MODEL).astype(F32),
                                cache_mem_v[l].reshape(bs, n_mem, D_MODEL).astype(F32),
                                cache_attn_k[l], cache_attn_v[l], state_hgrn[l], state_rglru[l], state_conv[l],
                                bs, ts, False)
        outs_p.append(st_p)
        outs_s.append(st_s)

    stack = lambda items, j: jnp.stack([it[j] for it in items])
    return (xp.reshape(bp, tp, D_MODEL), xs.reshape(bs, ts, D_MODEL),
            stack(outs_p, 0), stack(outs_p, 1), stack(outs_p, 2), stack(outs_p, 3), stack(outs_p, 4),
            jnp.stack(mem_ks), jnp.stack(mem_vs),
            stack(outs_s, 0), stack(outs_s, 1), stack(outs_s, 2), stack(outs_s, 3), stack(outs_s, 4))
```

```python
import functools

import numpy as np
import jax
import jax.numpy as jnp
from jax import lax
from jax.experimental import pallas as pl
from jax.experimental.pallas import tpu as pltpu
from jax.experimental.pallas import tpu_sc as plsc

F32 = jnp.float32
BF16 = jnp.bfloat16

D_MODEL = 1024
CHUNK = 64
A_HEADS = 8
A_HEAD_DIM = 64
A_WIDTH = A_HEADS * A_HEAD_DIM
A_PAST_CHUNKS = 8
A_WINDOW = A_PAST_CHUNKS * CHUNK
A_MAX_REL = 256
B_HEADS = 8
B_KEY_DIM = 64
B_VAL_DIM = 64
B_WIDTH = B_HEADS * B_KEY_DIM
C_WIDTH = 512
C_CONV = 4
C_GATE_C = 8.0
N_BRANCH = 3
IN_COLS = 3 * A_WIDTH + 4 * B_WIDTH + 2 * C_WIDTH + N_BRANCH * D_MODEL
X_HEADS = 4
X_HEAD_DIM = D_MODEL // X_HEADS
N_EXPERTS = 16
N_GROUPS = 4
GROUP_SIZE = N_EXPERTS // N_GROUPS
D_EXPERT = D_MODEL // 2
DEPTH = 4
DN_ALPHA = (2 * DEPTH) ** 0.25
LN_EPS = 1e-5
RMS_EPS = 1e-6
NEG_INF = -1e30

VMEM_LIMIT_BYTES = 56 * 1024 * 1024

INPROJ_TILE = 512
MIX_TILE = 256
ATTN_Q_TILE = 256
ATTN_K_TILE = 256
HG_GROUP = 256
RG_SCAN_GROUP = 8
HG_SEQS_PER_STEP = 2
HG_HALF = CHUNK // 2
HG_QUARTER = CHUNK // 4
EXPERT_PAIRS = tuple((lo, hi) for lo in range(GROUP_SIZE) for hi in range(lo + 1, GROUP_SIZE))
N_SEGMENTS = N_GROUPS * len(EXPERT_PAIRS)
ROUTE_ROWS = 16
ROUTE_PAIR_COL = 2
ROUTE_GROUP_COL = 7
ROUTE_LANES = 128
MOE_TILE = 512
SC_CORES = 2
SC_SUBCORES = 16
SC_WORKERS = SC_CORES * SC_SUBCORES
SC_GATHER_WINDOW = 32


def _split2(x):
    hi = x.astype(BF16)
    lo = (x - hi.astype(F32)).astype(BF16)
    return hi, lo


def _contract(a, b, dims):
    dg = lambda u, v: lax.dot_general(u, v, (dims, ((), ())), preferred_element_type=F32)
    if a.dtype == F32:
        a_hi, a_lo = _split2(a)
        b_hi, b_lo = _split2(b)
        return dg(a_lo, b_hi) + dg(a_hi, b_lo) + dg(a_hi, b_hi)
    return dg(a, b)


def _dot(a, b):
    return _contract(a, b, ((1,), (0,)))


def _dot_nt(a, b):
    return _contract(a, b, ((1,), (1,)))


def _dot_tn(a, b):
    return _contract(a, b, ((0,), (0,)))


def _split3(x):
    hi = x.astype(BF16)
    r1 = x - hi.astype(F32)
    mid = r1.astype(BF16)
    lo = (r1 - mid.astype(F32)).astype(BF16)
    return hi, mid, lo


def _sigmoid(x):
    return 0.5 * jnp.tanh(0.5 * x) + 0.5


def _layer_norm(x, g, b):
    mu = jnp.mean(x, axis=-1, keepdims=True)
    xc = x - mu
    var = jnp.mean(xc * xc, axis=-1, keepdims=True)
    return xc * lax.rsqrt(var + LN_EPS) * g + b


def _params(*semantics):
    return pltpu.CompilerParams(dimension_semantics=semantics, vmem_limit_bytes=VMEM_LIMIT_BYTES)


def _const_spec(shape):
    nd = len(shape)
    return pl.BlockSpec(shape, lambda *_: (0,) * nd, pipeline_mode=pl.Buffered(1))


def _layer_spec(shape, layer):
    nd = len(shape)
    return pl.BlockSpec((None,) + tuple(shape), lambda *_: (layer,) + (0,) * nd, pipeline_mode=pl.Buffered(1))


def _inproj_kernel(x_ref, w_ref, qkv_ref, hg_ref, rg_ref, gl_ref):
    xb = x_ref[...].astype(w_ref.dtype)
    cw = 512

    def mm(c0):
        return _dot(xb, w_ref[:, c0:c0 + cw])

    for j in range(3):
        qkv_ref[:, cw * j:cw * (j + 1)] = mm(cw * j).astype(qkv_ref.dtype)
    base = 3 * A_WIDTH
    for j in range(4):
        hg_ref[:, cw * j:cw * (j + 1)] = mm(base + cw * j)
    base += 4 * B_WIDTH
    for j in range(2):
        rg_ref[:, cw * j:cw * (j + 1)] = mm(base + cw * j)
    base += 2 * C_WIDTH
    for j in range(N_BRANCH * D_MODEL // cw):
        gl_ref[:, cw * j:cw * (j + 1)] = mm(base + cw * j).astype(gl_ref.dtype)


def _inproj(x, w_in, layer):
    n = x.shape[0]
    cd = w_in.dtype
    tm = INPROJ_TILE if cd == BF16 else INPROJ_TILE // 2
    assert n % tm == 0
    row = lambda i: (i, 0)
    return pl.pallas_call(
        _inproj_kernel,
        grid=(n // tm,),
        in_specs=[pl.BlockSpec((tm, D_MODEL), row), _layer_spec((D_MODEL, IN_COLS), layer)],
        out_specs=[pl.BlockSpec((tm, 3 * A_WIDTH), row),
                   pl.BlockSpec((tm, 4 * B_WIDTH), row), pl.BlockSpec((tm, 2 * C_WIDTH), row),
                   pl.BlockSpec((tm, N_BRANCH * D_MODEL), row)],
        out_shape=[jax.ShapeDtypeStruct((n, 3 * A_WIDTH), cd),
                   jax.ShapeDtypeStruct((n, 4 * B_WIDTH), F32), jax.ShapeDtypeStruct((n, 2 * C_WIDTH), F32),
                   jax.ShapeDtypeStruct((n, N_BRANCH * D_MODEL), cd)],
        compiler_params=_params("arbitrary"),
        name="inproj",
    )(x, w_in)


def _matmul_kernel(x_ref, w_ref, o_ref):
    o_ref[...] = _dot(x_ref[...].astype(w_ref.dtype), w_ref[...])


def _matmul(x, w, layer):
    n, k = x.shape
    m = w.shape[2]
    tm = 256
    return pl.pallas_call(
        _matmul_kernel,
        grid=(n // tm,),
        in_specs=[pl.BlockSpec((tm, k), lambda i: (i, 0)), _layer_spec((k, m), layer)],
        out_specs=pl.BlockSpec((tm, m), lambda i: (i, 0)),
        out_shape=jax.ShapeDtypeStruct((n, m), F32),
        compiler_params=_params("arbitrary"),
        name="matmul",
    )(x, w)


def _kv_kernel(x_ref, wk_ref, wv_ref, k_ref, v_ref):
    xb = x_ref[...].astype(wk_ref.dtype)
    k_ref[...] = _dot(xb, wk_ref[...])
    v_ref[...] = _dot(xb, wv_ref[...])


def _kv_rows(x, w_in, layer):
    n = x.shape[0]
    tm = 256
    col = lambda j: pl.BlockSpec((None, D_MODEL, A_WIDTH), lambda i: (layer, 0, j), pipeline_mode=pl.Buffered(1))
    out = pl.BlockSpec((tm, A_WIDTH), lambda i: (i, 0))
    return pl.pallas_call(
        _kv_kernel,
        grid=(n // tm,),
        in_specs=[pl.BlockSpec((tm, D_MODEL), lambda i: (i, 0)), col(1), col(2)],
        out_specs=[out, out],
        out_shape=[jax.ShapeDtypeStruct((n, A_WIDTH), F32)] * 2,
        compiler_params=_params("arbitrary"),
        name="kv_rows",
    )(x, w_in, w_in)


def _attn_core(q, k, v, bias_ref, valid):
    rows = q.shape[0]
    lane = lax.broadcasted_iota(jnp.int32, (rows, 2 * A_HEAD_DIM), 1)
    first = lane < A_HEAD_DIM
    q = q * jnp.asarray(A_HEAD_DIM ** -0.5, q.dtype)

    def scores(head):
        sl = slice(2 * A_HEAD_DIM * (head // 2), 2 * A_HEAD_DIM * (head // 2 + 1))
        sel = first if head % 2 == 0 else jnp.logical_not(first)
        qm = jnp.where(sel, q[:, sl], jnp.zeros_like(q[:, sl]))
        s = _dot_nt(qm, k[:, sl]) + bias_ref[head]
        return s if valid is None else jnp.where(valid, s, NEG_INF)

    outs = []
    pair = None
    s_next = scores(0)
    for head in range(A_HEADS):
        s = s_next
        if head + 1 < A_HEADS:
            s_next = scores(head + 1)
        m = jnp.max(s, axis=-1, keepdims=True)
        e = jnp.exp(s - m)
        l = jnp.sum(e, axis=-1, keepdims=True)
        sl = slice(2 * A_HEAD_DIM * (head // 2), 2 * A_HEAD_DIM * (head // 2 + 1))
        o = _dot(e.astype(v.dtype), v[:, sl]) * (1.0 / l)
        if head % 2 == 0:
            pair = o
        else:
            outs.append(jnp.where(first, pair, o))
    return jnp.concatenate(outs, axis=-1)


def _attn_prompt_kernel(q_ref, k0_ref, k1_ref, k2_ref, v0_ref, v1_ref, v2_ref, bias_ref, o_ref):
    i = pl.program_id(1)

    def run(masked):
        k = jnp.concatenate([k0_ref[...], k1_ref[...], k2_ref[...]], axis=0)
        v = jnp.concatenate([v0_ref[...], v1_ref[...], v2_ref[...]], axis=0)
        valid = None
        if masked:
            col = lax.broadcasted_iota(jnp.int32, (ATTN_Q_TILE, 3 * ATTN_K_TILE), 1)
            valid = col >= (2 - i) * ATTN_K_TILE
        o_ref[...] = _attn_core(q_ref[...], k, v, bias_ref, valid).astype(o_ref.dtype)

    pl.when(i < 2)(lambda: run(True))
    pl.when(i >= 2)(lambda: run(False))


def _attn_prompt(qkv, bias, batch, seq):
    nt = seq // ATTN_Q_TILE
    blk = (ATTN_Q_TILE, A_WIDTH)

    def kv_spec(j, col):
        return pl.BlockSpec(blk, lambda b, i: (b * nt + jnp.maximum(i - 2 + j, 0), col))

    return pl.pallas_call(
        _attn_prompt_kernel,
        grid=(batch, nt),
        in_specs=[pl.BlockSpec(blk, lambda b, i: (b * nt + i, 0))]
        + [kv_spec(j, 1) for j in range(3)] + [kv_spec(j, 2) for j in range(3)]
        + [_const_spec(bias.shape)],
        out_specs=pl.BlockSpec(blk, lambda b, i: (b * nt + i, 0)),
        out_shape=jax.ShapeDtypeStruct((batch * seq, A_WIDTH), qkv.dtype),
        compiler_params=_params("arbitrary", "arbitrary"),
        name="attn_prompt",
    )(qkv, qkv, qkv, qkv, qkv, qkv, qkv, bias)


def _attn_sample_kernel(q_ref, kn_ref, vn_ref, ck_ref, cv_ref, bias_ref, o_ref):
    cd = q_ref.dtype
    pad = jnp.zeros((CHUNK, A_WIDTH), cd)
    k = jnp.concatenate([ck_ref[0].astype(cd), kn_ref[...], pad], axis=0)
    v = jnp.concatenate([cv_ref[0].astype(cd), vn_ref[...], pad], axis=0)
    o_ref[...] = _attn_core(q_ref[...], k, v, bias_ref, None).astype(cd)


def _attn_sample(qkv, cache_k, cache_v, bias, batch, seq):
    win = cache_k.shape[1]
    blk = (seq, A_WIDTH)
    return pl.pallas_call(
        _attn_sample_kernel,
        grid=(batch,),
        in_specs=[pl.BlockSpec(blk, lambda b: (b, 0)), pl.BlockSpec(blk, lambda b: (b, 1)),
                  pl.BlockSpec(blk, lambda b: (b, 2)),
                  pl.BlockSpec((1, win, A_WIDTH), lambda b: (b, 0, 0)),
                  pl.BlockSpec((1, win, A_WIDTH), lambda b: (b, 0, 0)),
                  _const_spec(bias.shape)],
        out_specs=pl.BlockSpec(blk, lambda b: (b, 0)),
        out_shape=jax.ShapeDtypeStruct((batch * seq, A_WIDTH), qkv.dtype),
        compiler_params=_params("arbitrary"),
        name="attn_sample",
    )(qkv, qkv, qkv, cache_k, cache_v, bias)


def _band_bias(table):
    heads = table.shape[0]
    n_keys = 3 * ATTN_K_TILE
    span = ATTN_Q_TILE + n_keys
    r = np.arange(ATTN_Q_TILE)[:, None]
    j = np.arange(n_keys)[None, :]
    band = (j // CHUNK >= r // CHUNK) & (j // CHUNK <= r // CHUNK + A_PAST_CHUNKS)
    t = table.astype(F32)
    u = jnp.concatenate([t, jnp.broadcast_to(t[:, -1:], (heads, span - t.shape[1]))], axis=1)
    w = u[:, ::-1]
    tiled = jnp.broadcast_to(w[:, None, :], (heads, ATTN_Q_TILE, span)).reshape(heads, ATTN_Q_TILE * span)
    view = tiled[:, :ATTN_Q_TILE * (span - 1)].reshape(heads, ATTN_Q_TILE, span - 1)
    bias = view[:, :, ATTN_Q_TILE - 1:ATTN_Q_TILE - 1 + n_keys]
    return jnp.where(band[None], bias, NEG_INF)


def _head_block_diag(x, head_masks):
    zero = jnp.zeros_like(x)
    return jnp.concatenate([jnp.where(m, x, zero) for m in head_masks], axis=0)


def _hgrn_kernel(hg_ref, lb_ref, ng_ref, s0_ref, o_ref, sfin_ref, st_ref, ot_ref, *, tile):
    i = pl.program_id(1)

    @pl.when(i == 0)
    def _():
        st_ref[...] = s0_ref[...]

    for bb in range(hg_ref.shape[0]):
        _hgrn_rows(hg_ref.at[bb], lb_ref, ng_ref, o_ref.at[bb], st_ref.at[bb], ot_ref.at[bb], tile)

    @pl.when(i == pl.num_programs(1) - 1)
    def _():
        sfin_ref[...] = st_ref[...]


def _hgrn_rows(hg_ref, lb_ref, ng_ref, o_ref, st_ref, ot_ref, tile):
    n_groups = B_WIDTH // HG_GROUP
    cd = o_ref.dtype
    q = hg_ref[:, 0:B_WIDTH]
    f_logit = hg_ref[:, B_WIDTH:2 * B_WIDTH]
    v_in = hg_ref[:, 2 * B_WIDTH:3 * B_WIDTH].astype(cd)
    lb = lb_ref[...]
    f = lb + (1.0 - lb) * (1.0 / (1.0 + jnp.exp(-f_logit)))
    log_f = jnp.log(f)
    kk = 1.0 - f

    r_t = lax.broadcasted_iota(jnp.int32, (tile, tile), 0)
    c_t = lax.broadcasted_iota(jnp.int32, (tile, tile), 1)
    tri = jnp.where((r_t // CHUNK == c_t // CHUNK) & (c_t <= r_t), 1.0, 0.0).astype(BF16)
    split = _split3 if cd == F32 else _split2
    g_all = functools.reduce(lambda x, y: x + y, [_dot(tri, part) for part in reversed(split(log_f))])

    row = lax.broadcasted_iota(jnp.int32, (CHUNK, B_WIDTH), 0)
    upper = row >= HG_HALF
    lane_g = lax.broadcasted_iota(jnp.int32, (CHUNK, HG_GROUP), 1)
    row_g = lax.broadcasted_iota(jnp.int32, (CHUNK, HG_GROUP), 0)
    head_masks = [lane_g // B_KEY_DIM == h for h in range(HG_GROUP // B_KEY_DIM)]
    causal = (lane_g % CHUNK) <= row_g
    cross = (row_g >= HG_HALF) & ((lane_g % CHUNK) < HG_HALF)
    r_bd = lax.broadcasted_iota(jnp.int32, (HG_GROUP, HG_GROUP), 0)
    c_bd = lax.broadcasted_iota(jnp.int32, (HG_GROUP, HG_GROUP), 1)
    diag_blocks = (r_bd // B_VAL_DIM) == (c_bd // B_KEY_DIM)

    pending = []
    for c in range(tile // CHUNK):
        rs = slice(CHUNK * c, CHUNK * (c + 1))
        g = g_all[rs]
        qc = q[rs]
        kc = kk[rs]
        vc = v_in[rs]
        g_q1 = g[HG_QUARTER - 1:HG_QUARTER]
        g_mid = g[HG_HALF - 1:HG_HALF]
        g_q3 = g[HG_HALF + HG_QUARTER - 1:HG_HALF + HG_QUARTER]
        g_last = g[CHUNK - 1:CHUNK]
        d_diag = g - jnp.where(upper, g_q3, g_q1)
        d_off = jnp.where(upper, g - g_mid, g_mid - g)
        q_in = (qc * jnp.exp(g)).astype(cd)
        k_st = (kc * jnp.exp(g_last - g)).astype(cd)
        q_diag = qc * jnp.exp(d_diag)
        k_diag = kc * jnp.exp(-d_diag)
        e_off = jnp.exp(d_off)
        zero = jnp.zeros_like(qc)
        q_d = q_diag.astype(cd)
        k_d = k_diag.astype(cd)
        q_x = jnp.where(upper, qc * e_off, zero).astype(cd)
        k_x = jnp.where(upper, zero, kc * e_off).astype(cd)
        decay = jnp.exp(g_last)
        for gi in range(n_groups):
            cs = slice(HG_GROUP * gi, HG_GROUP * (gi + 1))
            att_d = _dot_nt(q_d[:, cs], _head_block_diag(k_d[:, cs], head_masks))
            att_x = _dot_nt(q_x[:, cs], _head_block_diag(k_x[:, cs], head_masks))
            att = jnp.where(cross, att_x, att_d)
            att = jnp.where(causal, att, 0.0).astype(cd)
            v_bd = _head_block_diag(vc[:, cs], head_masks)
            upd = jnp.where(diag_blocks, _dot_tn(vc[:, cs], k_st[:, cs]), 0.0)
            pending.append((rs, cs, gi, q_in[:, cs], _dot(att, v_bd), decay[:, cs], upd))

    states = [st_ref[gi] for gi in range(n_groups)]
    before = []
    for rs, cs, gi, q_in_g, o_intra, decay_g, upd in pending:
        before.append(states[gi].astype(cd))
        states[gi] = states[gi] * decay_g + upd
    for gi in range(n_groups):
        st_ref[gi] = states[gi]
    for (rs, cs, gi, q_in_g, o_intra, decay_g, upd), st_b in zip(pending, before):
        ot_ref[rs, cs] = _dot_nt(q_in_g, st_b) + o_intra

    o = ot_ref[...]
    lane_i = lax.broadcasted_iota(jnp.int32, (B_WIDTH, B_WIDTH), 0)
    lane_j = lax.broadcasted_iota(jnp.int32, (B_WIDTH, B_WIDTH), 1)
    head_ones = jnp.where(lane_i // B_VAL_DIM == lane_j // B_VAL_DIM, 1.0, 0.0).astype(BF16)
    sq_parts = split(o * o)
    sums = _dot(jnp.concatenate(sq_parts, axis=0), head_ones)
    ms = functools.reduce(lambda x, y: x + y, [sums[tile * j:tile * (j + 1)] for j in reversed(range(len(sq_parts)))])
    ms = ms * (1.0 / B_VAL_DIM)
    gate = hg_ref[:, 3 * B_WIDTH:4 * B_WIDTH]
    out = o * lax.rsqrt(ms + RMS_EPS) * ng_ref[...] * (gate * _sigmoid(gate))
    o_ref[...] = out.astype(cd)


def _hgrn(hg, lb, norm_g, s0_bd, batch, seq, tile, cd):
    nt = seq // tile
    n_groups = B_WIDTH // HG_GROUP
    par = HG_SEQS_PER_STEP
    assert batch % par == 0
    st_blk = (par, n_groups, HG_GROUP, HG_GROUP)
    ob, s_fin = pl.pallas_call(
        functools.partial(_hgrn_kernel, tile=tile),
        grid=(batch // par, nt),
        in_specs=[pl.BlockSpec((par, tile, 4 * B_WIDTH), lambda b, i: (b, i, 0)),
                  _const_spec((1, B_WIDTH)), _const_spec((1, B_WIDTH)),
                  pl.BlockSpec(st_blk, lambda b, i: (b, 0, 0, 0))],
        out_specs=[pl.BlockSpec((par, tile, B_WIDTH), lambda b, i: (b, i, 0)),
                   pl.BlockSpec(st_blk, lambda b, i: (b, 0, 0, 0))],
        out_shape=[jax.ShapeDtypeStruct((batch, seq, B_WIDTH), cd),
                   jax.ShapeDtypeStruct((batch,) + st_blk[1:], F32)],
        scratch_shapes=[pltpu.VMEM(st_blk, F32), pltpu.VMEM((par, tile, B_WIDTH), F32)],
        compiler_params=_params("arbitrary", "arbitrary"),
        name="hgrn",
    )(hg.reshape(batch, seq, 4 * B_WIDTH), lb, norm_g, s0_bd)
    return ob.reshape(batch * seq, B_WIDTH), s_fin


def _state_to_block_diag(s):
    b = s.shape[0]
    hpg = HG_GROUP // B_KEY_DIM
    st = s.astype(F32).reshape(b, B_HEADS // hpg, hpg, B_KEY_DIM, B_VAL_DIM).transpose(0, 1, 2, 4, 3)
    bd = jnp.einsum('bghvc,hk->bghvkc', st, jnp.eye(hpg, dtype=F32))
    return bd.reshape(b, B_HEADS // hpg, HG_GROUP, HG_GROUP)


def _block_diag_to_state(bd):
    b = bd.shape[0]
    hpg = HG_GROUP // B_KEY_DIM
    x = bd.reshape(b, B_HEADS // hpg, hpg, B_VAL_DIM, hpg, B_KEY_DIM)
    st = jnp.einsum('bghvkc,hk->bghvc', x, jnp.eye(hpg, dtype=F32))
    return st.transpose(0, 1, 2, 4, 3).reshape(b, B_HEADS, B_KEY_DIM, B_VAL_DIM)


def _rglru_kernel(rg_ref, conv0_ref, h0_ref, cw_ref, cb_ref, wa_ref, ba_ref, wx_ref, bx_ref, lam_ref,
                  o_ref, hlast_ref, xbuf_ref, hc_ref, *, tile, at_start):
    i = pl.program_id(1)
    pad = 8

    @pl.when(i == 0)
    def _():
        xbuf_ref[0:pad] = conv0_ref[0]
        hc_ref[...] = h0_ref[0]

    xr = rg_ref[:, 0:C_WIDTH]
    gate = rg_ref[:, C_WIDTH:2 * C_WIDTH]
    xbuf_ref[pad:pad + tile] = xr
    xc = cb_ref[...] + cw_ref[C_CONV - 1:C_CONV] * xr
    for j in range(1, C_CONV):
        xc = xc + cw_ref[C_CONV - 1 - j:C_CONV - j] * xbuf_ref[pad - j:pad - j + tile]
    xbuf_ref[0:pad] = xbuf_ref[tile:tile + pad]

    xcb = xc.astype(wa_ref.dtype)
    r = _sigmoid(_dot(xcb, wa_ref[...]) + ba_ref[...])
    ig = _sigmoid(_dot(xcb, wx_ref[...]) + bx_ref[...])
    neg_lam = -lam_ref[...]
    softplus = jnp.maximum(neg_lam, 0.0) + jnp.log(1.0 + jnp.exp(-jnp.abs(neg_lam)))
    a = jnp.exp(r * (-C_GATE_C * softplus))
    mult = jnp.sqrt(1.0 - a * a)
    row = lax.broadcasted_iota(jnp.int32, (tile, C_WIDTH), 0)
    if at_start:
        mult = jnp.where((row == 0) & (i == 0), 1.0, mult)
    b = mult * ig * xc

    in_group = row % RG_SCAN_GROUP
    d = 1
    while d < RG_SCAN_GROUP:
        a_sh = pltpu.roll(a, d, 0)
        b_sh = pltpu.roll(b, d, 0)
        keep = in_group >= d
        b = jnp.where(keep, a * b_sh + b, b)
        a = jnp.where(keep, a * a_sh, a)
        d *= 2
    carry = hc_ref[...]
    groups = []
    for j in range(tile // RG_SCAN_GROUP):
        rs = slice(RG_SCAN_GROUP * j, RG_SCAN_GROUP * (j + 1))
        hj = a[rs] * carry + b[rs]
        carry = hj[RG_SCAN_GROUP - 1:RG_SCAN_GROUP]
        groups.append(hj)
    h = jnp.concatenate(groups, axis=0)
    hc_ref[...] = h[tile - 1:tile]
    hlast_ref[0] = h[tile - 1:tile]
    gelu = 0.5 * gate * (1.0 + jnp.tanh(np.sqrt(2.0 / np.pi).astype(np.float32) * (gate + 0.044715 * gate * gate * gate)))
    o_ref[...] = (h * gelu).astype(o_ref.dtype)


def _rglru(rg, conv0_pad, h0, w, batch, seq, tile, at_start):
    nt = seq // tile
    vec = _const_spec((1, C_WIDTH))
    return pl.pallas_call(
        functools.partial(_rglru_kernel, tile=tile, at_start=at_start),
        grid=(batch, nt),
        in_specs=[pl.BlockSpec((tile, 2 * C_WIDTH), lambda b, i: (b * nt + i, 0)),
                  pl.BlockSpec((1, 8, C_WIDTH), lambda b, i: (b, 0, 0)),
                  pl.BlockSpec((1, 1, C_WIDTH), lambda b, i: (b, 0, 0)),
                  _const_spec((C_CONV, C_WIDTH)), vec,
                  _const_spec((C_WIDTH, C_WIDTH)), vec, _const_spec((C_WIDTH, C_WIDTH)), vec, vec],
        out_specs=[pl.BlockSpec((tile, C_WIDTH), lambda b, i: (b * nt + i, 0)),
                   pl.BlockSpec((1, 1, C_WIDTH), lambda b, i: (b, 0, 0))],
        out_shape=[jax.ShapeDtypeStruct((batch * seq, C_WIDTH), w['wa_bd'].dtype),
                   jax.ShapeDtypeStruct((batch, 1, C_WIDTH), F32)],
        scratch_shapes=[pltpu.VMEM((tile + 8, C_WIDTH), F32), pltpu.VMEM((1, C_WIDTH), F32)],
        compiler_params=_params("arbitrary", "arbitrary"),
        name="rglru",
    )(rg, conv0_pad, h0, w['conv_w'], w['conv_b'], w['wa_bd'], w['ba'], w['wx_bd'], w['bx'], w['lam'])


def _block_diag_weight(w):
    n, d, e = w.shape
    return jnp.einsum('nde,nm->ndme', w, jnp.eye(n, dtype=w.dtype)).reshape(n * d, n * e)


def _route(logits_t):
    m = jnp.max(logits_t, axis=0, keepdims=True)
    e = jnp.exp(logits_t - m)
    p = e / jnp.sum(e, axis=0, keepdims=True)
    rows = [p[j:j + 1] for j in range(N_EXPERTS)]
    scores = []
    for g in range(N_GROUPS):
        mem = rows[GROUP_SIZE * g:GROUP_SIZE * (g + 1)]
        best = None
        for a in range(GROUP_SIZE):
            for b in range(a + 1, GROUP_SIZE):
                pair = mem[a] + mem[b]
                best = pair if best is None else jnp.maximum(best, pair)
        scores.append(best)
    smax = functools.reduce(jnp.maximum, scores)
    taken = jnp.zeros_like(smax)
    sel = []
    for g in range(N_GROUPS):
        hit = jnp.where(scores[g] == smax, 1.0, 0.0) * (1.0 - taken)
        taken = taken + hit
        sel.append(hit)
    picked = []
    for j in range(N_EXPERTS):
        g = j // GROUP_SIZE
        rank = jnp.zeros_like(smax)
        for o in range(GROUP_SIZE * g, GROUP_SIZE * (g + 1)):
            if o == j:
                continue
            ahead = (rows[o] >= rows[j]) if o < j else (rows[o] > rows[j])
            rank = rank + jnp.where(ahead, 1.0, 0.0)
        picked.append(sel[g] * jnp.where(rank < float(2), 1.0, 0.0))
    denom = functools.reduce(lambda x, y: x + y, [picked[j] * rows[j] for j in range(N_EXPERTS)])
    comb = [picked[j] * rows[j] / denom for j in range(N_EXPERTS)]
    add = lambda items: functools.reduce(lambda x, y: x + y, items)
    cw = [add([sel[g] * comb[GROUP_SIZE * g + m] for g in range(N_GROUPS)]) for m in range(GROUP_SIZE)]
    on = [add([sel[g] * picked[GROUP_SIZE * g + m] for g in range(N_GROUPS)]) for m in range(GROUP_SIZE)]
    gid = add([float(g) * sel[g] for g in range(1, N_GROUPS)])
    pair_on = [on[lo] * on[hi] for lo, hi in EXPERT_PAIRS]
    w_lo = add([p * cw[lo] for p, (lo, hi) in zip(pair_on, EXPERT_PAIRS)])
    w_hi = add([p * cw[hi] for p, (lo, hi) in zip(pair_on, EXPERT_PAIRS)])
    pair = add([float(k) * p for k, p in enumerate(pair_on) if k > 0])
    segment = gid * float(len(EXPERT_PAIRS)) + pair
    pad = jnp.zeros((ROUTE_ROWS - ROUTE_GROUP_COL - 1,) + smax.shape[1:], F32)
    return jnp.concatenate([w_lo, w_hi, segment] + cw + [gid, pad], axis=0)


def _mix_kernel(x_ref, oa_ref, ob_ref, oc_ref, gl_ref, mk_ref, mv_ref, wb_ref, wout_ref, wq_ref, wo_ref,
                g1_ref, b1_ref, g2_ref, b2_ref, rt_ref, x2_ref, route_ref, *, sub, seq_rows):
    tiles = [slice(s0, s0 + sub) for s0 in range(0, x_ref.shape[0], sub)]
    cd = wb_ref.dtype

    mixed = []
    for rows in tiles:
        acc = None
        for b, o_ref in enumerate((oa_ref, ob_ref, oc_ref)):
            per_branch = _dot(o_ref[rows], wb_ref[b])
            gate = _sigmoid(gl_ref[rows, D_MODEL * b:D_MODEL * (b + 1)].astype(F32))
            acc = gate * per_branch if acc is None else acc + gate * per_branch
        mixed.append(acc.astype(cd))

    x1 = [_layer_norm(DN_ALPHA * x_ref[rows] + _dot(m, wout_ref[...]), g1_ref[...], b1_ref[...])
          for rows, m in zip(tiles, mixed)]
    q = [(_dot(t.astype(cd), wq_ref[...]) * (X_HEAD_DIM ** -0.5)).astype(cd) for t in x1]

    piece = min(sub, seq_rows)
    heads = [[] for _ in tiles]
    for h in range(X_HEADS):
        sl = slice(X_HEAD_DIM * h, X_HEAD_DIM * (h + 1))
        spans = [(t, r0, (t * sub + r0) // seq_rows) for t in range(len(tiles)) for r0 in range(0, sub, piece)]
        scores = [_dot_nt(q[t][r0:r0 + piece, sl], mk_ref[mem, :, sl]) for t, r0, mem in spans]
        outs = [[] for _ in tiles]
        for (t, r0, mem), s in zip(spans, scores):
            m = jnp.max(s, axis=-1, keepdims=True)
            e = jnp.exp(s - m)
            l = jnp.sum(e, axis=-1, keepdims=True)
            outs[t].append((_dot(e.astype(cd), mv_ref[mem, :, sl]) * (1.0 / l)).astype(cd))
        for t, parts in enumerate(outs):
            heads[t].append(parts[0] if len(parts) == 1 else jnp.concatenate(parts, axis=0))

    attn = [_dot(jnp.concatenate(hs, axis=-1), wo_ref[...]) for hs in heads]
    x2 = [_layer_norm(DN_ALPHA * a + b, g2_ref[...], b2_ref[...]) for a, b in zip(x1, attn)]

    r_hi, r_lo = _split2(rt_ref[...])
    r_both = jnp.concatenate([r_hi, r_lo], axis=0)
    eye_r = lax.broadcasted_iota(jnp.int32, (ROUTE_ROWS, ROUTE_LANES), 0)
    eye_c = lax.broadcasted_iota(jnp.int32, (ROUTE_ROWS, ROUTE_LANES), 1)
    eye = jnp.where(eye_r == eye_c, 1.0, 0.0).astype(BF16)
    logits = []
    for rows, t in zip(tiles, x2):
        x2_ref[rows] = t
        x_hi, x_lo = _split2(t)
        by_hi = _dot_nt(r_both, x_hi)
        logits.append(by_hi[N_EXPERTS:] + _dot_nt(r_hi, x_lo) + by_hi[:N_EXPERTS])
    hi, mid, lo = _split3(_route(logits[0] if len(logits) == 1 else jnp.concatenate(logits, axis=1)))
    for rows in tiles:
        route_ref[rows] = _dot_tn(hi[:, rows], eye) + _dot_tn(mid[:, rows], eye) + _dot_tn(lo[:, rows], eye)


def _mix(x, oa, ob, oc, gl, mk, mv, w, batch, seq):
    n_mem = mk.shape[1]
    if seq >= 2 * MIX_TILE:
        sub, tm, per_tile = MIX_TILE, 2 * MIX_TILE, 1
        mem = pl.BlockSpec((1, n_mem, D_MODEL), lambda b, i: (b, 0, 0))
    else:
        assert MIX_TILE % seq == 0 and batch % (MIX_TILE // seq) == 0
        sub, tm, per_tile = MIX_TILE, MIX_TILE, MIX_TILE // seq
        mem = pl.BlockSpec((per_tile, n_mem, D_MODEL), lambda b, i: (b, 0, 0), pipeline_mode=pl.Buffered(1))
    nt = (per_tile * seq) // tm
    row = lambda b, i: (b * nt + i, 0)
    vec = _const_spec((1, D_MODEL))
    sq = _layer_spec((D_MODEL, D_MODEL), w['layer'])
    return pl.pallas_call(
        functools.partial(_mix_kernel, sub=sub, seq_rows=seq),
        grid=(batch // per_tile, nt),
        in_specs=[pl.BlockSpec((tm, D_MODEL), row)] + [pl.BlockSpec((tm, A_WIDTH), row)] * 3
        + [pl.BlockSpec((tm, N_BRANCH * D_MODEL), row), mem, mem,
           _layer_spec((N_BRANCH, A_WIDTH, D_MODEL), w['layer']), sq, sq, sq, vec, vec, vec, vec,
           _const_spec((N_EXPERTS, D_MODEL))],
        out_specs=[pl.BlockSpec((tm, D_MODEL), row), pl.BlockSpec((tm, ROUTE_LANES), row)],
        out_shape=[jax.ShapeDtypeStruct((batch * seq, D_MODEL), F32),
                   jax.ShapeDtypeStruct((batch * seq, ROUTE_LANES), F32)],
        compiler_params=_params("arbitrary", "arbitrary"),
        name="mix",
    )(x, oa, ob, oc, gl, mk, mv, w['w_branch'], w['w_out'], w['xa_wq'], w['xa_wo'],
      w['ln1_g'], w['ln1_b'], w['ln2_g'], w['ln2_b'], w['router_t'])


def _sc_gather_rows(tables, idx):
    n_out = idx.shape[0]
    per_worker = n_out // SC_WORKERS
    window = min(SC_GATHER_WINDOW, per_worker)
    steps = per_worker // window
    assert per_worker * SC_WORKERS == n_out and steps * window == per_worker and window % 8 == 0
    idx3 = idx.astype(jnp.int32).reshape(SC_WORKERS, steps, window)
    mesh = plsc.VectorSubcoreMesh(core_axis_name="core", subcore_axis_name="subcore")
    n_tab = len(tables)

    def body(*refs):
        tab_hbm = refs[:n_tab]
        idx_hbm = refs[n_tab]
        out_hbm = refs[n_tab + 1:2 * n_tab + 1]
        idx_v = refs[2 * n_tab + 1]
        rows_v = refs[2 * n_tab + 2:3 * n_tab + 2]
        sem = refs[3 * n_tab + 2]
        wid = lax.axis_index("subcore") * SC_CORES + lax.axis_index("core")
        pltpu.sync_copy(idx_hbm.at[wid], idx_v)

        @pl.loop(0, steps)
        def _(j):
            base = wid * per_worker + j * window
            for k in range(n_tab):
                pltpu.async_copy(tab_hbm[k].at[idx_v.at[j]], rows_v[k], sem).wait()
                pltpu.sync_copy(rows_v[k], out_hbm[k].at[pl.ds(base, window)])

    call = pl.kernel(
        body,
        out_type=[jax.ShapeDtypeStruct((n_out, t.shape[1]), t.dtype) for t in tables],
        mesh=mesh,
        scratch_types=[pltpu.VMEM((steps, window), jnp.int32)]
        + [pltpu.VMEM((window, t.shape[1]), t.dtype) for t in tables] + [pltpu.SemaphoreType.DMA],
        name="sc_gather",
    )
    return call(*tables, idx3)


def _dispatch_plan(seg, tm, n_seg):
    n = seg.shape[0]
    n_pad = n + n_seg * tm
    ids = jnp.arange(n_seg, dtype=jnp.int32)
    onehot = (seg[:, None] == ids[None, :]).astype(jnp.int32)
    counts = jnp.sum(onehot, axis=0)
    padded = ((counts + tm - 1) // tm) * tm
    start_p = jnp.cumsum(padded) - padded
    start_u = jnp.cumsum(counts) - counts
    order = jnp.argsort(seg, stable=True).astype(jnp.int32)
    rank_sorted = jnp.argsort(order).astype(jnp.int32)
    pos = rank_sorted + jnp.sum(onehot * (start_p - start_u)[None, :], axis=1)
    slot = jnp.arange(n_pad, dtype=jnp.int32)
    seg_slot = jnp.minimum(jnp.sum((slot[:, None] >= (start_p + padded)[None, :]).astype(jnp.int32), axis=1),
                           n_seg - 1)
    oh_slot = (seg_slot[:, None] == ids[None, :]).astype(jnp.int32)
    r_slot = slot - jnp.sum(oh_slot * start_p[None, :], axis=1)
    valid = r_slot < jnp.sum(oh_slot * counts[None, :], axis=1)
    s = jnp.where(valid, jnp.sum(oh_slot * start_u[None, :], axis=1) + r_slot, 0)
    src = jnp.where(valid, jnp.take(order, s), slot % n)
    return pos.astype(jnp.int32), src, seg_slot[::tm], jnp.sum(padded) // tm


def _moe_kernel(ex_ref, x_ref, r_ref, *refs, n_exp, col0):
    w_refs, (g_ref, b_ref, o_ref) = refs[:3 * n_exp], refs[3 * n_exp:]

    @pl.when(pl.program_id(0) < ex_ref[n_exp * pl.num_programs(0)])
    def _():
        x = x_ref[...]
        cd = w_refs[0].dtype
        xb = x.astype(cd)
        r = r_ref[...]
        y = None
        for e in range(n_exp):
            w1_ref, w3_ref, w2_ref = w_refs[3 * e:3 * e + 3]
            h1 = _dot(xb, w1_ref[...])
            h3 = _dot(xb, w3_ref[...])
            ye = r[:, col0 + e:col0 + e + 1] * _dot((h1 * _sigmoid(h1) * h3).astype(cd), w2_ref[...])
            y = ye if y is None else y + ye
        o_ref[...] = _layer_norm(DN_ALPHA * x + y, g_ref[...], b_ref[...])


def _moe_sorted(xs, rs, experts, tiles_used, w, tm, col0):
    n_pad = xs.shape[0]
    n_exp, nt = experts.shape
    layer = w['layer']
    row = lambda i, ex: (i, 0)
    vec = pl.BlockSpec((1, D_MODEL), lambda i, ex: (0, 0))
    mode = dict(pipeline_mode=pl.Buffered(1)) if w['w1'].dtype == F32 else {}
    w_specs, w_args = [], []
    for e in range(n_exp):
        pick = lambda i, ex, e=e: (layer, ex[e * nt + i], 0, 0)
        w_specs += [pl.BlockSpec((None, None, D_MODEL, D_EXPERT), pick, **mode),
                    pl.BlockSpec((None, None, D_MODEL, D_EXPERT), pick, **mode),
                    pl.BlockSpec((None, None, D_EXPERT, D_MODEL), pick, **mode)]
        w_args += [w['w1'], w['w3'], w['w2']]
    return pl.pallas_call(
        functools.partial(_moe_kernel, n_exp=n_exp, col0=col0),
        grid_spec=pltpu.PrefetchScalarGridSpec(
            num_scalar_prefetch=1,
            grid=(nt,),
            in_specs=[pl.BlockSpec((tm, D_MODEL), row), pl.BlockSpec((tm, ROUTE_LANES), row)] + w_specs + [vec, vec],
            out_specs=pl.BlockSpec((tm, D_MODEL), row),
        ),
        out_shape=jax.ShapeDtypeStruct((n_pad, D_MODEL), F32),
        compiler_params=_params("arbitrary"),
        name="moe",
    )(jnp.concatenate([experts.reshape(n_exp * nt), tiles_used.astype(jnp.int32).reshape(1)]),
      xs, rs, *w_args, w['ln3_g'], w['ln3_b'])


def _moe(x2, route, w):
    n = x2.shape[0]
    n_pairs = len(EXPERT_PAIRS)
    by_pair = n >= N_SEGMENTS * MOE_TILE
    if by_pair:
        tm, n_seg, col0 = MOE_TILE, N_SEGMENTS, 0
        seg = route[:, ROUTE_PAIR_COL].astype(jnp.int32)
    else:
        tm, n_seg, col0 = min(MOE_TILE, n // N_GROUPS), N_GROUPS, ROUTE_PAIR_COL + 1
        seg = route[:, ROUTE_GROUP_COL].astype(jnp.int32)
    pos, src, seg_tile, tiles_used = _dispatch_plan(seg, tm, n_seg)
    if by_pair:
        p = seg_tile % n_pairs
        member = lambda side: functools.reduce(
            lambda x, y: x + y, [jnp.where(p == k, pr[side], 0) for k, pr in enumerate(EXPERT_PAIRS)])
        experts = GROUP_SIZE * (seg_tile // n_pairs) + jnp.stack([member(0), member(1)])
    else:
        experts = GROUP_SIZE * seg_tile[None, :] + jnp.arange(GROUP_SIZE, dtype=jnp.int32)[:, None]
    xs, rs = _sc_gather_rows([x2, route], src)
    ys = _moe_sorted(xs, rs, experts.astype(jnp.int32), tiles_used, w, tm, col0)
    return _sc_gather_rows([ys], pos)[0]


def _trunk_layer(x, w, mem_k, mem_v, cache_k, cache_v, s0, h0, conv0, batch, seq, prompt):
    qkv, hg, rg, gl = _inproj(x, w['w_in'], w['layer'])
    keep = min(A_WINDOW, seq)
    x_keep = x.reshape(batch, seq, D_MODEL)[:, seq - keep:].reshape(batch * keep, D_MODEL)
    k32, v32 = [t.reshape(batch, keep, A_HEADS, A_HEAD_DIM) for t in _kv_rows(x_keep, w['w_in'], w['layer'])]
    if prompt:
        oa = _attn_prompt(qkv, w['band_bias'], batch, seq)
        new_k, new_v = k32, v32
        tile = 256
    else:
        win = cache_k.shape[1]
        ck = cache_k.reshape(batch, win, A_WIDTH)
        cv = cache_v.reshape(batch, win, A_WIDTH)
        oa = _attn_sample(qkv, ck, cv, w['band_bias'][:, :seq, :win + 2 * CHUNK], batch, seq)
        new_k = jnp.concatenate([cache_k, k32], axis=1)[:, seq:]
        new_v = jnp.concatenate([cache_v, v32], axis=1)[:, seq:]
        tile = seq
    ob, s_bd = _hgrn(hg, w['lb'], w['hgrn_g'], _state_to_block_diag(s0), batch, seq, tile, w['w_in'].dtype)
    conv0_pad = jnp.concatenate([jnp.zeros((batch, 8 - (C_CONV - 1), C_WIDTH), F32), conv0.astype(F32)], axis=1)
    oc, h_new = _rglru(rg, conv0_pad, h0.astype(F32).reshape(batch, 1, C_WIDTH), w, batch, seq, tile, prompt)
    conv_new = jnp.concatenate([conv0.astype(F32), rg.reshape(batch, seq, 2 * C_WIDTH)[:, :, :C_WIDTH]],
                               axis=1)[:, seq:]
    x2, route = _mix(x, oa, ob, oc, gl, mem_k, mem_v, w, batch, seq)
    x3 = _moe(x2, route, w)
    return x3, (new_k, new_v, _block_diag_to_state(s_bd), h_new.reshape(batch, C_WIDTH), conv_new)


def kernel(x_prompt, x_sample, cache_attn_k, cache_attn_v, state_hgrn, state_rglru, state_conv, cache_mem_k, cache_mem_v, mem_prompt, w_in, attn_rel_bias, hgrn_lb_logits, hgrn_norm_g, rg_conv_w, rg_conv_b, rg_wa, rg_ba, rg_wx, rg_bx, rg_lambda, w_branch, w_out, ln1_g, ln1_b, xa_wq, xa_wk, xa_wv, xa_wo, ln2_g, ln2_b, moe_router, moe_w1, moe_w3, moe_w2, ln3_g, ln3_b):
    bp, tp, _ = x_prompt.shape
    bs, ts, _ = x_sample.shape
    n_mem = mem_prompt.shape[1]
    depth = w_in.shape[0]

    p = jax.nn.softmax(hgrn_lb_logits.astype(F32), axis=0)
    lb_all = jnp.cumsum(p, axis=0) - p[0:1]
    vec = lambda t: t.astype(F32).reshape(1, -1)

    xp = x_prompt.reshape(bp * tp, D_MODEL)
    xs = x_sample.reshape(bs * ts, D_MODEL)
    mem2d = mem_prompt.reshape(bp * n_mem, D_MODEL)
    s0p = jnp.zeros((bp, B_HEADS, B_KEY_DIM, B_VAL_DIM), F32)
    h0p = jnp.zeros((bp, C_WIDTH), F32)
    conv0p = jnp.zeros((bp, C_CONV - 1, C_WIDTH), F32)
    router_t = moe_router.astype(F32).T

    stacked = {'w_in': w_in, 'w_branch': w_branch, 'w_out': w_out, 'xa_wq': xa_wq, 'xa_wo': xa_wo,
               'w1': moe_w1, 'w3': moe_w3, 'w2': moe_w2}
    stacked_fast = {k: v.astype(BF16) for k, v in stacked.items()}
    stacked_precise = {k: v.astype(F32) for k, v in stacked.items()}
    wk_b, wv_b = xa_wk.astype(BF16), xa_wv.astype(BF16)

    outs_p, outs_s, mem_ks, mem_vs = [], [], [], []
    for l in range(depth):
        wf = {'wa_bd': _block_diag_weight(rg_wa[l]), 'wx_bd': _block_diag_weight(rg_wx[l])}
        shared = {
            'layer': l, 'band_bias': _band_bias(attn_rel_bias[l]),
            'lb': vec(lb_all[l]), 'hgrn_g': vec(jnp.tile(hgrn_norm_g[l], B_HEADS)),
            'conv_w': rg_conv_w[l].astype(F32), 'conv_b': vec(rg_conv_b[l]),
            'ba': vec(rg_ba[l]), 'bx': vec(rg_bx[l]), 'lam': vec(rg_lambda[l]),
            'ln1_g': vec(ln1_g[l]), 'ln1_b': vec(ln1_b[l]), 'ln2_g': vec(ln2_g[l]), 'ln2_b': vec(ln2_b[l]),
            'ln3_g': vec(ln3_g[l]), 'ln3_b': vec(ln3_b[l]), 'router_t': router_t,
        }
        w_fast = dict(shared, **stacked_fast, **{k: v.astype(BF16) for k, v in wf.items()})
        w_precise = dict(shared, **stacked_precise, **{k: v.astype(F32) for k, v in wf.items()})
        mk_p = _matmul(mem2d, wk_b, l)
        mv_p = _matmul(mem2d, wv_b, l)
        mem_ks.append(mk_p.reshape(bp, n_mem, X_HEADS, X_HEAD_DIM))
        mem_vs.append(mv_p.reshape(bp, n_mem, X_HEADS, X_HEAD_DIM))
        xp, st_p = _trunk_layer(xp, w_fast, mk_p.reshape(bp, n_mem, D_MODEL).astype(BF16),
                                mv_p.reshape(bp, n_mem, D_MODEL).astype(BF16),
                                None, None, s0p, h0p, conv0p, bp, tp, True)
        xs, st_s = _trunk_layer(xs, w_precise, cache_mem_k[l].reshape(bs, n_mem, D_MODEL).astype(F32),
                                cache_mem_v[l].reshape(bs, n_mem, D_MODEL).astype(F32),
                                cache_attn_k[l], cache_attn_v[l], state_hgrn[l], state_rglru[l], state_conv[l],
                                bs, ts, False)
        outs_p.append(st_p)
        outs_s.append(st_s)

    stack = lambda items, j: jnp.stack([it[j] for it in items])
    return (xp.reshape(bp, tp, D_MODEL), xs.reshape(bs, ts, D_MODEL),
            stack(outs_p, 0), stack(outs_p, 1), stack(outs_p, 2), stack(outs_p, 3), stack(outs_p, 4),
            jnp.stack(mem_ks), jnp.stack(mem_vs),
            stack(outs_s, 0), stack(outs_s, 1), stack(outs_s, 2), stack(outs_s, 3), stack(outs_s, 4))
```

```python
import functools

import numpy as np
import jax
import jax.numpy as jnp
from jax import lax
from jax.experimental import pallas as pl
from jax.experimental.pallas import tpu as pltpu
from jax.experimental.pallas import tpu_sc as plsc

F32 = jnp.float32
BF16 = jnp.bfloat16

D_MODEL = 1024
CHUNK = 64
A_HEADS = 8
A_HEAD_DIM = 64
A_WIDTH = A_HEADS * A_HEAD_DIM
A_PAST_CHUNKS = 8
A_WINDOW = A_PAST_CHUNKS * CHUNK
A_MAX_REL = 256
B_HEADS = 8
B_KEY_DIM = 64
B_VAL_DIM = 64
B_WIDTH = B_HEADS * B_KEY_DIM
C_WIDTH = 512
C_CONV = 4
C_GATE_C = 8.0
N_BRANCH = 3
IN_COLS = 3 * A_WIDTH + 4 * B_WIDTH + 2 * C_WIDTH + N_BRANCH * D_MODEL
X_HEADS = 4
X_HEAD_DIM = D_MODEL // X_HEADS
N_EXPERTS = 16
N_GROUPS = 4
GROUP_SIZE = N_EXPERTS // N_GROUPS
D_EXPERT = D_MODEL // 2
DEPTH = 4
DN_ALPHA = (2 * DEPTH) ** 0.25
LN_EPS = 1e-5
RMS_EPS = 1e-6
NEG_INF = -1e30

VMEM_LIMIT_BYTES = 56 * 1024 * 1024

INPROJ_TILE = 512
MIX_TILE = 256
ATTN_Q_TILE = 256
ATTN_K_TILE = 256
HG_GROUP = 256
RG_SCAN_GROUP = 8
HG_SEQS_PER_STEP = 2
HG_HALF = CHUNK // 2
HG_QUARTER = CHUNK // 4
EXPERT_PAIRS = tuple((lo, hi) for lo in range(GROUP_SIZE) for hi in range(lo + 1, GROUP_SIZE))
N_SEGMENTS = N_GROUPS * len(EXPERT_PAIRS)
ROUTE_ROWS = 16
ROUTE_PAIR_COL = 2
ROUTE_GROUP_COL = 7
ROUTE_LANES = 128
MOE_TILE = 512
SC_CORES = 2
SC_SUBCORES = 16
SC_WORKERS = SC_CORES * SC_SUBCORES
SC_GATHER_WINDOW = 32


def _split2(x):
    hi = x.astype(BF16)
    lo = (x - hi.astype(F32)).astype(BF16)
    return hi, lo


def _contract(a, b, dims):
    dg = lambda u, v: lax.dot_general(u, v, (dims, ((), ())), preferred_element_type=F32)
    if a.dtype == F32:
        a_hi, a_lo = _split2(a)
        b_hi, b_lo = _split2(b)
        return dg(a_lo, b_hi) + dg(a_hi, b_lo) + dg(a_hi, b_hi)
    return dg(a, b)


def _dot(a, b):
    return _contract(a, b, ((1,), (0,)))


def _dot_nt(a, b):
    return _contract(a, b, ((1,), (1,)))


def _dot_tn(a, b):
    return _contract(a, b, ((0,), (0,)))


def _split3(x):
    hi = x.astype(BF16)
    r1 = x - hi.astype(F32)
    mid = r1.astype(BF16)
    lo = (r1 - mid.astype(F32)).astype(BF16)
    return hi, mid, lo


def _sigmoid(x):
    return 0.5 * jnp.tanh(0.5 * x) + 0.5


def _layer_norm(x, g, b):
    mu = jnp.mean(x, axis=-1, keepdims=True)
    xc = x - mu
    var = jnp.mean(xc * xc, axis=-1, keepdims=True)
    return xc * lax.rsqrt(var + LN_EPS) * g + b


def _params(*semantics):
    return pltpu.CompilerParams(dimension_semantics=semantics, vmem_limit_bytes=VMEM_LIMIT_BYTES)


def _const_spec(shape):
    nd = len(shape)
    return pl.BlockSpec(shape, lambda *_: (0,) * nd, pipeline_mode=pl.Buffered(1))


def _layer_spec(shape, layer):
    nd = len(shape)
    return pl.BlockSpec((None,) + tuple(shape), lambda *_: (layer,) + (0,) * nd, pipeline_mode=pl.Buffered(1))


def _inproj_kernel(x_ref, w_ref, qkv_ref, hg_ref, rg_ref, gl_ref):
    xb = x_ref[...].astype(w_ref.dtype)
    cw = 512

    def mm(c0):
        return _dot(xb, w_ref[:, c0:c0 + cw])

    for j in range(3):
        qkv_ref[:, cw * j:cw * (j + 1)] = mm(cw * j).astype(qkv_ref.dtype)
    base = 3 * A_WIDTH
    for j in range(4):
        hg_ref[:, cw * j:cw * (j + 1)] = mm(base + cw * j)
    base += 4 * B_WIDTH
    for j in range(2):
        rg_ref[:, cw * j:cw * (j + 1)] = mm(base + cw * j)
    base += 2 * C_WIDTH
    for j in range(N_BRANCH * D_MODEL // cw):
        gl_ref[:, cw * j:cw * (j + 1)] = mm(base + cw * j).astype(gl_ref.dtype)


def _inproj(x, w_in, layer):
    n = x.shape[0]
    cd = w_in.dtype
    tm = INPROJ_TILE if cd == BF16 else INPROJ_TILE // 2
    assert n % tm == 0
    row = lambda i: (i, 0)
    return pl.pallas_call(
        _inproj_kernel,
        grid=(n // tm,),
        in_specs=[pl.BlockSpec((tm, D_MODEL), row), _layer_spec((D_MODEL, IN_COLS), layer)],
        out_specs=[pl.BlockSpec((tm, 3 * A_WIDTH), row),
                   pl.BlockSpec((tm, 4 * B_WIDTH), row), pl.BlockSpec((tm, 2 * C_WIDTH), row),
                   pl.BlockSpec((tm, N_BRANCH * D_MODEL), row)],
        out_shape=[jax.ShapeDtypeStruct((n, 3 * A_WIDTH), cd),
                   jax.ShapeDtypeStruct((n, 4 * B_WIDTH), F32), jax.ShapeDtypeStruct((n, 2 * C_WIDTH), F32),
                   jax.ShapeDtypeStruct((n, N_BRANCH * D_MODEL), cd)],
        compiler_params=_params("arbitrary"),
        name="inproj",
    )(x, w_in)


def _matmul_kernel(x_ref, w_ref, o_ref):
    o_ref[...] = _dot(x_ref[...].astype(w_ref.dtype), w_ref[...])


def _matmul(x, w, layer):
    n, k = x.shape
    m = w.shape[2]
    tm = 256
    return pl.pallas_call(
        _matmul_kernel,
        grid=(n // tm,),
        in_specs=[pl.BlockSpec((tm, k), lambda i: (i, 0)), _layer_spec((k, m), layer)],
        out_specs=pl.BlockSpec((tm, m), lambda i: (i, 0)),
        out_shape=jax.ShapeDtypeStruct((n, m), F32),
        compiler_params=_params("arbitrary"),
        name="matmul",
    )(x, w)


def _kv_kernel(x_ref, wk_ref, wv_ref, k_ref, v_ref):
    xb = x_ref[...].astype(wk_ref.dtype)
    k_ref[...] = _dot(xb, wk_ref[...])
    v_ref[...] = _dot(xb, wv_ref[...])


def _kv_rows(x, w_in, layer):
    n = x.shape[0]
    tm = 256
    col = lambda j: pl.BlockSpec((None, D_MODEL, A_WIDTH), lambda i: (layer, 0, j), pipeline_mode=pl.Buffered(1))
    out = pl.BlockSpec((tm, A_WIDTH), lambda i: (i, 0))
    return pl.pallas_call(
        _kv_kernel,
        grid=(n // tm,),
        in_specs=[pl.BlockSpec((tm, D_MODEL), lambda i: (i, 0)), col(1), col(2)],
        out_specs=[out, out],
        out_shape=[jax.ShapeDtypeStruct((n, A_WIDTH), F32)] * 2,
        compiler_params=_params("arbitrary"),
        name="kv_rows",
    )(x, w_in, w_in)


def _attn_core(q, k, v, bias_ref, valid):
    rows = q.shape[0]
    lane = lax.broadcasted_iota(jnp.int32, (rows, 2 * A_HEAD_DIM), 1)
    first = lane < A_HEAD_DIM
    q = q * jnp.asarray(A_HEAD_DIM ** -0.5, q.dtype)

    def scores(head):
        sl = slice(2 * A_HEAD_DIM * (head // 2), 2 * A_HEAD_DIM * (head // 2 + 1))
        sel = first if head % 2 == 0 else jnp.logical_not(first)
        qm = jnp.where(sel, q[:, sl], jnp.zeros_like(q[:, sl]))
        s = _dot_nt(qm, k[:, sl]) + bias_ref[head]
        return s if valid is None else jnp.where(valid, s, NEG_INF)

    outs = []
    pair = None
    s_next = scores(0)
    for head in range(A_HEADS):
        s = s_next
        if head + 1 < A_HEADS:
            s_next = scores(head + 1)
        m = jnp.max(s, axis=-1, keepdims=True)
        e = jnp.exp(s - m)
        l = jnp.sum(e, axis=-1, keepdims=True)
        sl = slice(2 * A_HEAD_DIM * (head // 2), 2 * A_HEAD_DIM * (head // 2 + 1))
        o = _dot(e.astype(v.dtype), v[:, sl]) * (1.0 / l)
        if head % 2 == 0:
            pair = o
        else:
            outs.append(jnp.where(first, pair, o))
    return jnp.concatenate(outs, axis=-1)


def _attn_prompt_kernel(q_ref, k0_ref, k1_ref, k2_ref, v0_ref, v1_ref, v2_ref, bias_ref, o_ref):
    i = pl.program_id(1)

    def run(masked):
        k = jnp.concatenate([k0_ref[...], k1_ref[...], k2_ref[...]], axis=0)
        v = jnp.concatenate([v0_ref[...], v1_ref[...], v2_ref[...]], axis=0)
        valid = None
        if masked:
            col = lax.broadcasted_iota(jnp.int32, (ATTN_Q_TILE, 3 * ATTN_K_TILE), 1)
            valid = col >= (2 - i) * ATTN_K_TILE
        o_ref[...] = _attn_core(q_ref[...], k, v, bias_ref, valid).astype(o_ref.dtype)

    pl.when(i < 2)(lambda: run(True))
    pl.when(i >= 2)(lambda: run(False))


def _attn_prompt(qkv, bias, batch, seq):
    nt = seq // ATTN_Q_TILE
    blk = (ATTN_Q_TILE, A_WIDTH)

    def kv_spec(j, col):
        return pl.BlockSpec(blk, lambda b, i: (b * nt + jnp.maximum(i - 2 + j, 0), col))

    return pl.pallas_call(
        _attn_prompt_kernel,
        grid=(batch, nt),
        in_specs=[pl.BlockSpec(blk, lambda b, i: (b * nt + i, 0))]
        + [kv_spec(j, 1) for j in range(3)] + [kv_spec(j, 2) for j in range(3)]
        + [_const_spec(bias.shape)],
        out_specs=pl.BlockSpec(blk, lambda b, i: (b * nt + i, 0)),
        out_shape=jax.ShapeDtypeStruct((batch * seq, A_WIDTH), qkv.dtype),
        compiler_params=_params("arbitrary", "arbitrary"),
        name="attn_prompt",
    )(qkv, qkv, qkv, qkv, qkv, qkv, qkv, bias)


def _attn_sample_kernel(q_ref, kn_ref, vn_ref, ck_ref, cv_ref, bias_ref, o_ref):
    cd = q_ref.dtype
    pad = jnp.zeros((CHUNK, A_WIDTH), cd)
    k = jnp.concatenate([ck_ref[0].astype(cd), kn_ref[...], pad], axis=0)
    v = jnp.concatenate([cv_ref[0].astype(cd), vn_ref[...], pad], axis=0)
    o_ref[...] = _attn_core(q_ref[...], k, v, bias_ref, None).astype(cd)


def _attn_sample(qkv, cache_k, cache_v, bias, batch, seq):
    win = cache_k.shape[1]
    blk = (seq, A_WIDTH)
    return pl.pallas_call(
        _attn_sample_kernel,
        grid=(batch,),
        in_specs=[pl.BlockSpec(blk, lambda b: (b, 0)), pl.BlockSpec(blk, lambda b: (b, 1)),
                  pl.BlockSpec(blk, lambda b: (b, 2)),
                  pl.BlockSpec((1, win, A_WIDTH), lambda b: (b, 0, 0)),
                  pl.BlockSpec((1, win, A_WIDTH), lambda b: (b, 0, 0)),
                  _const_spec(bias.shape)],
        out_specs=pl.BlockSpec(blk, lambda b: (b, 0)),
        out_shape=jax.ShapeDtypeStruct((batch * seq, A_WIDTH), qkv.dtype),
        compiler_params=_params("arbitrary"),
        name="attn_sample",
    )(qkv, qkv, qkv, cache_k, cache_v, bias)


def _band_bias(table):
    heads = table.shape[0]
    n_keys = 3 * ATTN_K_TILE
    span = ATTN_Q_TILE + n_keys
    r = np.arange(ATTN_Q_TILE)[:, None]
    j = np.arange(n_keys)[None, :]
    band = (j // CHUNK >= r // CHUNK) & (j // CHUNK <= r // CHUNK + A_PAST_CHUNKS)
    t = table.astype(F32)
    u = jnp.concatenate([t, jnp.broadcast_to(t[:, -1:], (heads, span - t.shape[1]))], axis=1)
    w = u[:, ::-1]
    tiled = jnp.broadcast_to(w[:, None, :], (heads, ATTN_Q_TILE, span)).reshape(heads, ATTN_Q_TILE * span)
    view = tiled[:, :ATTN_Q_TILE * (span - 1)].reshape(heads, ATTN_Q_TILE, span - 1)
    bias = view[:, :, ATTN_Q_TILE - 1:ATTN_Q_TILE - 1 + n_keys]
    return jnp.where(band[None], bias, NEG_INF)


def _head_block_diag(x, head_masks):
    zero = jnp.zeros_like(x)
    return jnp.concatenate([jnp.where(m, x, zero) for m in head_masks], axis=0)


def _hgrn_kernel(hg_ref, lb_ref, ng_ref, s0_ref, o_ref, sfin_ref, st_ref, ot_ref, *, tile):
    i = pl.program_id(1)

    @pl.when(i == 0)
    def _():
        st_ref[...] = s0_ref[...]

    for bb in range(hg_ref.shape[0]):
        _hgrn_rows(hg_ref.at[bb], lb_ref, ng_ref, o_ref.at[bb], st_ref.at[bb], ot_ref.at[bb], tile)

    @pl.when(i == pl.num_programs(1) - 1)
    def _():
        sfin_ref[...] = st_ref[...]


def _hgrn_rows(hg_ref, lb_ref, ng_ref, o_ref, st_ref, ot_ref, tile):
    n_groups = B_WIDTH // HG_GROUP
    cd = o_ref.dtype
    q = hg_ref[:, 0:B_WIDTH]
    f_logit = hg_ref[:, B_WIDTH:2 * B_WIDTH]
    v_in = hg_ref[:, 2 * B_WIDTH:3 * B_WIDTH].astype(cd)
    lb = lb_ref[...]
    f = lb + (1.0 - lb) * (1.0 / (1.0 + jnp.exp(-f_logit)))
    log_f = jnp.log(f)
    kk = 1.0 - f

    r_t = lax.broadcasted_iota(jnp.int32, (tile, tile), 0)
    c_t = lax.broadcasted_iota(jnp.int32, (tile, tile), 1)
    tri = jnp.where((r_t // CHUNK == c_t // CHUNK) & (c_t <= r_t), 1.0, 0.0).astype(BF16)
    split = _split3 if cd == F32 else _split2
    g_all = functools.reduce(lambda x, y: x + y, [_dot(tri, part) for part in reversed(split(log_f))])

    row = lax.broadcasted_iota(jnp.int32, (CHUNK, B_WIDTH), 0)
    upper = row >= HG_HALF
    lane_g = lax.broadcasted_iota(jnp.int32, (CHUNK, HG_GROUP), 1)
    row_g = lax.broadcasted_iota(jnp.int32, (CHUNK, HG_GROUP), 0)
    head_masks = [lane_g // B_KEY_DIM == h for h in range(HG_GROUP // B_KEY_DIM)]
    causal = (lane_g % CHUNK) <= row_g
    cross = (row_g >= HG_HALF) & ((lane_g % CHUNK) < HG_HALF)
    r_bd = lax.broadcasted_iota(jnp.int32, (HG_GROUP, HG_GROUP), 0)
    c_bd = lax.broadcasted_iota(jnp.int32, (HG_GROUP, HG_GROUP), 1)
    diag_blocks = (r_bd // B_VAL_DIM) == (c_bd // B_KEY_DIM)

    pending = []
    for c in range(tile // CHUNK):
        rs = slice(CHUNK * c, CHUNK * (c + 1))
        g = g_all[rs]
        qc = q[rs]
        kc = kk[rs]
        vc = v_in[rs]
        g_q1 = g[HG_QUARTER - 1:HG_QUARTER]
        g_mid = g[HG_HALF - 1:HG_HALF]
        g_q3 = g[HG_HALF + HG_QUARTER - 1:HG_HALF + HG_QUARTER]
        g_last = g[CHUNK - 1:CHUNK]
        d_diag = g - jnp.where(upper, g_q3, g_q1)
        d_off = jnp.where(upper, g - g_mid, g_mid - g)
        q_in = (qc * jnp.exp(g)).astype(cd)
        k_st = (kc * jnp.exp(g_last - g)).astype(cd)
        q_diag = qc * jnp.exp(d_diag)
        k_diag = kc * jnp.exp(-d_diag)
        e_off = jnp.exp(d_off)
        zero = jnp.zeros_like(qc)
        q_d = q_diag.astype(cd)
        k_d = k_diag.astype(cd)
        q_x = jnp.where(upper, qc * e_off, zero).astype(cd)
        k_x = jnp.where(upper, zero, kc * e_off).astype(cd)
        decay = jnp.exp(g_last)
        for gi in range(n_groups):
            cs = slice(HG_GROUP * gi, HG_GROUP * (gi + 1))
            att_d = _dot_nt(q_d[:, cs], _head_block_diag(k_d[:, cs], head_masks))
            att_x = _dot_nt(q_x[:, cs], _head_block_diag(k_x[:, cs], head_masks))
            att = jnp.where(cross, att_x, att_d)
            att = jnp.where(causal, att, 0.0).astype(cd)
            v_bd = _head_block_diag(vc[:, cs], head_masks)
            upd = jnp.where(diag_blocks, _dot_tn(vc[:, cs], k_st[:, cs]), 0.0)
            pending.append((rs, cs, gi, q_in[:, cs], _dot(att, v_bd), decay[:, cs], upd))

    states = [st_ref[gi] for gi in range(n_groups)]
    before = []
    for rs, cs, gi, q_in_g, o_intra, decay_g, upd in pending:
        before.append(states[gi].astype(cd))
        states[gi] = states[gi] * decay_g + upd
    for gi in range(n_groups):
        st_ref[gi] = states[gi]
    for (rs, cs, gi, q_in_g, o_intra, decay_g, upd), st_b in zip(pending, before):
        ot_ref[rs, cs] = _dot_nt(q_in_g, st_b) + o_intra

    o = ot_ref[...]
    lane_i = lax.broadcasted_iota(jnp.int32, (B_WIDTH, B_WIDTH), 0)
    lane_j = lax.broadcasted_iota(jnp.int32, (B_WIDTH, B_WIDTH), 1)
    head_ones = jnp.where(lane_i // B_VAL_DIM == lane_j // B_VAL_DIM, 1.0, 0.0).astype(BF16)
    sq_parts = split(o * o)
    sums = _dot(jnp.concatenate(sq_parts, axis=0), head_ones)
    ms = functools.reduce(lambda x, y: x + y, [sums[tile * j:tile * (j + 1)] for j in reversed(range(len(sq_parts)))])
    ms = ms * (1.0 / B_VAL_DIM)
    gate = hg_ref[:, 3 * B_WIDTH:4 * B_WIDTH]
    out = o * lax.rsqrt(ms + RMS_EPS) * ng_ref[...] * (gate * _sigmoid(gate))
    o_ref[...] = out.astype(cd)


def _hgrn(hg, lb, norm_g, s0_bd, batch, seq, tile, cd):
    nt = seq // tile
    n_groups = B_WIDTH // HG_GROUP
    par = HG_SEQS_PER_STEP
    assert batch % par == 0
    st_blk = (par, n_groups, HG_GROUP, HG_GROUP)
    ob, s_fin = pl.pallas_call(
        functools.partial(_hgrn_kernel, tile=tile),
        grid=(batch // par, nt),
        in_specs=[pl.BlockSpec((par, tile, 4 * B_WIDTH), lambda b, i: (b, i, 0)),
                  _const_spec((1, B_WIDTH)), _const_spec((1, B_WIDTH)),
                  pl.BlockSpec(st_blk, lambda b, i: (b, 0, 0, 0))],
        out_specs=[pl.BlockSpec((par, tile, B_WIDTH), lambda b, i: (b, i, 0)),
                   pl.BlockSpec(st_blk, lambda b, i: (b, 0, 0, 0))],
        out_shape=[jax.ShapeDtypeStruct((batch, seq, B_WIDTH), cd),
                   jax.ShapeDtypeStruct((batch,) + st_blk[1:], F32)],
        scratch_shapes=[pltpu.VMEM(st_blk, F32), pltpu.VMEM((par, tile, B_WIDTH), F32)],
        compiler_params=_params("arbitrary", "arbitrary"),
        name="hgrn",
    )(hg.reshape(batch, seq, 4 * B_WIDTH), lb, norm_g, s0_bd)
    return ob.reshape(batch * seq, B_WIDTH), s_fin


def _state_to_block_diag(s):
    b = s.shape[0]
    hpg = HG_GROUP // B_KEY_DIM
    st = s.astype(F32).reshape(b, B_HEADS // hpg, hpg, B_KEY_DIM, B_VAL_DIM).transpose(0, 1, 2, 4, 3)
    bd = jnp.einsum('bghvc,hk->bghvkc', st, jnp.eye(hpg, dtype=F32))
    return bd.reshape(b, B_HEADS // hpg, HG_GROUP, HG_GROUP)


def _block_diag_to_state(bd):
    b = bd.shape[0]
    hpg = HG_GROUP // B_KEY_DIM
    x = bd.reshape(b, B_HEADS // hpg, hpg, B_VAL_DIM, hpg, B_KEY_DIM)
    st = jnp.einsum('bghvkc,hk->bghvc', x, jnp.eye(hpg, dtype=F32))
    return st.transpose(0, 1, 2, 4, 3).reshape(b, B_HEADS, B_KEY_DIM, B_VAL_DIM)


def _rglru_kernel(rg_ref, conv0_ref, h0_ref, cw_ref, cb_ref, wa_ref, ba_ref, wx_ref, bx_ref, lam_ref,
                  o_ref, hlast_ref, xbuf_ref, hc_ref, *, tile, at_start):
    i = pl.program_id(1)
    pad = 8

    @pl.when(i == 0)
    def _():
        xbuf_ref[0:pad] = conv0_ref[0]
        hc_ref[...] = h0_ref[0]

    xr = rg_ref[:, 0:C_WIDTH]
    gate = rg_ref[:, C_WIDTH:2 * C_WIDTH]
    xbuf_ref[pad:pad + tile] = xr
    xc = cb_ref[...] + cw_ref[C_CONV - 1:C_CONV] * xr
    for j in range(1, C_CONV):
        xc = xc + cw_ref[C_CONV - 1 - j:C_CONV - j] * xbuf_ref[pad - j:pad - j + tile]
    xbuf_ref[0:pad] = xbuf_ref[tile:tile + pad]

    xcb = xc.astype(wa_ref.dtype)
    r = _sigmoid(_dot(xcb, wa_ref[...]) + ba_ref[...])
    ig = _sigmoid(_dot(xcb, wx_ref[...]) + bx_ref[...])
    neg_lam = -lam_ref[...]
    softplus = jnp.maximum(neg_lam, 0.0) + jnp.log(1.0 + jnp.exp(-jnp.abs(neg_lam)))
    a = jnp.exp(r * (-C_GATE_C * softplus))
    mult = jnp.sqrt(1.0 - a * a)
    row = lax.broadcasted_iota(jnp.int32, (tile, C_WIDTH), 0)
    if at_start:
        mult = jnp.where((row == 0) & (i == 0), 1.0, mult)
    b = mult * ig * xc

    in_group = row % RG_SCAN_GROUP
    d = 1
    while d < RG_SCAN_GROUP:
        a_sh = pltpu.roll(a, d, 0)
        b_sh = pltpu.roll(b, d, 0)
        keep = in_group >= d
        b = jnp.where(keep, a * b_sh + b, b)
        a = jnp.where(keep, a * a_sh, a)
        d *= 2
    carry = hc_ref[...]
    groups = []
    for j in range(tile // RG_SCAN_GROUP):
        rs = slice(RG_SCAN_GROUP * j, RG_SCAN_GROUP * (j + 1))
        hj = a[rs] * carry + b[rs]
        carry = hj[RG_SCAN_GROUP - 1:RG_SCAN_GROUP]
        groups.append(hj)
    h = jnp.concatenate(groups, axis=0)
    hc_ref[...] = h[tile - 1:tile]
    hlast_ref[0] = h[tile - 1:tile]
    gelu = 0.5 * gate * (1.0 + jnp.tanh(np.sqrt(2.0 / np.pi).astype(np.float32) * (gate + 0.044715 * gate * gate * gate)))
    o_ref[...] = (h * gelu).astype(o_ref.dtype)


def _rglru(rg, conv0_pad, h0, w, batch, seq, tile, at_start):
    nt = seq // tile
    vec = _const_spec((1, C_WIDTH))
    return pl.pallas_call(
        functools.partial(_rglru_kernel, tile=tile, at_start=at_start),
        grid=(batch, nt),
        in_specs=[pl.BlockSpec((tile, 2 * C_WIDTH), lambda b, i: (b * nt + i, 0)),
                  pl.BlockSpec((1, 8, C_WIDTH), lambda b, i: (b, 0, 0)),
                  pl.BlockSpec((1, 1, C_WIDTH), lambda b, i: (b, 0, 0)),
                  _const_spec((C_CONV, C_WIDTH)), vec,
                  _const_spec((C_WIDTH, C_WIDTH)), vec, _const_spec((C_WIDTH, C_WIDTH)), vec, vec],
        out_specs=[pl.BlockSpec((tile, C_WIDTH), lambda b, i: (b * nt + i, 0)),
                   pl.BlockSpec((1, 1, C_WIDTH), lambda b, i: (b, 0, 0))],
        out_shape=[jax.ShapeDtypeStruct((batch * seq, C_WIDTH), w['wa_bd'].dtype),
                   jax.ShapeDtypeStruct((batch, 1, C_WIDTH), F32)],
        scratch_shapes=[pltpu.VMEM((tile + 8, C_WIDTH), F32), pltpu.VMEM((1, C_WIDTH), F32)],
        compiler_params=_params("arbitrary", "arbitrary"),
        name="rglru",
    )(rg, conv0_pad, h0, w['conv_w'], w['conv_b'], w['wa_bd'], w['ba'], w['wx_bd'], w['bx'], w['lam'])


def _block_diag_weight(w):
    n, d, e = w.shape
    return jnp.einsum('nde,nm->ndme', w, jnp.eye(n, dtype=w.dtype)).reshape(n * d, n * e)


def _route(logits_t):
    m = jnp.max(logits_t, axis=0, keepdims=True)
    e = jnp.exp(logits_t - m)
    p = e / jnp.sum(e, axis=0, keepdims=True)
    rows = [p[j:j + 1] for j in range(N_EXPERTS)]
    scores = []
    for g in range(N_GROUPS):
        mem = rows[GROUP_SIZE * g:GROUP_SIZE * (g + 1)]
        best = None
        for a in range(GROUP_SIZE):
            for b in range(a + 1, GROUP_SIZE):
                pair = mem[a] + mem[b]
                best = pair if best is None else jnp.maximum(best, pair)
        scores.append(best)
    smax = functools.reduce(jnp.maximum, scores)
    taken = jnp.zeros_like(smax)
    sel = []
    for g in range(N_GROUPS):
        hit = jnp.where(scores[g] == smax, 1.0, 0.0) * (1.0 - taken)
        taken = taken + hit
        sel.append(hit)
    picked = []
    for j in range(N_EXPERTS):
        g = j // GROUP_SIZE
        rank = jnp.zeros_like(smax)
        for o in range(GROUP_SIZE * g, GROUP_SIZE * (g + 1)):
            if o == j:
                continue
            ahead = (rows[o] >= rows[j]) if o < j else (rows[o] > rows[j])
            rank = rank + jnp.where(ahead, 1.0, 0.0)
        picked.append(sel[g] * jnp.where(rank < float(2), 1.0, 0.0))
    denom = functools.reduce(lambda x, y: x + y, [picked[j] * rows[j] for j in range(N_EXPERTS)])
    comb = [picked[j] * rows[j] / denom for j in range(N_EXPERTS)]
    add = lambda items: functools.reduce(lambda x, y: x + y, items)
    cw = [add([sel[g] * comb[GROUP_SIZE * g + m] for g in range(N_GROUPS)]) for m in range(GROUP_SIZE)]
    on = [add([sel[g] * picked[GROUP_SIZE * g + m] for g in range(N_GROUPS)]) for m in range(GROUP_SIZE)]
    gid = add([float(g) * sel[g] for g in range(1, N_GROUPS)])
    pair_on = [on[lo] * on[hi] for lo, hi in EXPERT_PAIRS]
    w_lo = add([p * cw[lo] for p, (lo, hi) in zip(pair_on, EXPERT_PAIRS)])
    w_hi = add([p * cw[hi] for p, (lo, hi) in zip(pair_on, EXPERT_PAIRS)])
    pair = add([float(k) * p for k, p in enumerate(pair_on) if k > 0])
    segment = gid * float(len(EXPERT_PAIRS)) + pair
    pad = jnp.zeros((ROUTE_ROWS - ROUTE_GROUP_COL - 1,) + smax.shape[1:], F32)
    return jnp.concatenate([w_lo, w_hi, segment] + cw + [gid, pad], axis=0)


def _mix_kernel(x_ref, oa_ref, ob_ref, oc_ref, gl_ref, mk_ref, mv_ref, wb_ref, wout_ref, wq_ref, wo_ref,
                g1_ref, b1_ref, g2_ref, b2_ref, rt_ref, x2_ref, route_ref, *, sub, seq_rows):
    tiles = [slice(s0, s0 + sub) for s0 in range(0, x_ref.shape[0], sub)]
    cd = wb_ref.dtype

    mixed = []
    for rows in tiles:
        acc = None
        for b, o_ref in enumerate((oa_ref, ob_ref, oc_ref)):
            per_branch = _dot(o_ref[rows], wb_ref[b])
            gate = _sigmoid(gl_ref[rows, D_MODEL * b:D_MODEL * (b + 1)].astype(F32))
            acc = gate * per_branch if acc is None else acc + gate * per_branch
        mixed.append(acc.astype(cd))

    x1 = [_layer_norm(DN_ALPHA * x_ref[rows] + _dot(m, wout_ref[...]), g1_ref[...], b1_ref[...])
          for rows, m in zip(tiles, mixed)]
    q = [(_dot(t.astype(cd), wq_ref[...]) * (X_HEAD_DIM ** -0.5)).astype(cd) for t in x1]

    piece = min(sub, seq_rows)
    heads = [[] for _ in tiles]
    for h in range(X_HEADS):
        sl = slice(X_HEAD_DIM * h, X_HEAD_DIM * (h + 1))
        spans = [(t, r0, (t * sub + r0) // seq_rows) for t in range(len(tiles)) for r0 in range(0, sub, piece)]
        scores = [_dot_nt(q[t][r0:r0 + piece, sl], mk_ref[mem, :, sl]) for t, r0, mem in spans]
        outs = [[] for _ in tiles]
        for (t, r0, mem), s in zip(spans, scores):
            m = jnp.max(s, axis=-1, keepdims=True)
            e = jnp.exp(s - m)
            l = jnp.sum(e, axis=-1, keepdims=True)
            outs[t].append((_dot(e.astype(cd), mv_ref[mem, :, sl]) * (1.0 / l)).astype(cd))
        for t, parts in enumerate(outs):
            heads[t].append(parts[0] if len(parts) == 1 else jnp.concatenate(parts, axis=0))

    attn = [_dot(jnp.concatenate(hs, axis=-1), wo_ref[...]) for hs in heads]
    x2 = [_layer_norm(DN_ALPHA * a + b, g2_ref[...], b2_ref[...]) for a, b in zip(x1, attn)]

    r_hi, r_lo = _split2(rt_ref[...])
    r_both = jnp.concatenate([r_hi, r_lo], axis=0)
    eye_r = lax.broadcasted_iota(jnp.int32, (ROUTE_ROWS, ROUTE_LANES), 0)
    eye_c = lax.broadcasted_iota(jnp.int32, (ROUTE_ROWS, ROUTE_LANES), 1)
    eye = jnp.where(eye_r == eye_c, 1.0, 0.0).astype(BF16)
    logits = []
    for rows, t in zip(tiles, x2):
        x2_ref[rows] = t
        x_hi, x_lo = _split2(t)
        by_hi = _dot_nt(r_both, x_hi)
        logits.append(by_hi[N_EXPERTS:] + _dot_nt(r_hi, x_lo) + by_hi[:N_EXPERTS])
    hi, mid, lo = _split3(_route(logits[0] if len(logits) == 1 else jnp.concatenate(logits, axis=1)))
    for rows in tiles:
        route_ref[rows] = _dot_tn(hi[:, rows], eye) + _dot_tn(mid[:, rows], eye) + _dot_tn(lo[:, rows], eye)


def _mix(x, oa, ob, oc, gl, mk, mv, w, batch, seq):
    n_mem = mk.shape[1]
    if seq >= 2 * MIX_TILE:
        sub, tm, per_tile = MIX_TILE, 2 * MIX_TILE, 1
        mem = pl.BlockSpec((1, n_mem, D_MODEL), lambda b, i: (b, 0, 0))
    else:
        assert MIX_TILE % seq == 0 and batch % (MIX_TILE // seq) == 0
        sub, tm, per_tile = MIX_TILE, MIX_TILE, MIX_TILE // seq
        mem = pl.BlockSpec((per_tile, n_mem, D_MODEL), lambda b, i: (b, 0, 0), pipeline_mode=pl.Buffered(1))
    nt = (per_tile * seq) // tm
    row = lambda b, i: (b * nt + i, 0)
    vec = _const_spec((1, D_MODEL))
    sq = _layer_spec((D_MODEL, D_MODEL), w['layer'])
    return pl.pallas_call(
        functools.partial(_mix_kernel, sub=sub, seq_rows=seq),
        grid=(batch // per_tile, nt),
        in_specs=[pl.BlockSpec((tm, D_MODEL), row)] + [pl.BlockSpec((tm, A_WIDTH), row)] * 3
        + [pl.BlockSpec((tm, N_BRANCH * D_MODEL), row), mem, mem,
           _layer_spec((N_BRANCH, A_WIDTH, D_MODEL), w['layer']), sq, sq, sq, vec, vec, vec, vec,
           _const_spec((N_EXPERTS, D_MODEL))],
        out_specs=[pl.BlockSpec((tm, D_MODEL), row), pl.BlockSpec((tm, ROUTE_LANES), row)],
        out_shape=[jax.ShapeDtypeStruct((batch * seq, D_MODEL), F32),
                   jax.ShapeDtypeStruct((batch * seq, ROUTE_LANES), F32)],
        compiler_params=_params("arbitrary", "arbitrary"),
        name="mix",
    )(x, oa, ob, oc, gl, mk, mv, w['w_branch'], w['w_out'], w['xa_wq'], w['xa_wo'],
      w['ln1_g'], w['ln1_b'], w['ln2_g'], w['ln2_b'], w['router_t'])


def _sc_gather_rows(tables, idx):
    n_out = idx.shape[0]
    per_worker = n_out // SC_WORKERS
    window = min(SC_GATHER_WINDOW, per_worker)
    steps = per_worker // window
    assert per_worker * SC_WORKERS == n_out and steps * window == per_worker and window % 8 == 0
    idx3 = idx.astype(jnp.int32).reshape(SC_WORKERS, steps, window)
    mesh = plsc.VectorSubcoreMesh(core_axis_name="core", subcore_axis_name="subcore")
    n_tab = len(tables)

    def body(*refs):
        tab_hbm = refs[:n_tab]
        idx_hbm = refs[n_tab]
        out_hbm = refs[n_tab + 1:2 * n_tab + 1]
        idx_v = refs[2 * n_tab + 1]
        rows_v = refs[2 * n_tab + 2:3 * n_tab + 2]
        sem = refs[3 * n_tab + 2]
        wid = lax.axis_index("subcore") * SC_CORES + lax.axis_index("core")
        pltpu.sync_copy(idx_hbm.at[wid], idx_v)

        @pl.loop(0, steps)
        def _(j):
            base = wid * per_worker + j * window
            for k in range(n_tab):
                pltpu.async_copy(tab_hbm[k].at[idx_v.at[j]], rows_v[k], sem).wait()
                pltpu.sync_copy(rows_v[k], out_hbm[k].at[pl.ds(base, window)])

    call = pl.kernel(
        body,
        out_type=[jax.ShapeDtypeStruct((n_out, t.shape[1]), t.dtype) for t in tables],
        mesh=mesh,
        scratch_types=[pltpu.VMEM((steps, window), jnp.int32)]
        + [pltpu.VMEM((window, t.shape[1]), t.dtype) for t in tables] + [pltpu.SemaphoreType.DMA],
        name="sc_gather",
    )
    return call(*tables, idx3)


def _dispatch_plan(seg, tm, n_seg):
    n = seg.shape[0]
    n_pad = n + n_seg * tm
    ids = jnp.arange(n_seg, dtype=jnp.int32)
    onehot = (seg[:, None] == ids[None, :]).astype(jnp.int32)
    counts = jnp.sum(onehot, axis=0)
    padded = ((counts + tm - 1) // tm) * tm
    start_p = jnp.cumsum(padded) - padded
    start_u = jnp.cumsum(counts) - counts
    order = jnp.argsort(seg, stable=True).astype(jnp.int32)
    rank_sorted = jnp.argsort(order).astype(jnp.int32)
    pos = (rank_sorted + jnp.sum(onehot * (start_p - start_u)[None, :], axis=1)).astype(jnp.int32)
    gap = padded - counts
    gap_end = jnp.cumsum(gap)
    j = jnp.arange(n_pad - n, dtype=jnp.int32)
    region = jnp.sum((j[:, None] >= gap_end[None, :]).astype(jnp.int32), axis=1)
    oh_region = (region[:, None] == jnp.arange(n_seg + 1, dtype=jnp.int32)[None, :]).astype(jnp.int32)
    first_empty = jnp.concatenate([start_p + counts, jnp.sum(padded)[None]])
    region_start = jnp.concatenate([gap_end - gap, gap_end[-1:]])
    empty_slot = j + jnp.sum(oh_region * (first_empty - region_start)[None, :], axis=1)
    _, src = lax.sort_key_val(jnp.concatenate([pos, empty_slot.astype(jnp.int32)]),
                              jnp.concatenate([jnp.arange(n, dtype=jnp.int32), j % n]))
    tile_start = jnp.arange(n_pad // tm, dtype=jnp.int32) * tm
    seg_tile = jnp.minimum(jnp.sum((tile_start[:, None] >= (start_p + padded)[None, :]).astype(jnp.int32), axis=1),
                           n_seg - 1)
    return pos, src, seg_tile, jnp.sum(padded) // tm


def _moe_kernel(ex_ref, x_ref, r_ref, *refs, n_exp, col0):
    w_refs, (g_ref, b_ref, o_ref) = refs[:3 * n_exp], refs[3 * n_exp:]

    @pl.when(pl.program_id(0) < ex_ref[n_exp * pl.num_programs(0)])
    def _():
        x = x_ref[...]
        cd = w_refs[0].dtype
        xb = x.astype(cd)
        r = r_ref[...]
        y = None
        for e in range(n_exp):
            w1_ref, w3_ref, w2_ref = w_refs[3 * e:3 * e + 3]
            h1 = _dot(xb, w1_ref[...])
            h3 = _dot(xb, w3_ref[...])
            ye = r[:, col0 + e:col0 + e + 1] * _dot((h1 * _sigmoid(h1) * h3).astype(cd), w2_ref[...])
            y = ye if y is None else y + ye
        o_ref[...] = _layer_norm(DN_ALPHA * x + y, g_ref[...], b_ref[...])


def _moe_sorted(xs, rs, experts, tiles_used, w, tm, col0):
    n_pad = xs.shape[0]
    n_exp, nt = experts.shape
    layer = w['layer']
    row = lambda i, ex: (i, 0)
    vec = pl.BlockSpec((1, D_MODEL), lambda i, ex: (0, 0))
    mode = dict(pipeline_mode=pl.Buffered(1)) if w['w1'].dtype == F32 else {}
    w_specs, w_args = [], []
    for e in range(n_exp):
        pick = lambda i, ex, e=e: (layer, ex[e * nt + i], 0, 0)
        w_specs += [pl.BlockSpec((None, None, D_MODEL, D_EXPERT), pick, **mode),
                    pl.BlockSpec((None, None, D_MODEL, D_EXPERT), pick, **mode),
                    pl.BlockSpec((None, None, D_EXPERT, D_MODEL), pick, **mode)]
        w_args += [w['w1'], w['w3'], w['w2']]
    return pl.pallas_call(
        functools.partial(_moe_kernel, n_exp=n_exp, col0=col0),
        grid_spec=pltpu.PrefetchScalarGridSpec(
            num_scalar_prefetch=1,
            grid=(nt,),
            in_specs=[pl.BlockSpec((tm, D_MODEL), row), pl.BlockSpec((tm, ROUTE_LANES), row)] + w_specs + [vec, vec],
            out_specs=pl.BlockSpec((tm, D_MODEL), row),
        ),
        out_shape=jax.ShapeDtypeStruct((n_pad, D_MODEL), F32),
        compiler_params=_params("arbitrary"),
        name="moe",
    )(jnp.concatenate([experts.reshape(n_exp * nt), tiles_used.astype(jnp.int32).reshape(1)]),
      xs, rs, *w_args, w['ln3_g'], w['ln3_b'])


def _moe(x2, route, w):
    n = x2.shape[0]
    n_pairs = len(EXPERT_PAIRS)
    by_pair = n >= N_SEGMENTS * MOE_TILE
    if by_pair:
        tm, n_seg, col0 = MOE_TILE, N_SEGMENTS, 0
        seg = route[:, ROUTE_PAIR_COL].astype(jnp.int32)
    else:
        tm, n_seg, col0 = min(MOE_TILE, n // N_GROUPS), N_GROUPS, ROUTE_PAIR_COL + 1
        seg = route[:, ROUTE_GROUP_COL].astype(jnp.int32)
    pos, src, seg_tile, tiles_used = _dispatch_plan(seg, tm, n_seg)
    if by_pair:
        p = seg_tile % n_pairs
        member = lambda side: functools.reduce(
            lambda x, y: x + y, [jnp.where(p == k, pr[side], 0) for k, pr in enumerate(EXPERT_PAIRS)])
        experts = GROUP_SIZE * (seg_tile // n_pairs) + jnp.stack([member(0), member(1)])
    else:
        experts = GROUP_SIZE * seg_tile[None, :] + jnp.arange(GROUP_SIZE, dtype=jnp.int32)[:, None]
    xs, rs = _sc_gather_rows([x2, route], src)
    ys = _moe_sorted(xs, rs, experts.astype(jnp.int32), tiles_used, w, tm, col0)
    return _sc_gather_rows([ys], pos)[0]


def _trunk_layer(x, w, mem_k, mem_v, cache_k, cache_v, s0, h0, conv0, batch, seq, prompt):
    qkv, hg, rg, gl = _inproj(x, w['w_in'], w['layer'])
    keep = min(A_WINDOW, seq)
    x_keep = x.reshape(batch, seq, D_MODEL)[:, seq - keep:].reshape(batch * keep, D_MODEL)
    k32, v32 = [t.reshape(batch, keep, A_HEADS, A_HEAD_DIM) for t in _kv_rows(x_keep, w['w_in'], w['layer'])]
    if prompt:
        oa = _attn_prompt(qkv, w['band_bias'], batch, seq)
        new_k, new_v = k32, v32
        tile = 256
    else:
        win = cache_k.shape[1]
        ck = cache_k.reshape(batch, win, A_WIDTH)
        cv = cache_v.reshape(batch, win, A_WIDTH)
        oa = _attn_sample(qkv, ck, cv, w['band_bias'][:, :seq, :win + 2 * CHUNK], batch, seq)
        shift = lambda cache, new: (jnp.concatenate([cache[:, seq:], new], axis=1) if seq <= win
                                    else new[:, seq - win:])
        new_k, new_v = shift(cache_k, k32), shift(cache_v, v32)
        tile = seq
    ob, s_bd = _hgrn(hg, w['lb'], w['hgrn_g'], _state_to_block_diag(s0), batch, seq, tile, w['w_in'].dtype)
    conv0_pad = jnp.concatenate([jnp.zeros((batch, 8 - (C_CONV - 1), C_WIDTH), F32), conv0.astype(F32)], axis=1)
    oc, h_new = _rglru(rg, conv0_pad, h0.astype(F32).reshape(batch, 1, C_WIDTH), w, batch, seq, tile, prompt)
    n_tail = min(seq, C_CONV - 1)
    xr_tail = rg.reshape(batch, seq, 2 * C_WIDTH)[:, seq - n_tail:, :C_WIDTH]
    conv_new = jnp.concatenate([conv0.astype(F32), xr_tail], axis=1)[:, n_tail:]
    x2, route = _mix(x, oa, ob, oc, gl, mem_k, mem_v, w, batch, seq)
    x3 = _moe(x2, route, w)
    return x3, (new_k, new_v, _block_diag_to_state(s_bd), h_new.reshape(batch, C_WIDTH), conv_new)


def kernel(x_prompt, x_sample, cache_attn_k, cache_attn_v, state_hgrn, state_rglru, state_conv, cache_mem_k, cache_mem_v, mem_prompt, w_in, attn_rel_bias, hgrn_lb_logits, hgrn_norm_g, rg_conv_w, rg_conv_b, rg_wa, rg_ba, rg_wx, rg_bx, rg_lambda, w_branch, w_out, ln1_g, ln1_b, xa_wq, xa_wk, xa_wv, xa_wo, ln2_g, ln2_b, moe_router, moe_w1, moe_w3, moe_w2, ln3_g, ln3_b):
    bp, tp, _ = x_prompt.shape
    bs, ts, _ = x_sample.shape
    n_mem = mem_prompt.shape[1]
    depth = w_in.shape[0]

    p = jax.nn.softmax(hgrn_lb_logits.astype(F32), axis=0)
    lb_all = jnp.cumsum(p, axis=0) - p[0:1]
    vec = lambda t: t.astype(F32).reshape(1, -1)

    xp = x_prompt.reshape(bp * tp, D_MODEL)
    xs = x_sample.reshape(bs * ts, D_MODEL)
    mem2d = mem_prompt.reshape(bp * n_mem, D_MODEL)
    s0p = jnp.zeros((bp, B_HEADS, B_KEY_DIM, B_VAL_DIM), F32)
    h0p = jnp.zeros((bp, C_WIDTH), F32)
    conv0p = jnp.zeros((bp, C_CONV - 1, C_WIDTH), F32)
    router_t = moe_router.astype(F32).T

    stacked = {'w_in': w_in, 'w_branch': w_branch, 'w_out': w_out, 'xa_wq': xa_wq, 'xa_wo': xa_wo,
               'w1': moe_w1, 'w3': moe_w3, 'w2': moe_w2}
    stacked_fast = {k: v.astype(BF16) for k, v in stacked.items()}
    stacked_precise = {k: v.astype(F32) for k, v in stacked.items()}
    wk_b, wv_b = xa_wk.astype(BF16), xa_wv.astype(BF16)

    outs_p, outs_s, mem_ks, mem_vs = [], [], [], []
    for l in range(depth):
        wf = {'wa_bd': _block_diag_weight(rg_wa[l]), 'wx_bd': _block_diag_weight(rg_wx[l])}
        shared = {
            'layer': l, 'band_bias': _band_bias(attn_rel_bias[l]),
            'lb': vec(lb_all[l]), 'hgrn_g': vec(jnp.tile(hgrn_norm_g[l], B_HEADS)),
            'conv_w': rg_conv_w[l].astype(F32), 'conv_b': vec(rg_conv_b[l]),
            'ba': vec(rg_ba[l]), 'bx': vec(rg_bx[l]), 'lam': vec(rg_lambda[l]),
            'ln1_g': vec(ln1_g[l]), 'ln1_b': vec(ln1_b[l]), 'ln2_g': vec(ln2_g[l]), 'ln2_b': vec(ln2_b[l]),
            'ln3_g': vec(ln3_g[l]), 'ln3_b': vec(ln3_b[l]), 'router_t': router_t,
        }
        w_fast = dict(shared, **stacked_fast, **{k: v.astype(BF16) for k, v in wf.items()})
        w_precise = dict(shared, **stacked_precise, **{k: v.astype(F32) for k, v in wf.items()})
        mk_p = _matmul(mem2d, wk_b, l)
        mv_p = _matmul(mem2d, wv_b, l)
        mem_ks.append(mk_p.reshape(bp, n_mem, X_HEADS, X_HEAD_DIM))
        mem_vs.append(mv_p.reshape(bp, n_mem, X_HEADS, X_HEAD_DIM))
        xp, st_p = _trunk_layer(xp, w_fast, mk_p.reshape(bp, n_mem, D_MODEL).astype(BF16),
                                mv_p.reshape(bp, n_mem, D_MODEL).astype(BF16),
                                None, None, s0p, h0p, conv0p, bp, tp, True)
        xs, st_s = _trunk_layer(xs, w_precise, cache_mem_k[l].reshape(bs, n_mem, D_MODEL).astype(F32),
                                cache_mem_v[l].reshape(bs, n_mem, D_MODEL).astype(F32),
                                cache_attn_k[l], cache_attn_v[l], state_hgrn[l], state_rglru[l], state_conv[l],
                                bs, ts, False)
        outs_p.append(st_p)
        outs_s.append(st_s)

    stack = lambda items, j: jnp.stack([it[j] for it in items])
    return (xp.reshape(bp, tp, D_MODEL), xs.reshape(bs, ts, D_MODEL),
            stack(outs_p, 0), stack(outs_p, 1), stack(outs_p, 2), stack(outs_p, 3), stack(outs_p, 4),
            jnp.stack(mem_ks), jnp.stack(mem_vs),
            stack(outs_s, 0), stack(outs_s, 1), stack(outs_s, 2), stack(outs_s, 3), stack(outs_s, 4))
```

```python
import functools

import numpy as np
import jax
import jax.numpy as jnp
from jax import lax
from jax.experimental import pallas as pl
from jax.experimental.pallas import tpu as pltpu
from jax.experimental.pallas import tpu_sc as plsc

F32 = jnp.float32
BF16 = jnp.bfloat16

D_MODEL = 1024
CHUNK = 64
A_HEADS = 8
A_HEAD_DIM = 64
A_WIDTH = A_HEADS * A_HEAD_DIM
A_PAST_CHUNKS = 8
A_WINDOW = A_PAST_CHUNKS * CHUNK
A_MAX_REL = 256
B_HEADS = 8
B_KEY_DIM = 64
B_VAL_DIM = 64
B_WIDTH = B_HEADS * B_KEY_DIM
C_WIDTH = 512
C_CONV = 4
C_GATE_C = 8.0
N_BRANCH = 3
IN_COLS = 3 * A_WIDTH + 4 * B_WIDTH + 2 * C_WIDTH + N_BRANCH * D_MODEL
X_HEADS = 4
X_HEAD_DIM = D_MODEL // X_HEADS
N_EXPERTS = 16
N_GROUPS = 4
GROUP_SIZE = N_EXPERTS // N_GROUPS
D_EXPERT = D_MODEL // 2
DEPTH = 4
DN_ALPHA = (2 * DEPTH) ** 0.25
LN_EPS = 1e-5
RMS_EPS = 1e-6
NEG_INF = -1e30

VMEM_LIMIT_BYTES = 56 * 1024 * 1024

INPROJ_TILE = 512
MIX_TILE = 256
ATTN_Q_TILE = 256
ATTN_K_TILE = 256
HG_GROUP = 256
RG_SCAN_GROUP = 8
HG_SEQS_PER_STEP = 2
HG_HALF = CHUNK // 2
HG_QUARTER = CHUNK // 4
EXPERT_PAIRS = tuple((lo, hi) for lo in range(GROUP_SIZE) for hi in range(lo + 1, GROUP_SIZE))
N_SEGMENTS = N_GROUPS * len(EXPERT_PAIRS)
ROUTE_ROWS = 16
ROUTE_PAIR_COL = 2
ROUTE_GROUP_COL = 7
ROUTE_LANES = 128
MOE_TILE = 512
SC_CORES = 2
SC_SUBCORES = 16
SC_WORKERS = SC_CORES * SC_SUBCORES
SC_GATHER_WINDOW = 64


def _split2(x):
    hi = x.astype(BF16)
    lo = (x - hi.astype(F32)).astype(BF16)
    return hi, lo


def _contract(a, b, dims):
    dg = lambda u, v: lax.dot_general(u, v, (dims, ((), ())), preferred_element_type=F32)
    if a.dtype == F32:
        a_hi, a_lo = _split2(a)
        b_hi, b_lo = _split2(b)
        return dg(a_lo, b_hi) + dg(a_hi, b_lo) + dg(a_hi, b_hi)
    return dg(a, b)


def _dot(a, b):
    return _contract(a, b, ((1,), (0,)))


def _dot_nt(a, b):
    return _contract(a, b, ((1,), (1,)))


def _dot_tn(a, b):
    return _contract(a, b, ((0,), (0,)))


def _split3(x):
    hi = x.astype(BF16)
    r1 = x - hi.astype(F32)
    mid = r1.astype(BF16)
    lo = (r1 - mid.astype(F32)).astype(BF16)
    return hi, mid, lo


def _sigmoid(x):
    return 0.5 * jnp.tanh(0.5 * x) + 0.5


def _layer_norm(x, g, b):
    mu = jnp.mean(x, axis=-1, keepdims=True)
    xc = x - mu
    var = jnp.mean(xc * xc, axis=-1, keepdims=True)
    return xc * lax.rsqrt(var + LN_EPS) * g + b


def _params(*semantics):
    return pltpu.CompilerParams(dimension_semantics=semantics, vmem_limit_bytes=VMEM_LIMIT_BYTES)


def _const_spec(shape):
    nd = len(shape)
    return pl.BlockSpec(shape, lambda *_: (0,) * nd, pipeline_mode=pl.Buffered(1))


def _layer_spec(shape, layer):
    nd = len(shape)
    return pl.BlockSpec((None,) + tuple(shape), lambda *_: (layer,) + (0,) * nd, pipeline_mode=pl.Buffered(1))


def _inproj_kernel(x_ref, w_ref, qkv_ref, hg_ref, rg_ref, gl_ref):
    xb = x_ref[...].astype(w_ref.dtype)
    cw = 512

    def mm(c0):
        return _dot(xb, w_ref[:, c0:c0 + cw])

    for j in range(3):
        qkv_ref[:, cw * j:cw * (j + 1)] = mm(cw * j).astype(qkv_ref.dtype)
    base = 3 * A_WIDTH
    for j in range(4):
        hg_ref[:, cw * j:cw * (j + 1)] = mm(base + cw * j)
    base += 4 * B_WIDTH
    for j in range(2):
        rg_ref[:, cw * j:cw * (j + 1)] = mm(base + cw * j)
    base += 2 * C_WIDTH
    for j in range(N_BRANCH * D_MODEL // cw):
        gl_ref[:, cw * j:cw * (j + 1)] = mm(base + cw * j).astype(gl_ref.dtype)


def _inproj(x, w_in, layer):
    n = x.shape[0]
    cd = w_in.dtype
    tm = INPROJ_TILE if cd == BF16 else INPROJ_TILE // 2
    assert n % tm == 0
    row = lambda i: (i, 0)
    return pl.pallas_call(
        _inproj_kernel,
        grid=(n // tm,),
        in_specs=[pl.BlockSpec((tm, D_MODEL), row), _layer_spec((D_MODEL, IN_COLS), layer)],
        out_specs=[pl.BlockSpec((tm, 3 * A_WIDTH), row),
                   pl.BlockSpec((tm, 4 * B_WIDTH), row), pl.BlockSpec((tm, 2 * C_WIDTH), row),
                   pl.BlockSpec((tm, N_BRANCH * D_MODEL), row)],
        out_shape=[jax.ShapeDtypeStruct((n, 3 * A_WIDTH), cd),
                   jax.ShapeDtypeStruct((n, 4 * B_WIDTH), F32), jax.ShapeDtypeStruct((n, 2 * C_WIDTH), F32),
                   jax.ShapeDtypeStruct((n, N_BRANCH * D_MODEL), cd)],
        compiler_params=_params("arbitrary"),
        name="inproj",
    )(x, w_in)


def _matmul_kernel(x_ref, w_ref, o_ref):
    o_ref[...] = _dot(x_ref[...].astype(w_ref.dtype), w_ref[...])


def _matmul(x, w, layer):
    n, k = x.shape
    m = w.shape[2]
    tm = 256
    return pl.pallas_call(
        _matmul_kernel,
        grid=(n // tm,),
        in_specs=[pl.BlockSpec((tm, k), lambda i: (i, 0)), _layer_spec((k, m), layer)],
        out_specs=pl.BlockSpec((tm, m), lambda i: (i, 0)),
        out_shape=jax.ShapeDtypeStruct((n, m), F32),
        compiler_params=_params("arbitrary"),
        name="matmul",
    )(x, w)


def _kv_kernel(x_ref, wk_ref, wv_ref, k_ref, v_ref):
    xb = x_ref[...].astype(wk_ref.dtype)
    k_ref[...] = _dot(xb, wk_ref[...])
    v_ref[...] = _dot(xb, wv_ref[...])


def _kv_rows(x, w_in, layer):
    n = x.shape[0]
    tm = 256
    col = lambda j: pl.BlockSpec((None, D_MODEL, A_WIDTH), lambda i: (layer, 0, j), pipeline_mode=pl.Buffered(1))
    out = pl.BlockSpec((tm, A_WIDTH), lambda i: (i, 0))
    return pl.pallas_call(
        _kv_kernel,
        grid=(n // tm,),
        in_specs=[pl.BlockSpec((tm, D_MODEL), lambda i: (i, 0)), col(1), col(2)],
        out_specs=[out, out],
        out_shape=[jax.ShapeDtypeStruct((n, A_WIDTH), F32)] * 2,
        compiler_params=_params("arbitrary"),
        name="kv_rows",
    )(x, w_in, w_in)


def _attn_core(q, k, v, bias_ref, valid):
    rows = q.shape[0]
    lane = lax.broadcasted_iota(jnp.int32, (rows, 2 * A_HEAD_DIM), 1)
    first = lane < A_HEAD_DIM
    q = q * jnp.asarray(A_HEAD_DIM ** -0.5, q.dtype)

    def scores(head):
        sl = slice(2 * A_HEAD_DIM * (head // 2), 2 * A_HEAD_DIM * (head // 2 + 1))
        sel = first if head % 2 == 0 else jnp.logical_not(first)
        qm = jnp.where(sel, q[:, sl], jnp.zeros_like(q[:, sl]))
        s = _dot_nt(qm, k[:, sl]) + bias_ref[head]
        return s if valid is None else jnp.where(valid, s, NEG_INF)

    outs = []
    pair = None
    s_next = scores(0)
    for head in range(A_HEADS):
        s = s_next
        if head + 1 < A_HEADS:
            s_next = scores(head + 1)
        m = jnp.max(s, axis=-1, keepdims=True)
        e = jnp.exp(s - m)
        l = jnp.sum(e, axis=-1, keepdims=True)
        sl = slice(2 * A_HEAD_DIM * (head // 2), 2 * A_HEAD_DIM * (head // 2 + 1))
        o = _dot(e.astype(v.dtype), v[:, sl]) * (1.0 / l)
        if head % 2 == 0:
            pair = o
        else:
            outs.append(jnp.where(first, pair, o))
    return jnp.concatenate(outs, axis=-1)


def _attn_prompt_kernel(q_ref, k0_ref, k1_ref, k2_ref, v0_ref, v1_ref, v2_ref, bias_ref, o_ref):
    i = pl.program_id(1)

    def run(masked):
        k = jnp.concatenate([k0_ref[...], k1_ref[...], k2_ref[...]], axis=0)
        v = jnp.concatenate([v0_ref[...], v1_ref[...], v2_ref[...]], axis=0)
        valid = None
        if masked:
            col = lax.broadcasted_iota(jnp.int32, (ATTN_Q_TILE, 3 * ATTN_K_TILE), 1)
            valid = col >= (2 - i) * ATTN_K_TILE
        o_ref[...] = _attn_core(q_ref[...], k, v, bias_ref, valid).astype(o_ref.dtype)

    pl.when(i < 2)(lambda: run(True))
    pl.when(i >= 2)(lambda: run(False))


def _attn_prompt(qkv, bias, batch, seq):
    nt = seq // ATTN_Q_TILE
    blk = (ATTN_Q_TILE, A_WIDTH)

    def kv_spec(j, col):
        return pl.BlockSpec(blk, lambda b, i: (b * nt + jnp.maximum(i - 2 + j, 0), col))

    return pl.pallas_call(
        _attn_prompt_kernel,
        grid=(batch, nt),
        in_specs=[pl.BlockSpec(blk, lambda b, i: (b * nt + i, 0))]
        + [kv_spec(j, 1) for j in range(3)] + [kv_spec(j, 2) for j in range(3)]
        + [_const_spec(bias.shape)],
        out_specs=pl.BlockSpec(blk, lambda b, i: (b * nt + i, 0)),
        out_shape=jax.ShapeDtypeStruct((batch * seq, A_WIDTH), qkv.dtype),
        compiler_params=_params("arbitrary", "arbitrary"),
        name="attn_prompt",
    )(qkv, qkv, qkv, qkv, qkv, qkv, qkv, bias)


def _attn_sample_kernel(q_ref, kn_ref, vn_ref, ck_ref, cv_ref, bias_ref, o_ref):
    cd = q_ref.dtype
    pad = jnp.zeros((CHUNK, A_WIDTH), cd)
    k = jnp.concatenate([ck_ref[0].astype(cd), kn_ref[...], pad], axis=0)
    v = jnp.concatenate([cv_ref[0].astype(cd), vn_ref[...], pad], axis=0)
    o_ref[...] = _attn_core(q_ref[...], k, v, bias_ref, None).astype(cd)


def _attn_sample(qkv, cache_k, cache_v, bias, batch, seq):
    win = cache_k.shape[1]
    blk = (seq, A_WIDTH)
    return pl.pallas_call(
        _attn_sample_kernel,
        grid=(batch,),
        in_specs=[pl.BlockSpec(blk, lambda b: (b, 0)), pl.BlockSpec(blk, lambda b: (b, 1)),
                  pl.BlockSpec(blk, lambda b: (b, 2)),
                  pl.BlockSpec((1, win, A_WIDTH), lambda b: (b, 0, 0)),
                  pl.BlockSpec((1, win, A_WIDTH), lambda b: (b, 0, 0)),
                  _const_spec(bias.shape)],
        out_specs=pl.BlockSpec(blk, lambda b: (b, 0)),
        out_shape=jax.ShapeDtypeStruct((batch * seq, A_WIDTH), qkv.dtype),
        compiler_params=_params("arbitrary"),
        name="attn_sample",
    )(qkv, qkv, qkv, cache_k, cache_v, bias)


def _band_bias(table):
    heads = table.shape[0]
    n_keys = 3 * ATTN_K_TILE
    span = ATTN_Q_TILE + n_keys
    r = np.arange(ATTN_Q_TILE)[:, None]
    j = np.arange(n_keys)[None, :]
    band = (j // CHUNK >= r // CHUNK) & (j // CHUNK <= r // CHUNK + A_PAST_CHUNKS)
    t = table.astype(F32)
    u = jnp.concatenate([t, jnp.broadcast_to(t[:, -1:], (heads, span - t.shape[1]))], axis=1)
    w = u[:, ::-1]
    tiled = jnp.broadcast_to(w[:, None, :], (heads, ATTN_Q_TILE, span)).reshape(heads, ATTN_Q_TILE * span)
    view = tiled[:, :ATTN_Q_TILE * (span - 1)].reshape(heads, ATTN_Q_TILE, span - 1)
    bias = view[:, :, ATTN_Q_TILE - 1:ATTN_Q_TILE - 1 + n_keys]
    return jnp.where(band[None], bias, NEG_INF)


def _head_block_diag(x, head_masks):
    zero = jnp.zeros_like(x)
    return jnp.concatenate([jnp.where(m, x, zero) for m in head_masks], axis=0)


def _hgrn_kernel(hg_ref, lb_ref, ng_ref, s0_ref, o_ref, sfin_ref, st_ref, ot_ref, *, tile):
    i = pl.program_id(1)

    @pl.when(i == 0)
    def _():
        st_ref[...] = s0_ref[...]

    for bb in range(hg_ref.shape[0]):
        _hgrn_rows(hg_ref.at[bb], lb_ref, ng_ref, o_ref.at[bb], st_ref.at[bb], ot_ref.at[bb], tile)

    @pl.when(i == pl.num_programs(1) - 1)
    def _():
        sfin_ref[...] = st_ref[...]


def _hgrn_rows(hg_ref, lb_ref, ng_ref, o_ref, st_ref, ot_ref, tile):
    n_groups = B_WIDTH // HG_GROUP
    cd = o_ref.dtype
    q = hg_ref[:, 0:B_WIDTH]
    f_logit = hg_ref[:, B_WIDTH:2 * B_WIDTH]
    v_in = hg_ref[:, 2 * B_WIDTH:3 * B_WIDTH].astype(cd)
    lb = lb_ref[...]
    f = lb + (1.0 - lb) * (1.0 / (1.0 + jnp.exp(-f_logit)))
    log_f = jnp.log(f)
    kk = 1.0 - f

    r_t = lax.broadcasted_iota(jnp.int32, (tile, tile), 0)
    c_t = lax.broadcasted_iota(jnp.int32, (tile, tile), 1)
    tri = jnp.where((r_t // CHUNK == c_t // CHUNK) & (c_t <= r_t), 1.0, 0.0).astype(BF16)
    split = _split3 if cd == F32 else _split2
    g_all = functools.reduce(lambda x, y: x + y, [_dot(tri, part) for part in reversed(split(log_f))])

    row = lax.broadcasted_iota(jnp.int32, (CHUNK, B_WIDTH), 0)
    upper = row >= HG_HALF
    lane_g = lax.broadcasted_iota(jnp.int32, (CHUNK, HG_GROUP), 1)
    row_g = lax.broadcasted_iota(jnp.int32, (CHUNK, HG_GROUP), 0)
    head_masks = [lane_g // B_KEY_DIM == h for h in range(HG_GROUP // B_KEY_DIM)]
    causal = (lane_g % CHUNK) <= row_g
    cross = (row_g >= HG_HALF) & ((lane_g % CHUNK) < HG_HALF)
    r_bd = lax.broadcasted_iota(jnp.int32, (HG_GROUP, HG_GROUP), 0)
    c_bd = lax.broadcasted_iota(jnp.int32, (HG_GROUP, HG_GROUP), 1)
    diag_blocks = (r_bd // B_VAL_DIM) == (c_bd // B_KEY_DIM)

    pending = []
    for c in range(tile // CHUNK):
        rs = slice(CHUNK * c, CHUNK * (c + 1))
        g = g_all[rs]
        qc = q[rs]
        kc = kk[rs]
        vc = v_in[rs]
        g_q1 = g[HG_QUARTER - 1:HG_QUARTER]
        g_mid = g[HG_HALF - 1:HG_HALF]
        g_q3 = g[HG_HALF + HG_QUARTER - 1:HG_HALF + HG_QUARTER]
        g_last = g[CHUNK - 1:CHUNK]
        d_diag = g - jnp.where(upper, g_q3, g_q1)
        d_off = jnp.where(upper, g - g_mid, g_mid - g)
        q_in = (qc * jnp.exp(g)).astype(cd)
        k_st = (kc * jnp.exp(g_last - g)).astype(cd)
        q_diag = qc * jnp.exp(d_diag)
        k_diag = kc * jnp.exp(-d_diag)
        e_off = jnp.exp(d_off)
        zero = jnp.zeros_like(qc)
        q_d = q_diag.astype(cd)
        k_d = k_diag.astype(cd)
        q_x = jnp.where(upper, qc * e_off, zero).astype(cd)
        k_x = jnp.where(upper, zero, kc * e_off).astype(cd)
        decay = jnp.exp(g_last)
        for gi in range(n_groups):
            cs = slice(HG_GROUP * gi, HG_GROUP * (gi + 1))
            att_d = _dot_nt(q_d[:, cs], _head_block_diag(k_d[:, cs], head_masks))
            att_x = _dot_nt(q_x[:, cs], _head_block_diag(k_x[:, cs], head_masks))
            att = jnp.where(cross, att_x, att_d)
            att = jnp.where(causal, att, 0.0).astype(cd)
            v_bd = _head_block_diag(vc[:, cs], head_masks)
            upd = jnp.where(diag_blocks, _dot_tn(vc[:, cs], k_st[:, cs]), 0.0)
            pending.append((rs, cs, gi, q_in[:, cs], _dot(att, v_bd), decay[:, cs], upd))

    states = [st_ref[gi] for gi in range(n_groups)]
    before = []
    for rs, cs, gi, q_in_g, o_intra, decay_g, upd in pending:
        before.append(states[gi].astype(cd))
        states[gi] = states[gi] * decay_g + upd
    for gi in range(n_groups):
        st_ref[gi] = states[gi]
    for (rs, cs, gi, q_in_g, o_intra, decay_g, upd), st_b in zip(pending, before):
        ot_ref[rs, cs] = _dot_nt(q_in_g, st_b) + o_intra

    o = ot_ref[...]
    lane_i = lax.broadcasted_iota(jnp.int32, (B_WIDTH, B_WIDTH), 0)
    lane_j = lax.broadcasted_iota(jnp.int32, (B_WIDTH, B_WIDTH), 1)
    head_ones = jnp.where(lane_i // B_VAL_DIM == lane_j // B_VAL_DIM, 1.0, 0.0).astype(BF16)
    sq_parts = split(o * o)
    sums = _dot(jnp.concatenate(sq_parts, axis=0), head_ones)
    ms = functools.reduce(lambda x, y: x + y, [sums[tile * j:tile * (j + 1)] for j in reversed(range(len(sq_parts)))])
    ms = ms * (1.0 / B_VAL_DIM)
    gate = hg_ref[:, 3 * B_WIDTH:4 * B_WIDTH]
    out = o * lax.rsqrt(ms + RMS_EPS) * ng_ref[...] * (gate * _sigmoid(gate))
    o_ref[...] = out.astype(cd)


def _hgrn(hg, lb, norm_g, s0_bd, batch, seq, tile, cd):
    nt = seq // tile
    n_groups = B_WIDTH // HG_GROUP
    par = HG_SEQS_PER_STEP
    assert batch % par == 0
    st_blk = (par, n_groups, HG_GROUP, HG_GROUP)
    ob, s_fin = pl.pallas_call(
        functools.partial(_hgrn_kernel, tile=tile),
        grid=(batch // par, nt),
        in_specs=[pl.BlockSpec((par, tile, 4 * B_WIDTH), lambda b, i: (b, i, 0)),
                  _const_spec((1, B_WIDTH)), _const_spec((1, B_WIDTH)),
                  pl.BlockSpec(st_blk, lambda b, i: (b, 0, 0, 0))],
        out_specs=[pl.BlockSpec((par, tile, B_WIDTH), lambda b, i: (b, i, 0)),
                   pl.BlockSpec(st_blk, lambda b, i: (b, 0, 0, 0))],
        out_shape=[jax.ShapeDtypeStruct((batch, seq, B_WIDTH), cd),
                   jax.ShapeDtypeStruct((batch,) + st_blk[1:], F32)],
        scratch_shapes=[pltpu.VMEM(st_blk, F32), pltpu.VMEM((par, tile, B_WIDTH), F32)],
        compiler_params=_params("arbitrary", "arbitrary"),
        name="hgrn",
    )(hg.reshape(batch, seq, 4 * B_WIDTH), lb, norm_g, s0_bd)
    return ob.reshape(batch * seq, B_WIDTH), s_fin


def _state_to_block_diag(s):
    b = s.shape[0]
    hpg = HG_GROUP // B_KEY_DIM
    st = s.astype(F32).reshape(b, B_HEADS // hpg, hpg, B_KEY_DIM, B_VAL_DIM).transpose(0, 1, 2, 4, 3)
    bd = jnp.einsum('bghvc,hk->bghvkc', st, jnp.eye(hpg, dtype=F32))
    return bd.reshape(b, B_HEADS // hpg, HG_GROUP, HG_GROUP)


def _block_diag_to_state(bd):
    b = bd.shape[0]
    hpg = HG_GROUP // B_KEY_DIM
    x = bd.reshape(b, B_HEADS // hpg, hpg, B_VAL_DIM, hpg, B_KEY_DIM)
    st = jnp.einsum('bghvkc,hk->bghvc', x, jnp.eye(hpg, dtype=F32))
    return st.transpose(0, 1, 2, 4, 3).reshape(b, B_HEADS, B_KEY_DIM, B_VAL_DIM)


def _rglru_kernel(rg_ref, conv0_ref, h0_ref, cw_ref, cb_ref, wa_ref, ba_ref, wx_ref, bx_ref, lam_ref,
                  o_ref, hlast_ref, xbuf_ref, hc_ref, *, tile, at_start):
    i = pl.program_id(1)
    pad = 8

    @pl.when(i == 0)
    def _():
        xbuf_ref[0:pad] = conv0_ref[0]
        hc_ref[...] = h0_ref[0]

    xr = rg_ref[:, 0:C_WIDTH]
    gate = rg_ref[:, C_WIDTH:2 * C_WIDTH]
    xbuf_ref[pad:pad + tile] = xr
    xc = cb_ref[...] + cw_ref[C_CONV - 1:C_CONV] * xr
    for j in range(1, C_CONV):
        xc = xc + cw_ref[C_CONV - 1 - j:C_CONV - j] * xbuf_ref[pad - j:pad - j + tile]
    xbuf_ref[0:pad] = xbuf_ref[tile:tile + pad]

    xcb = xc.astype(wa_ref.dtype)
    r = _sigmoid(_dot(xcb, wa_ref[...]) + ba_ref[...])
    ig = _sigmoid(_dot(xcb, wx_ref[...]) + bx_ref[...])
    neg_lam = -lam_ref[...]
    softplus = jnp.maximum(neg_lam, 0.0) + jnp.log(1.0 + jnp.exp(-jnp.abs(neg_lam)))
    a = jnp.exp(r * (-C_GATE_C * softplus))
    mult = jnp.sqrt(1.0 - a * a)
    row = lax.broadcasted_iota(jnp.int32, (tile, C_WIDTH), 0)
    if at_start:
        mult = jnp.where((row == 0) & (i == 0), 1.0, mult)
    b = mult * ig * xc

    in_group = row % RG_SCAN_GROUP
    d = 1
    while d < RG_SCAN_GROUP:
        a_sh = pltpu.roll(a, d, 0)
        b_sh = pltpu.roll(b, d, 0)
        keep = in_group >= d
        b = jnp.where(keep, a * b_sh + b, b)
        a = jnp.where(keep, a * a_sh, a)
        d *= 2
    carry = hc_ref[...]
    groups = []
    for j in range(tile // RG_SCAN_GROUP):
        rs = slice(RG_SCAN_GROUP * j, RG_SCAN_GROUP * (j + 1))
        hj = a[rs] * carry + b[rs]
        carry = hj[RG_SCAN_GROUP - 1:RG_SCAN_GROUP]
        groups.append(hj)
    h = jnp.concatenate(groups, axis=0)
    hc_ref[...] = h[tile - 1:tile]
    hlast_ref[0] = h[tile - 1:tile]
    gelu = 0.5 * gate * (1.0 + jnp.tanh(np.sqrt(2.0 / np.pi).astype(np.float32) * (gate + 0.044715 * gate * gate * gate)))
    o_ref[...] = (h * gelu).astype(o_ref.dtype)


def _rglru(rg, conv0_pad, h0, w, batch, seq, tile, at_start):
    nt = seq // tile
    vec = _const_spec((1, C_WIDTH))
    return pl.pallas_call(
        functools.partial(_rglru_kernel, tile=tile, at_start=at_start),
        grid=(batch, nt),
        in_specs=[pl.BlockSpec((tile, 2 * C_WIDTH), lambda b, i: (b * nt + i, 0)),
                  pl.BlockSpec((1, 8, C_WIDTH), lambda b, i: (b, 0, 0)),
                  pl.BlockSpec((1, 1, C_WIDTH), lambda b, i: (b, 0, 0)),
                  _const_spec((C_CONV, C_WIDTH)), vec,
                  _const_spec((C_WIDTH, C_WIDTH)), vec, _const_spec((C_WIDTH, C_WIDTH)), vec, vec],
        out_specs=[pl.BlockSpec((tile, C_WIDTH), lambda b, i: (b * nt + i, 0)),
                   pl.BlockSpec((1, 1, C_WIDTH), lambda b, i: (b, 0, 0))],
        out_shape=[jax.ShapeDtypeStruct((batch * seq, C_WIDTH), w['wa_bd'].dtype),
                   jax.ShapeDtypeStruct((batch, 1, C_WIDTH), F32)],
        scratch_shapes=[pltpu.VMEM((tile + 8, C_WIDTH), F32), pltpu.VMEM((1, C_WIDTH), F32)],
        compiler_params=_params("arbitrary", "arbitrary"),
        name="rglru",
    )(rg, conv0_pad, h0, w['conv_w'], w['conv_b'], w['wa_bd'], w['ba'], w['wx_bd'], w['bx'], w['lam'])


def _block_diag_weight(w):
    n, d, e = w.shape
    return jnp.einsum('nde,nm->ndme', w, jnp.eye(n, dtype=w.dtype)).reshape(n * d, n * e)


def _route(logits_t):
    m = jnp.max(logits_t, axis=0, keepdims=True)
    e = jnp.exp(logits_t - m)
    p = e / jnp.sum(e, axis=0, keepdims=True)
    rows = [p[j:j + 1] for j in range(N_EXPERTS)]
    scores = []
    for g in range(N_GROUPS):
        mem = rows[GROUP_SIZE * g:GROUP_SIZE * (g + 1)]
        best = None
        for a in range(GROUP_SIZE):
            for b in range(a + 1, GROUP_SIZE):
                pair = mem[a] + mem[b]
                best = pair if best is None else jnp.maximum(best, pair)
        scores.append(best)
    smax = functools.reduce(jnp.maximum, scores)
    taken = jnp.zeros_like(smax)
    sel = []
    for g in range(N_GROUPS):
        hit = jnp.where(scores[g] == smax, 1.0, 0.0) * (1.0 - taken)
        taken = taken + hit
        sel.append(hit)
    picked = []
    for j in range(N_EXPERTS):
        g = j // GROUP_SIZE
        rank = jnp.zeros_like(smax)
        for o in range(GROUP_SIZE * g, GROUP_SIZE * (g + 1)):
            if o == j:
                continue
            ahead = (rows[o] >= rows[j]) if o < j else (rows[o] > rows[j])
            rank = rank + jnp.where(ahead, 1.0, 0.0)
        picked.append(sel[g] * jnp.where(rank < float(2), 1.0, 0.0))
    denom = functools.reduce(lambda x, y: x + y, [picked[j] * rows[j] for j in range(N_EXPERTS)])
    comb = [picked[j] * rows[j] / denom for j in range(N_EXPERTS)]
    add = lambda items: functools.reduce(lambda x, y: x + y, items)
    cw = [add([sel[g] * comb[GROUP_SIZE * g + m] for g in range(N_GROUPS)]) for m in range(GROUP_SIZE)]
    on = [add([sel[g] * picked[GROUP_SIZE * g + m] for g in range(N_GROUPS)]) for m in range(GROUP_SIZE)]
    gid = add([float(g) * sel[g] for g in range(1, N_GROUPS)])
    pair_on = [on[lo] * on[hi] for lo, hi in EXPERT_PAIRS]
    w_lo = add([p * cw[lo] for p, (lo, hi) in zip(pair_on, EXPERT_PAIRS)])
    w_hi = add([p * cw[hi] for p, (lo, hi) in zip(pair_on, EXPERT_PAIRS)])
    pair = add([float(k) * p for k, p in enumerate(pair_on) if k > 0])
    segment = gid * float(len(EXPERT_PAIRS)) + pair
    pad = jnp.zeros((ROUTE_ROWS - ROUTE_GROUP_COL - 1,) + smax.shape[1:], F32)
    return jnp.concatenate([w_lo, w_hi, segment] + cw + [gid, pad], axis=0)


def _mix_kernel(x_ref, oa_ref, ob_ref, oc_ref, gl_ref, mk_ref, mv_ref, wb_ref, wout_ref, wq_ref, wo_ref,
                g1_ref, b1_ref, g2_ref, b2_ref, rt_ref, x2_ref, route_ref, *, sub, seq_rows):
    tiles = [slice(s0, s0 + sub) for s0 in range(0, x_ref.shape[0], sub)]
    cd = wb_ref.dtype

    mixed = []
    for rows in tiles:
        acc = None
        for b, o_ref in enumerate((oa_ref, ob_ref, oc_ref)):
            per_branch = _dot(o_ref[rows], wb_ref[b])
            gate = _sigmoid(gl_ref[rows, D_MODEL * b:D_MODEL * (b + 1)].astype(F32))
            acc = gate * per_branch if acc is None else acc + gate * per_branch
        mixed.append(acc.astype(cd))

    x1 = [_layer_norm(DN_ALPHA * x_ref[rows] + _dot(m, wout_ref[...]), g1_ref[...], b1_ref[...])
          for rows, m in zip(tiles, mixed)]
    q = [(_dot(t.astype(cd), wq_ref[...]) * (X_HEAD_DIM ** -0.5)).astype(cd) for t in x1]

    piece = min(sub, seq_rows)
    heads = [[] for _ in tiles]
    for h in range(X_HEADS):
        sl = slice(X_HEAD_DIM * h, X_HEAD_DIM * (h + 1))
        spans = [(t, r0, (t * sub + r0) // seq_rows) for t in range(len(tiles)) for r0 in range(0, sub, piece)]
        scores = [_dot_nt(q[t][r0:r0 + piece, sl], mk_ref[mem, :, sl]) for t, r0, mem in spans]
        outs = [[] for _ in tiles]
        for (t, r0, mem), s in zip(spans, scores):
            m = jnp.max(s, axis=-1, keepdims=True)
            e = jnp.exp(s - m)
            l = jnp.sum(e, axis=-1, keepdims=True)
            outs[t].append((_dot(e.astype(cd), mv_ref[mem, :, sl]) * (1.0 / l)).astype(cd))
        for t, parts in enumerate(outs):
            heads[t].append(parts[0] if len(parts) == 1 else jnp.concatenate(parts, axis=0))

    attn = [_dot(jnp.concatenate(hs, axis=-1), wo_ref[...]) for hs in heads]
    x2 = [_layer_norm(DN_ALPHA * a + b, g2_ref[...], b2_ref[...]) for a, b in zip(x1, attn)]

    r_hi, r_lo = _split2(rt_ref[...])
    r_both = jnp.concatenate([r_hi, r_lo], axis=0)
    eye_r = lax.broadcasted_iota(jnp.int32, (ROUTE_ROWS, ROUTE_LANES), 0)
    eye_c = lax.broadcasted_iota(jnp.int32, (ROUTE_ROWS, ROUTE_LANES), 1)
    eye = jnp.where(eye_r == eye_c, 1.0, 0.0).astype(BF16)
    logits = []
    for rows, t in zip(tiles, x2):
        x2_ref[rows] = t
        x_hi, x_lo = _split2(t)
        by_hi = _dot_nt(r_both, x_hi)
        logits.append(by_hi[N_EXPERTS:] + _dot_nt(r_hi, x_lo) + by_hi[:N_EXPERTS])
    hi, mid, lo = _split3(_route(logits[0] if len(logits) == 1 else jnp.concatenate(logits, axis=1)))
    for rows in tiles:
        route_ref[rows] = _dot_tn(hi[:, rows], eye) + _dot_tn(mid[:, rows], eye) + _dot_tn(lo[:, rows], eye)


def _mix(x, oa, ob, oc, gl, mk, mv, w, batch, seq):
    n_mem = mk.shape[1]
    if seq >= 2 * MIX_TILE:
        sub, tm, per_tile = MIX_TILE, 2 * MIX_TILE, 1
        mem = pl.BlockSpec((1, n_mem, D_MODEL), lambda b, i: (b, 0, 0))
    else:
        assert MIX_TILE % seq == 0 and batch % (MIX_TILE // seq) == 0
        sub, tm, per_tile = MIX_TILE, MIX_TILE, MIX_TILE // seq
        mem = pl.BlockSpec((per_tile, n_mem, D_MODEL), lambda b, i: (b, 0, 0), pipeline_mode=pl.Buffered(1))
    nt = (per_tile * seq) // tm
    row = lambda b, i: (b * nt + i, 0)
    vec = _const_spec((1, D_MODEL))
    sq = _layer_spec((D_MODEL, D_MODEL), w['layer'])
    return pl.pallas_call(
        functools.partial(_mix_kernel, sub=sub, seq_rows=seq),
        grid=(batch // per_tile, nt),
        in_specs=[pl.BlockSpec((tm, D_MODEL), row)] + [pl.BlockSpec((tm, A_WIDTH), row)] * 3
        + [pl.BlockSpec((tm, N_BRANCH * D_MODEL), row), mem, mem,
           _layer_spec((N_BRANCH, A_WIDTH, D_MODEL), w['layer']), sq, sq, sq, vec, vec, vec, vec,
           _const_spec((N_EXPERTS, D_MODEL))],
        out_specs=[pl.BlockSpec((tm, D_MODEL), row), pl.BlockSpec((tm, ROUTE_LANES), row)],
        out_shape=[jax.ShapeDtypeStruct((batch * seq, D_MODEL), F32),
                   jax.ShapeDtypeStruct((batch * seq, ROUTE_LANES), F32)],
        compiler_params=_params("arbitrary", "arbitrary"),
        name="mix",
    )(x, oa, ob, oc, gl, mk, mv, w['w_branch'], w['w_out'], w['xa_wq'], w['xa_wo'],
      w['ln1_g'], w['ln1_b'], w['ln2_g'], w['ln2_b'], w['router_t'])


def _sc_gather_rows(tables, idx):
    n_out = idx.shape[0]
    per_worker = n_out // SC_WORKERS
    window = min(SC_GATHER_WINDOW, per_worker)
    steps = per_worker // window
    assert per_worker * SC_WORKERS == n_out and steps * window == per_worker and window % 8 == 0
    idx3 = idx.astype(jnp.int32).reshape(SC_WORKERS, steps, window)
    mesh = plsc.VectorSubcoreMesh(core_axis_name="core", subcore_axis_name="subcore")
    n_tab = len(tables)

    def body(*refs):
        tab_hbm = refs[:n_tab]
        idx_hbm = refs[n_tab]
        out_hbm = refs[n_tab + 1:2 * n_tab + 1]
        idx_v = refs[2 * n_tab + 1]
        rows_v = refs[2 * n_tab + 2:3 * n_tab + 2]
        sem = refs[3 * n_tab + 2]
        wid = lax.axis_index("subcore") * SC_CORES + lax.axis_index("core")
        pltpu.sync_copy(idx_hbm.at[wid], idx_v)

        @pl.loop(0, steps)
        def _(j):
            base = wid * per_worker + j * window
            for k in range(n_tab):
                pltpu.async_copy(tab_hbm[k].at[idx_v.at[j]], rows_v[k], sem).wait()
                pltpu.sync_copy(rows_v[k], out_hbm[k].at[pl.ds(base, window)])

    call = pl.kernel(
        body,
        out_type=[jax.ShapeDtypeStruct((n_out, t.shape[1]), t.dtype) for t in tables],
        mesh=mesh,
        scratch_types=[pltpu.VMEM((steps, window), jnp.int32)]
        + [pltpu.VMEM((window, t.shape[1]), t.dtype) for t in tables] + [pltpu.SemaphoreType.DMA],
        name="sc_gather",
    )
    return call(*tables, idx3)


def _dispatch_plan(seg, tm, n_seg):
    n = seg.shape[0]
    n_pad = n + n_seg * tm
    ids = jnp.arange(n_seg, dtype=jnp.int32)
    onehot = (seg[:, None] == ids[None, :]).astype(jnp.int32)
    counts = jnp.sum(onehot, axis=0)
    padded = ((counts + tm - 1) // tm) * tm
    start_p = jnp.cumsum(padded) - padded
    start_u = jnp.cumsum(counts) - counts
    order = jnp.argsort(seg, stable=True).astype(jnp.int32)
    rank_sorted = jnp.argsort(order).astype(jnp.int32)
    pos = (rank_sorted + jnp.sum(onehot * (start_p - start_u)[None, :], axis=1)).astype(jnp.int32)
    gap = padded - counts
    gap_end = jnp.cumsum(gap)
    j = jnp.arange(n_pad - n, dtype=jnp.int32)
    region = jnp.sum((j[:, None] >= gap_end[None, :]).astype(jnp.int32), axis=1)
    oh_region = (region[:, None] == jnp.arange(n_seg + 1, dtype=jnp.int32)[None, :]).astype(jnp.int32)
    first_empty = jnp.concatenate([start_p + counts, jnp.sum(padded)[None]])
    region_start = jnp.concatenate([gap_end - gap, gap_end[-1:]])
    empty_slot = j + jnp.sum(oh_region * (first_empty - region_start)[None, :], axis=1)
    _, src = lax.sort_key_val(jnp.concatenate([pos, empty_slot.astype(jnp.int32)]),
                              jnp.concatenate([jnp.arange(n, dtype=jnp.int32), j % n]))
    tile_start = jnp.arange(n_pad // tm, dtype=jnp.int32) * tm
    seg_tile = jnp.minimum(jnp.sum((tile_start[:, None] >= (start_p + padded)[None, :]).astype(jnp.int32), axis=1),
                           n_seg - 1)
    return pos, src, seg_tile, jnp.sum(padded) // tm


def _moe_kernel(ex_ref, x_ref, r_ref, *refs, n_exp, col0):
    w_refs, (g_ref, b_ref, o_ref) = refs[:3 * n_exp], refs[3 * n_exp:]

    @pl.when(pl.program_id(0) < ex_ref[n_exp * pl.num_programs(0)])
    def _():
        x = x_ref[...]
        cd = w_refs[0].dtype
        xb = x.astype(cd)
        r = r_ref[...]
        y = None
        for e in range(n_exp):
            w1_ref, w3_ref, w2_ref = w_refs[3 * e:3 * e + 3]
            h1 = _dot(xb, w1_ref[...])
            h3 = _dot(xb, w3_ref[...])
            ye = r[:, col0 + e:col0 + e + 1] * _dot((h1 * _sigmoid(h1) * h3).astype(cd), w2_ref[...])
            y = ye if y is None else y + ye
        o_ref[...] = _layer_norm(DN_ALPHA * x + y, g_ref[...], b_ref[...])


def _moe_sorted(xs, rs, experts, tiles_used, w, tm, col0):
    n_pad = xs.shape[0]
    n_exp, nt = experts.shape
    layer = w['layer']
    row = lambda i, ex: (i, 0)
    vec = pl.BlockSpec((1, D_MODEL), lambda i, ex: (0, 0))
    mode = dict(pipeline_mode=pl.Buffered(1)) if w['w1'].dtype == F32 else {}
    w_specs, w_args = [], []
    for e in range(n_exp):
        pick = lambda i, ex, e=e: (layer, ex[e * nt + i], 0, 0)
        w_specs += [pl.BlockSpec((None, None, D_MODEL, D_EXPERT), pick, **mode),
                    pl.BlockSpec((None, None, D_MODEL, D_EXPERT), pick, **mode),
                    pl.BlockSpec((None, None, D_EXPERT, D_MODEL), pick, **mode)]
        w_args += [w['w1'], w['w3'], w['w2']]
    return pl.pallas_call(
        functools.partial(_moe_kernel, n_exp=n_exp, col0=col0),
        grid_spec=pltpu.PrefetchScalarGridSpec(
            num_scalar_prefetch=1,
            grid=(nt,),
            in_specs=[pl.BlockSpec((tm, D_MODEL), row), pl.BlockSpec((tm, ROUTE_LANES), row)] + w_specs + [vec, vec],
            out_specs=pl.BlockSpec((tm, D_MODEL), row),
        ),
        out_shape=jax.ShapeDtypeStruct((n_pad, D_MODEL), F32),
        compiler_params=_params("arbitrary"),
        name="moe",
    )(jnp.concatenate([experts.reshape(n_exp * nt), tiles_used.astype(jnp.int32).reshape(1)]),
      xs, rs, *w_args, w['ln3_g'], w['ln3_b'])


def _moe(x2, route, w):
    n = x2.shape[0]
    n_pairs = len(EXPERT_PAIRS)
    by_pair = n >= N_SEGMENTS * MOE_TILE
    if by_pair:
        tm, n_seg, col0 = MOE_TILE, N_SEGMENTS, 0
        seg = route[:, ROUTE_PAIR_COL].astype(jnp.int32)
    else:
        tm, n_seg, col0 = min(MOE_TILE, n // N_GROUPS), N_GROUPS, ROUTE_PAIR_COL + 1
        seg = route[:, ROUTE_GROUP_COL].astype(jnp.int32)
    pos, src, seg_tile, tiles_used = _dispatch_plan(seg, tm, n_seg)
    if by_pair:
        p = seg_tile % n_pairs
        member = lambda side: functools.reduce(
            lambda x, y: x + y, [jnp.where(p == k, pr[side], 0) for k, pr in enumerate(EXPERT_PAIRS)])
        experts = GROUP_SIZE * (seg_tile // n_pairs) + jnp.stack([member(0), member(1)])
    else:
        experts = GROUP_SIZE * seg_tile[None, :] + jnp.arange(GROUP_SIZE, dtype=jnp.int32)[:, None]
    xs, rs = _sc_gather_rows([x2, route], src)
    ys = _moe_sorted(xs, rs, experts.astype(jnp.int32), tiles_used, w, tm, col0)
    return _sc_gather_rows([ys], pos)[0]


def _trunk_layer(x, w, mem_k, mem_v, cache_k, cache_v, s0, h0, conv0, batch, seq, prompt):
    qkv, hg, rg, gl = _inproj(x, w['w_in'], w['layer'])
    keep = min(A_WINDOW, seq)
    x_keep = x.reshape(batch, seq, D_MODEL)[:, seq - keep:].reshape(batch * keep, D_MODEL)
    k32, v32 = [t.reshape(batch, keep, A_HEADS, A_HEAD_DIM) for t in _kv_rows(x_keep, w['w_in'], w['layer'])]
    if prompt:
        oa = _attn_prompt(qkv, w['band_bias'], batch, seq)
        new_k, new_v = k32, v32
        tile = 256
    else:
        win = cache_k.shape[1]
        ck = cache_k.reshape(batch, win, A_WIDTH)
        cv = cache_v.reshape(batch, win, A_WIDTH)
        oa = _attn_sample(qkv, ck, cv, w['band_bias'][:, :seq, :win + 2 * CHUNK], batch, seq)
        shift = lambda cache, new: (jnp.concatenate([cache[:, seq:], new], axis=1) if seq <= win
                                    else new[:, seq - win:])
        new_k, new_v = shift(cache_k, k32), shift(cache_v, v32)
        tile = seq
    ob, s_bd = _hgrn(hg, w['lb'], w['hgrn_g'], _state_to_block_diag(s0), batch, seq, tile, w['w_in'].dtype)
    conv0_pad = jnp.concatenate([jnp.zeros((batch, 8 - (C_CONV - 1), C_WIDTH), F32), conv0.astype(F32)], axis=1)
    oc, h_new = _rglru(rg, conv0_pad, h0.astype(F32).reshape(batch, 1, C_WIDTH), w, batch, seq, tile, prompt)
    n_tail = min(seq, C_CONV - 1)
    xr_tail = rg.reshape(batch, seq, 2 * C_WIDTH)[:, seq - n_tail:, :C_WIDTH]
    conv_new = jnp.concatenate([conv0.astype(F32), xr_tail], axis=1)[:, n_tail:]
    x2, route = _mix(x, oa, ob, oc, gl, mem_k, mem_v, w, batch, seq)
    x3 = _moe(x2, route, w)
    return x3, (new_k, new_v, _block_diag_to_state(s_bd), h_new.reshape(batch, C_WIDTH), conv_new)


def kernel(x_prompt, x_sample, cache_attn_k, cache_attn_v, state_hgrn, state_rglru, state_conv, cache_mem_k, cache_mem_v, mem_prompt, w_in, attn_rel_bias, hgrn_lb_logits, hgrn_norm_g, rg_conv_w, rg_conv_b, rg_wa, rg_ba, rg_wx, rg_bx, rg_lambda, w_branch, w_out, ln1_g, ln1_b, xa_wq, xa_wk, xa_wv, xa_wo, ln2_g, ln2_b, moe_router, moe_w1, moe_w3, moe_w2, ln3_g, ln3_b):
    bp, tp, _ = x_prompt.shape
    bs, ts, _ = x_sample.shape
    n_mem = mem_prompt.shape[1]
    depth = w_in.shape[0]

    p = jax.nn.softmax(hgrn_lb_logits.astype(F32), axis=0)
    lb_all = jnp.cumsum(p, axis=0) - p[0:1]
    vec = lambda t: t.astype(F32).reshape(1, -1)

    xp = x_prompt.reshape(bp * tp, D_MODEL)
    xs = x_sample.reshape(bs * ts, D_MODEL)
    mem2d = mem_prompt.reshape(bp * n_mem, D_MODEL)
    s0p = jnp.zeros((bp, B_HEADS, B_KEY_DIM, B_VAL_DIM), F32)
    h0p = jnp.zeros((bp, C_WIDTH), F32)
    conv0p = jnp.zeros((bp, C_CONV - 1, C_WIDTH), F32)
    router_t = moe_router.astype(F32).T

    stacked = {'w_in': w_in, 'w_branch': w_branch, 'w_out': w_out, 'xa_wq': xa_wq, 'xa_wo': xa_wo,
               'w1': moe_w1, 'w3': moe_w3, 'w2': moe_w2}
    stacked_fast = {k: v.astype(BF16) for k, v in stacked.items()}
    stacked_precise = {k: v.astype(F32) for k, v in stacked.items()}
    wk_b, wv_b = xa_wk.astype(BF16), xa_wv.astype(BF16)

    outs_p, outs_s, mem_ks, mem_vs = [], [], [], []
    for l in range(depth):
        wf = {'wa_bd': _block_diag_weight(rg_wa[l]), 'wx_bd': _block_diag_weight(rg_wx[l])}
        shared = {
            'layer': l, 'band_bias': _band_bias(attn_rel_bias[l]),
            'lb': vec(lb_all[l]), 'hgrn_g': vec(jnp.tile(hgrn_norm_g[l], B_HEADS)),
            'conv_w': rg_conv_w[l].astype(F32), 'conv_b': vec(rg_conv_b[l]),
            'ba': vec(rg_ba[l]), 'bx': vec(rg_bx[l]), 'lam': vec(rg_lambda[l]),
            'ln1_g': vec(ln1_g[l]), 'ln1_b': vec(ln1_b[l]), 'ln2_g': vec(ln2_g[l]), 'ln2_b': vec(ln2_b[l]),
            'ln3_g': vec(ln3_g[l]), 'ln3_b': vec(ln3_b[l]), 'router_t': router_t,
        }
        w_fast = dict(shared, **stacked_fast, **{k: v.astype(BF16) for k, v in wf.items()})
        w_precise = dict(shared, **stacked_precise, **{k: v.astype(F32) for k, v in wf.items()})
        mk_p = _matmul(mem2d, wk_b, l)
        mv_p = _matmul(mem2d, wv_b, l)
        mem_ks.append(mk_p.reshape(bp, n_mem, X_HEADS, X_HEAD_DIM))
        mem_vs.append(mv_p.reshape(bp, n_mem, X_HEADS, X_HEAD_DIM))
        xp, st_p = _trunk_layer(xp, w_fast, mk_p.reshape(bp, n_mem, D_MODEL).astype(BF16),
                                mv_p.reshape(bp, n_mem, D_MODEL).astype(BF16),
                                None, None, s0p, h0p, conv0p, bp, tp, True)
        xs, st_s = _trunk_layer(xs, w_precise, cache_mem_k[l].reshape(bs, n_mem, D_MODEL).astype(F32),
                                cache_mem_v[l].reshape(bs, n_mem, D_MODEL).astype(F32),
                                cache_attn_k[l], cache_attn_v[l], state_hgrn[l], state_rglru[l], state_conv[l],
                                bs, ts, False)
        outs_p.append(st_p)
        outs_s.append(st_s)

    stack = lambda items, j: jnp.stack([it[j] for it in items])
    return (xp.reshape(bp, tp, D_MODEL), xs.reshape(bs, ts, D_MODEL),
            stack(outs_p, 0), stack(outs_p, 1), stack(outs_p, 2), stack(outs_p, 3), stack(outs_p, 4),
            jnp.stack(mem_ks), jnp.stack(mem_vs),
            stack(outs_s, 0), stack(outs_s, 1), stack(outs_s, 2), stack(outs_s, 3), stack(outs_s, 4))
```

```python
import functools

import numpy as np
import jax
import jax.numpy as jnp
from jax import lax
from jax.experimental import pallas as pl
from jax.experimental.pallas import tpu as pltpu
from jax.experimental.pallas import tpu_sc as plsc

F32 = jnp.float32
BF16 = jnp.bfloat16

D_MODEL = 1024
CHUNK = 64
A_HEADS = 8
A_HEAD_DIM = 64
A_WIDTH = A_HEADS * A_HEAD_DIM
A_PAST_CHUNKS = 8
A_WINDOW = A_PAST_CHUNKS * CHUNK
A_MAX_REL = 256
B_HEADS = 8
B_KEY_DIM = 64
B_VAL_DIM = 64
B_WIDTH = B_HEADS * B_KEY_DIM
C_WIDTH = 512
C_CONV = 4
C_GATE_C = 8.0
N_BRANCH = 3
IN_COLS = 3 * A_WIDTH + 4 * B_WIDTH + 2 * C_WIDTH + N_BRANCH * D_MODEL
X_HEADS = 4
X_HEAD_DIM = D_MODEL // X_HEADS
N_EXPERTS = 16
N_GROUPS = 4
GROUP_SIZE = N_EXPERTS // N_GROUPS
D_EXPERT = D_MODEL // 2
DEPTH = 4
DN_ALPHA = (2 * DEPTH) ** 0.25
LN_EPS = 1e-5
RMS_EPS = 1e-6
NEG_INF = -1e30

VMEM_LIMIT_BYTES = 56 * 1024 * 1024

INPROJ_TILE = 512
MIX_TILE = 256
ATTN_Q_TILE = 256
ATTN_K_TILE = 256
HG_GROUP = 256
RG_SCAN_GROUP = 8
HG_SEQS_PER_STEP = 2
HG_HALF = CHUNK // 2
HG_QUARTER = CHUNK // 4
EXPERT_PAIRS = tuple((lo, hi) for lo in range(GROUP_SIZE) for hi in range(lo + 1, GROUP_SIZE))
N_SEGMENTS = N_GROUPS * len(EXPERT_PAIRS)
ROUTE_ROWS = 16
ROUTE_PAIR_COL = 2
ROUTE_GROUP_COL = 7
ROUTE_LANES = 128
MOE_TILE = 512
SC_CORES = 2
SC_SUBCORES = 16
SC_WORKERS = SC_CORES * SC_SUBCORES
SC_GATHER_WINDOW = 64


def _split2(x):
    hi = x.astype(BF16)
    lo = (x - hi.astype(F32)).astype(BF16)
    return hi, lo


def _contract(a, b, dims):
    dg = lambda u, v: lax.dot_general(u, v, (dims, ((), ())), preferred_element_type=F32)
    if a.dtype == F32:
        a_hi, a_lo = _split2(a)
        b_hi, b_lo = _split2(b)
        return dg(a_lo, b_hi) + dg(a_hi, b_lo) + dg(a_hi, b_hi)
    return dg(a, b)


def _dot(a, b):
    return _contract(a, b, ((1,), (0,)))


def _dot_nt(a, b):
    return _contract(a, b, ((1,), (1,)))


def _dot_tn(a, b):
    return _contract(a, b, ((0,), (0,)))


def _split3(x):
    hi = x.astype(BF16)
    r1 = x - hi.astype(F32)
    mid = r1.astype(BF16)
    lo = (r1 - mid.astype(F32)).astype(BF16)
    return hi, mid, lo


def _sigmoid(x):
    return 0.5 * jnp.tanh(0.5 * x) + 0.5


def _layer_norm(x, g, b):
    mu = jnp.mean(x, axis=-1, keepdims=True)
    xc = x - mu
    var = jnp.mean(xc * xc, axis=-1, keepdims=True)
    return xc * lax.rsqrt(var + LN_EPS) * g + b


def _params(*semantics):
    return pltpu.CompilerParams(dimension_semantics=semantics, vmem_limit_bytes=VMEM_LIMIT_BYTES)


def _const_spec(shape):
    nd = len(shape)
    return pl.BlockSpec(shape, lambda *_: (0,) * nd, pipeline_mode=pl.Buffered(1))


def _layer_spec(shape, layer):
    nd = len(shape)
    return pl.BlockSpec((None,) + tuple(shape), lambda *_: (layer,) + (0,) * nd, pipeline_mode=pl.Buffered(1))


def _inproj_kernel(x_ref, w_ref, qkv_ref, hg_ref, rg_ref, gl_ref):
    xb = x_ref[...].astype(w_ref.dtype)
    cw = 512

    def mm(c0):
        return _dot(xb, w_ref[:, c0:c0 + cw])

    for j in range(3):
        qkv_ref[:, cw * j:cw * (j + 1)] = mm(cw * j).astype(qkv_ref.dtype)
    base = 3 * A_WIDTH
    for j in range(4):
        hg_ref[:, cw * j:cw * (j + 1)] = mm(base + cw * j)
    base += 4 * B_WIDTH
    for j in range(2):
        rg_ref[:, cw * j:cw * (j + 1)] = mm(base + cw * j)
    base += 2 * C_WIDTH
    for j in range(N_BRANCH * D_MODEL // cw):
        gl_ref[:, cw * j:cw * (j + 1)] = mm(base + cw * j).astype(gl_ref.dtype)


def _inproj(x, w_in, layer):
    n = x.shape[0]
    cd = w_in.dtype
    tm = INPROJ_TILE if cd == BF16 else INPROJ_TILE // 2
    assert n % tm == 0
    row = lambda i: (i, 0)
    return pl.pallas_call(
        _inproj_kernel,
        grid=(n // tm,),
        in_specs=[pl.BlockSpec((tm, D_MODEL), row), _layer_spec((D_MODEL, IN_COLS), layer)],
        out_specs=[pl.BlockSpec((tm, 3 * A_WIDTH), row),
                   pl.BlockSpec((tm, 4 * B_WIDTH), row), pl.BlockSpec((tm, 2 * C_WIDTH), row),
                   pl.BlockSpec((tm, N_BRANCH * D_MODEL), row)],
        out_shape=[jax.ShapeDtypeStruct((n, 3 * A_WIDTH), cd),
                   jax.ShapeDtypeStruct((n, 4 * B_WIDTH), F32), jax.ShapeDtypeStruct((n, 2 * C_WIDTH), F32),
                   jax.ShapeDtypeStruct((n, N_BRANCH * D_MODEL), cd)],
        compiler_params=_params("arbitrary"),
        name="inproj",
    )(x, w_in)


def _matmul_kernel(x_ref, w_ref, o_ref):
    o_ref[...] = _dot(x_ref[...].astype(w_ref.dtype), w_ref[...])


def _matmul(x, w, layer):
    n, k = x.shape
    m = w.shape[2]
    tm = 256
    return pl.pallas_call(
        _matmul_kernel,
        grid=(n // tm,),
        in_specs=[pl.BlockSpec((tm, k), lambda i: (i, 0)), _layer_spec((k, m), layer)],
        out_specs=pl.BlockSpec((tm, m), lambda i: (i, 0)),
        out_shape=jax.ShapeDtypeStruct((n, m), F32),
        compiler_params=_params("arbitrary"),
        name="matmul",
    )(x, w)


def _kv_kernel(x_ref, wk_ref, wv_ref, k_ref, v_ref):
    xb = x_ref[...].astype(wk_ref.dtype)
    k_ref[...] = _dot(xb, wk_ref[...])
    v_ref[...] = _dot(xb, wv_ref[...])


def _kv_rows(x, w_in, layer):
    n = x.shape[0]
    tm = 256
    col = lambda j: pl.BlockSpec((None, D_MODEL, A_WIDTH), lambda i: (layer, 0, j), pipeline_mode=pl.Buffered(1))
    out = pl.BlockSpec((tm, A_WIDTH), lambda i: (i, 0))
    return pl.pallas_call(
        _kv_kernel,
        grid=(n // tm,),
        in_specs=[pl.BlockSpec((tm, D_MODEL), lambda i: (i, 0)), col(1), col(2)],
        out_specs=[out, out],
        out_shape=[jax.ShapeDtypeStruct((n, A_WIDTH), F32)] * 2,
        compiler_params=_params("arbitrary"),
        name="kv_rows",
    )(x, w_in, w_in)


def _attn_core(q, k, v, bias_ref, valid):
    rows = q.shape[0]
    lane = lax.broadcasted_iota(jnp.int32, (rows, 2 * A_HEAD_DIM), 1)
    first = lane < A_HEAD_DIM
    q = q * jnp.asarray(A_HEAD_DIM ** -0.5, q.dtype)

    def scores(head):
        sl = slice(2 * A_HEAD_DIM * (head // 2), 2 * A_HEAD_DIM * (head // 2 + 1))
        sel = first if head % 2 == 0 else jnp.logical_not(first)
        qm = jnp.where(sel, q[:, sl], jnp.zeros_like(q[:, sl]))
        s = _dot_nt(qm, k[:, sl]) + bias_ref[head]
        return s if valid is None else jnp.where(valid, s, NEG_INF)

    outs = []
    pair = None
    s_next = scores(0)
    for head in range(A_HEADS):
        s = s_next
        if head + 1 < A_HEADS:
            s_next = scores(head + 1)
        m = jnp.max(s, axis=-1, keepdims=True)
        e = jnp.exp(s - m)
        l = jnp.sum(e, axis=-1, keepdims=True)
        sl = slice(2 * A_HEAD_DIM * (head // 2), 2 * A_HEAD_DIM * (head // 2 + 1))
        o = _dot(e.astype(v.dtype), v[:, sl]) * (1.0 / l)
        if head % 2 == 0:
            pair = o
        else:
            outs.append(jnp.where(first, pair, o))
    return jnp.concatenate(outs, axis=-1)


def _attn_prompt_kernel(q_ref, k0_ref, k1_ref, k2_ref, v0_ref, v1_ref, v2_ref, bias_ref, o_ref):
    i = pl.program_id(1)

    def run(masked):
        k = jnp.concatenate([k0_ref[...], k1_ref[...], k2_ref[...]], axis=0)
        v = jnp.concatenate([v0_ref[...], v1_ref[...], v2_ref[...]], axis=0)
        valid = None
        if masked:
            col = lax.broadcasted_iota(jnp.int32, (ATTN_Q_TILE, 3 * ATTN_K_TILE), 1)
            valid = col >= (2 - i) * ATTN_K_TILE
        o_ref[...] = _attn_core(q_ref[...], k, v, bias_ref, valid).astype(o_ref.dtype)

    pl.when(i < 2)(lambda: run(True))
    pl.when(i >= 2)(lambda: run(False))


def _attn_prompt(qkv, bias, batch, seq):
    nt = seq // ATTN_Q_TILE
    blk = (ATTN_Q_TILE, A_WIDTH)

    def kv_spec(j, col):
        return pl.BlockSpec(blk, lambda b, i: (b * nt + jnp.maximum(i - 2 + j, 0), col))

    return pl.pallas_call(
        _attn_prompt_kernel,
        grid=(batch, nt),
        in_specs=[pl.BlockSpec(blk, lambda b, i: (b * nt + i, 0))]
        + [kv_spec(j, 1) for j in range(3)] + [kv_spec(j, 2) for j in range(3)]
        + [_const_spec(bias.shape)],
        out_specs=pl.BlockSpec(blk, lambda b, i: (b * nt + i, 0)),
        out_shape=jax.ShapeDtypeStruct((batch * seq, A_WIDTH), qkv.dtype),
        compiler_params=_params("arbitrary", "arbitrary"),
        name="attn_prompt",
    )(qkv, qkv, qkv, qkv, qkv, qkv, qkv, bias)


def _attn_sample_kernel(q_ref, kn_ref, vn_ref, ck_ref, cv_ref, bias_ref, o_ref):
    cd = q_ref.dtype
    pad = jnp.zeros((CHUNK, A_WIDTH), cd)
    k = jnp.concatenate([ck_ref[0].astype(cd), kn_ref[...], pad], axis=0)
    v = jnp.concatenate([cv_ref[0].astype(cd), vn_ref[...], pad], axis=0)
    o_ref[...] = _attn_core(q_ref[...], k, v, bias_ref, None).astype(cd)


def _attn_sample(qkv, cache_k, cache_v, bias, batch, seq):
    win = cache_k.shape[1]
    blk = (seq, A_WIDTH)
    return pl.pallas_call(
        _attn_sample_kernel,
        grid=(batch,),
        in_specs=[pl.BlockSpec(blk, lambda b: (b, 0)), pl.BlockSpec(blk, lambda b: (b, 1)),
                  pl.BlockSpec(blk, lambda b: (b, 2)),
                  pl.BlockSpec((1, win, A_WIDTH), lambda b: (b, 0, 0)),
                  pl.BlockSpec((1, win, A_WIDTH), lambda b: (b, 0, 0)),
                  _const_spec(bias.shape)],
        out_specs=pl.BlockSpec(blk, lambda b: (b, 0)),
        out_shape=jax.ShapeDtypeStruct((batch * seq, A_WIDTH), qkv.dtype),
        compiler_params=_params("arbitrary"),
        name="attn_sample",
    )(qkv, qkv, qkv, cache_k, cache_v, bias)


def _band_bias(table):
    heads = table.shape[0]
    n_keys = 3 * ATTN_K_TILE
    span = ATTN_Q_TILE + n_keys
    r = np.arange(ATTN_Q_TILE)[:, None]
    j = np.arange(n_keys)[None, :]
    band = (j // CHUNK >= r // CHUNK) & (j // CHUNK <= r // CHUNK + A_PAST_CHUNKS)
    t = table.astype(F32)
    u = jnp.concatenate([t, jnp.broadcast_to(t[:, -1:], (heads, span - t.shape[1]))], axis=1)
    w = u[:, ::-1]
    tiled = jnp.broadcast_to(w[:, None, :], (heads, ATTN_Q_TILE, span)).reshape(heads, ATTN_Q_TILE * span)
    view = tiled[:, :ATTN_Q_TILE * (span - 1)].reshape(heads, ATTN_Q_TILE, span - 1)
    bias = view[:, :, ATTN_Q_TILE - 1:ATTN_Q_TILE - 1 + n_keys]
    return jnp.where(band[None], bias, NEG_INF)


def _head_block_diag(x, head_masks):
    zero = jnp.zeros_like(x)
    return jnp.concatenate([jnp.where(m, x, zero) for m in head_masks], axis=0)


def _hgrn_rows(hg_ref, lb_ref, ng_ref, o_ref, st_ref, ot_ref, tile):
    n_groups = B_WIDTH // HG_GROUP
    cd = o_ref.dtype
    q = hg_ref[:, 0:B_WIDTH]
    f_logit = hg_ref[:, B_WIDTH:2 * B_WIDTH]
    v_in = hg_ref[:, 2 * B_WIDTH:3 * B_WIDTH].astype(cd)
    lb = lb_ref[...]
    f = lb + (1.0 - lb) * (1.0 / (1.0 + jnp.exp(-f_logit)))
    log_f = jnp.log(f)
    kk = 1.0 - f

    r_t = lax.broadcasted_iota(jnp.int32, (tile, tile), 0)
    c_t = lax.broadcasted_iota(jnp.int32, (tile, tile), 1)
    tri = jnp.where((r_t // CHUNK == c_t // CHUNK) & (c_t <= r_t), 1.0, 0.0).astype(BF16)
    split = _split3 if cd == F32 else _split2
    g_all = functools.reduce(lambda x, y: x + y, [_dot(tri, part) for part in reversed(split(log_f))])

    row = lax.broadcasted_iota(jnp.int32, (CHUNK, B_WIDTH), 0)
    upper = row >= HG_HALF
    lane_g = lax.broadcasted_iota(jnp.int32, (CHUNK, HG_GROUP), 1)
    row_g = lax.broadcasted_iota(jnp.int32, (CHUNK, HG_GROUP), 0)
    head_masks = [lane_g // B_KEY_DIM == h for h in range(HG_GROUP // B_KEY_DIM)]
    causal = (lane_g % CHUNK) <= row_g
    cross = (row_g >= HG_HALF) & ((lane_g % CHUNK) < HG_HALF)
    r_bd = lax.broadcasted_iota(jnp.int32, (HG_GROUP, HG_GROUP), 0)
    c_bd = lax.broadcasted_iota(jnp.int32, (HG_GROUP, HG_GROUP), 1)
    diag_blocks = (r_bd // B_VAL_DIM) == (c_bd // B_KEY_DIM)

    pending = []
    for c in range(tile // CHUNK):
        rs = slice(CHUNK * c, CHUNK * (c + 1))
        g = g_all[rs]
        qc = q[rs]
        kc = kk[rs]
        vc = v_in[rs]
        g_q1 = g[HG_QUARTER - 1:HG_QUARTER]
        g_mid = g[HG_HALF - 1:HG_HALF]
        g_q3 = g[HG_HALF + HG_QUARTER - 1:HG_HALF + HG_QUARTER]
        g_last = g[CHUNK - 1:CHUNK]
        d_diag = g - jnp.where(upper, g_q3, g_q1)
        d_off = jnp.where(upper, g - g_mid, g_mid - g)
        q_in = (qc * jnp.exp(g)).astype(cd)
        k_st = (kc * jnp.exp(g_last - g)).astype(cd)
        q_diag = qc * jnp.exp(d_diag)
        k_diag = kc * jnp.exp(-d_diag)
        e_off = jnp.exp(d_off)
        zero = jnp.zeros_like(qc)
        q_d = q_diag.astype(cd)
        k_d = k_diag.astype(cd)
        q_x = jnp.where(upper, qc * e_off, zero).astype(cd)
        k_x = jnp.where(upper, zero, kc * e_off).astype(cd)
        decay = jnp.exp(g_last)
        for gi in range(n_groups):
            cs = slice(HG_GROUP * gi, HG_GROUP * (gi + 1))
            att_d = _dot_nt(q_d[:, cs], _head_block_diag(k_d[:, cs], head_masks))
            att_x = _dot_nt(q_x[:, cs], _head_block_diag(k_x[:, cs], head_masks))
            att = jnp.where(cross, att_x, att_d)
            att = jnp.where(causal, att, 0.0).astype(cd)
            v_bd = _head_block_diag(vc[:, cs], head_masks)
            upd = jnp.where(diag_blocks, _dot_tn(vc[:, cs], k_st[:, cs]), 0.0)
            pending.append((rs, cs, gi, q_in[:, cs], _dot(att, v_bd), decay[:, cs], upd))

    states = [st_ref[gi] for gi in range(n_groups)]
    before = []
    for rs, cs, gi, q_in_g, o_intra, decay_g, upd in pending:
        before.append(states[gi].astype(cd))
        states[gi] = states[gi] * decay_g + upd
    for gi in range(n_groups):
        st_ref[gi] = states[gi]
    for (rs, cs, gi, q_in_g, o_intra, decay_g, upd), st_b in zip(pending, before):
        ot_ref[rs, cs] = _dot_nt(q_in_g, st_b) + o_intra

    o = ot_ref[...]
    lane_i = lax.broadcasted_iota(jnp.int32, (B_WIDTH, B_WIDTH), 0)
    lane_j = lax.broadcasted_iota(jnp.int32, (B_WIDTH, B_WIDTH), 1)
    head_ones = jnp.where(lane_i // B_VAL_DIM == lane_j // B_VAL_DIM, 1.0, 0.0).astype(BF16)
    sq_parts = split(o * o)
    sums = _dot(jnp.concatenate(sq_parts, axis=0), head_ones)
    ms = functools.reduce(lambda x, y: x + y, [sums[tile * j:tile * (j + 1)] for j in reversed(range(len(sq_parts)))])
    ms = ms * (1.0 / B_VAL_DIM)
    gate = hg_ref[:, 3 * B_WIDTH:4 * B_WIDTH]
    out = o * lax.rsqrt(ms + RMS_EPS) * ng_ref[...] * (gate * _sigmoid(gate))
    o_ref[...] = out.astype(cd)


def _state_to_block_diag(s):
    b = s.shape[0]
    hpg = HG_GROUP // B_KEY_DIM
    st = s.astype(F32).reshape(b, B_HEADS // hpg, hpg, B_KEY_DIM, B_VAL_DIM).transpose(0, 1, 2, 4, 3)
    bd = jnp.einsum('bghvc,hk->bghvkc', st, jnp.eye(hpg, dtype=F32))
    return bd.reshape(b, B_HEADS // hpg, HG_GROUP, HG_GROUP)


def _block_diag_to_state(bd):
    b = bd.shape[0]
    hpg = HG_GROUP // B_KEY_DIM
    x = bd.reshape(b, B_HEADS // hpg, hpg, B_VAL_DIM, hpg, B_KEY_DIM)
    st = jnp.einsum('bghvkc,hk->bghvc', x, jnp.eye(hpg, dtype=F32))
    return st.transpose(0, 1, 2, 4, 3).reshape(b, B_HEADS, B_KEY_DIM, B_VAL_DIM)


def _rglru_rows(i, rg_ref, conv0_ref, h0_ref, cw_ref, cb_ref, wa_ref, ba_ref, wx_ref, bx_ref, lam_ref,
                o_ref, hlast_ref, xbuf_ref, hc_ref, tile, at_start):
    pad = 8

    @pl.when(i == 0)
    def _():
        xbuf_ref[0:pad] = conv0_ref[...]
        hc_ref[...] = h0_ref[...]

    xr = rg_ref[:, 0:C_WIDTH]
    gate = rg_ref[:, C_WIDTH:2 * C_WIDTH]
    xbuf_ref[pad:pad + tile] = xr
    xc = cb_ref[...] + cw_ref[C_CONV - 1:C_CONV] * xr
    for j in range(1, C_CONV):
        xc = xc + cw_ref[C_CONV - 1 - j:C_CONV - j] * xbuf_ref[pad - j:pad - j + tile]
    xbuf_ref[0:pad] = xbuf_ref[tile:tile + pad]

    xcb = xc.astype(wa_ref.dtype)
    r = _sigmoid(_dot(xcb, wa_ref[...]) + ba_ref[...])
    ig = _sigmoid(_dot(xcb, wx_ref[...]) + bx_ref[...])
    neg_lam = -lam_ref[...]
    softplus = jnp.maximum(neg_lam, 0.0) + jnp.log(1.0 + jnp.exp(-jnp.abs(neg_lam)))
    a = jnp.exp(r * (-C_GATE_C * softplus))
    mult = jnp.sqrt(1.0 - a * a)
    row = lax.broadcasted_iota(jnp.int32, (tile, C_WIDTH), 0)
    if at_start:
        mult = jnp.where((row == 0) & (i == 0), 1.0, mult)
    b = mult * ig * xc

    in_group = row % RG_SCAN_GROUP
    d = 1
    while d < RG_SCAN_GROUP:
        a_sh = pltpu.roll(a, d, 0)
        b_sh = pltpu.roll(b, d, 0)
        keep = in_group >= d
        b = jnp.where(keep, a * b_sh + b, b)
        a = jnp.where(keep, a * a_sh, a)
        d *= 2
    carry = hc_ref[...]
    groups = []
    for j in range(tile // RG_SCAN_GROUP):
        rs = slice(RG_SCAN_GROUP * j, RG_SCAN_GROUP * (j + 1))
        hj = a[rs] * carry + b[rs]
        carry = hj[RG_SCAN_GROUP - 1:RG_SCAN_GROUP]
        groups.append(hj)
    h = jnp.concatenate(groups, axis=0)
    hc_ref[...] = h[tile - 1:tile]
    hlast_ref[...] = h[tile - 1:tile]
    gelu = 0.5 * gate * (1.0 + jnp.tanh(np.sqrt(2.0 / np.pi).astype(np.float32) * (gate + 0.044715 * gate * gate * gate)))
    o_ref[...] = (h * gelu).astype(o_ref.dtype)


def _recur_kernel(hg_ref, lb_ref, ng_ref, s0_ref, rg_ref, conv0_ref, h0_ref, cw_ref, cb_ref, wa_ref, ba_ref,
                  wx_ref, bx_ref, lam_ref, ob_ref, sfin_ref, oc_ref, hlast_ref, st_ref, ot_ref, xbuf_ref, hc_ref,
                  *, tile, at_start):
    i = pl.program_id(1)

    @pl.when(i == 0)
    def _():
        st_ref[...] = s0_ref[...]

    for bb in range(hg_ref.shape[0]):
        _hgrn_rows(hg_ref.at[bb], lb_ref, ng_ref, ob_ref.at[bb], st_ref.at[bb], ot_ref.at[bb], tile)
        _rglru_rows(i, rg_ref.at[bb], conv0_ref.at[bb], h0_ref.at[bb], cw_ref, cb_ref, wa_ref, ba_ref, wx_ref,
                    bx_ref, lam_ref, oc_ref.at[bb], hlast_ref.at[bb], xbuf_ref.at[bb], hc_ref.at[bb], tile, at_start)

    @pl.when(i == pl.num_programs(1) - 1)
    def _():
        sfin_ref[...] = st_ref[...]


def _recurrences(hg, rg, s0_bd, conv0_pad, h0, w, batch, seq, tile, at_start):
    nt = seq // tile
    n_groups = B_WIDTH // HG_GROUP
    par = HG_SEQS_PER_STEP
    assert batch % par == 0
    cd = w['wa_bd'].dtype
    st_blk = (par, n_groups, HG_GROUP, HG_GROUP)
    per_seq = lambda shape: pl.BlockSpec((par,) + shape, lambda b, i: (b,) + (0,) * len(shape))
    rows = lambda width: pl.BlockSpec((par, tile, width), lambda b, i: (b, i, 0))
    vec_b, vec_c = _const_spec((1, B_WIDTH)), _const_spec((1, C_WIDTH))
    ob, s_fin, oc, h_last = pl.pallas_call(
        functools.partial(_recur_kernel, tile=tile, at_start=at_start),
        grid=(batch // par, nt),
        in_specs=[rows(4 * B_WIDTH), vec_b, vec_b, per_seq(st_blk[1:]),
                  rows(2 * C_WIDTH), per_seq((8, C_WIDTH)), per_seq((1, C_WIDTH)),
                  _const_spec((C_CONV, C_WIDTH)), vec_c,
                  _const_spec((C_WIDTH, C_WIDTH)), vec_c, _const_spec((C_WIDTH, C_WIDTH)), vec_c, vec_c],
        out_specs=[rows(B_WIDTH), per_seq(st_blk[1:]), rows(C_WIDTH), per_seq((1, C_WIDTH))],
        out_shape=[jax.ShapeDtypeStruct((batch, seq, B_WIDTH), cd),
                   jax.ShapeDtypeStruct((batch,) + st_blk[1:], F32),
                   jax.ShapeDtypeStruct((batch, seq, C_WIDTH), cd),
                   jax.ShapeDtypeStruct((batch, 1, C_WIDTH), F32)],
        scratch_shapes=[pltpu.VMEM(st_blk, F32), pltpu.VMEM((par, tile, B_WIDTH), F32),
                        pltpu.VMEM((par, tile + 8, C_WIDTH), F32), pltpu.VMEM((par, 1, C_WIDTH), F32)],
        compiler_params=_params("arbitrary", "arbitrary"),
        name="recur",
    )(hg.reshape(batch, seq, 4 * B_WIDTH), w['lb'], w['hgrn_g'], s0_bd,
      rg.reshape(batch, seq, 2 * C_WIDTH), conv0_pad, h0,
      w['conv_w'], w['conv_b'], w['wa_bd'], w['ba'], w['wx_bd'], w['bx'], w['lam'])
    return ob.reshape(batch * seq, B_WIDTH), s_fin, oc.reshape(batch * seq, C_WIDTH), h_last


def _block_diag_weight(w):
    n, d, e = w.shape
    return jnp.einsum('nde,nm->ndme', w, jnp.eye(n, dtype=w.dtype)).reshape(n * d, n * e)


def _route(logits_t):
    m = jnp.max(logits_t, axis=0, keepdims=True)
    e = jnp.exp(logits_t - m)
    p = e / jnp.sum(e, axis=0, keepdims=True)
    rows = [p[j:j + 1] for j in range(N_EXPERTS)]
    scores = []
    for g in range(N_GROUPS):
        mem = rows[GROUP_SIZE * g:GROUP_SIZE * (g + 1)]
        best = None
        for a in range(GROUP_SIZE):
            for b in range(a + 1, GROUP_SIZE):
                pair = mem[a] + mem[b]
                best = pair if best is None else jnp.maximum(best, pair)
        scores.append(best)
    smax = functools.reduce(jnp.maximum, scores)
    taken = jnp.zeros_like(smax)
    sel = []
    for g in range(N_GROUPS):
        hit = jnp.where(scores[g] == smax, 1.0, 0.0) * (1.0 - taken)
        taken = taken + hit
        sel.append(hit)
    picked = []
    for j in range(N_EXPERTS):
        g = j // GROUP_SIZE
        rank = jnp.zeros_like(smax)
        for o in range(GROUP_SIZE * g, GROUP_SIZE * (g + 1)):
            if o == j:
                continue
            ahead = (rows[o] >= rows[j]) if o < j else (rows[o] > rows[j])
            rank = rank + jnp.where(ahead, 1.0, 0.0)
        picked.append(sel[g] * jnp.where(rank < float(2), 1.0, 0.0))
    denom = functools.reduce(lambda x, y: x + y, [picked[j] * rows[j] for j in range(N_EXPERTS)])
    comb = [picked[j] * rows[j] / denom for j in range(N_EXPERTS)]
    add = lambda items: functools.reduce(lambda x, y: x + y, items)
    cw = [add([sel[g] * comb[GROUP_SIZE * g + m] for g in range(N_GROUPS)]) for m in range(GROUP_SIZE)]
    on = [add([sel[g] * picked[GROUP_SIZE * g + m] for g in range(N_GROUPS)]) for m in range(GROUP_SIZE)]
    gid = add([float(g) * sel[g] for g in range(1, N_GROUPS)])
    pair_on = [on[lo] * on[hi] for lo, hi in EXPERT_PAIRS]
    w_lo = add([p * cw[lo] for p, (lo, hi) in zip(pair_on, EXPERT_PAIRS)])
    w_hi = add([p * cw[hi] for p, (lo, hi) in zip(pair_on, EXPERT_PAIRS)])
    pair = add([float(k) * p for k, p in enumerate(pair_on) if k > 0])
    segment = gid * float(len(EXPERT_PAIRS)) + pair
    pad = jnp.zeros((ROUTE_ROWS - ROUTE_GROUP_COL - 1,) + smax.shape[1:], F32)
    return jnp.concatenate([w_lo, w_hi, segment] + cw + [gid, pad], axis=0)


def _mix_kernel(x_ref, oa_ref, ob_ref, oc_ref, gl_ref, mk_ref, mv_ref, wb_ref, wout_ref, wq_ref, wo_ref,
                g1_ref, b1_ref, g2_ref, b2_ref, rt_ref, x2_ref, route_ref, *, sub, seq_rows):
    tiles = [slice(s0, s0 + sub) for s0 in range(0, x_ref.shape[0], sub)]
    cd = wb_ref.dtype

    mixed = []
    for rows in tiles:
        acc = None
        for b, o_ref in enumerate((oa_ref, ob_ref, oc_ref)):
            per_branch = _dot(o_ref[rows], wb_ref[b])
            gate = _sigmoid(gl_ref[rows, D_MODEL * b:D_MODEL * (b + 1)].astype(F32))
            acc = gate * per_branch if acc is None else acc + gate * per_branch
        mixed.append(acc.astype(cd))

    x1 = [_layer_norm(DN_ALPHA * x_ref[rows] + _dot(m, wout_ref[...]), g1_ref[...], b1_ref[...])
          for rows, m in zip(tiles, mixed)]
    q = [(_dot(t.astype(cd), wq_ref[...]) * (X_HEAD_DIM ** -0.5)).astype(cd) for t in x1]

    piece = min(sub, seq_rows)
    heads = [[] for _ in tiles]
    for h in range(X_HEADS):
        sl = slice(X_HEAD_DIM * h, X_HEAD_DIM * (h + 1))
        spans = [(t, r0, (t * sub + r0) // seq_rows) for t in range(len(tiles)) for r0 in range(0, sub, piece)]
        scores = [_dot_nt(q[t][r0:r0 + piece, sl], mk_ref[mem, :, sl]) for t, r0, mem in spans]
        outs = [[] for _ in tiles]
        for (t, r0, mem), s in zip(spans, scores):
            m = jnp.max(s, axis=-1, keepdims=True)
            e = jnp.exp(s - m)
            l = jnp.sum(e, axis=-1, keepdims=True)
            outs[t].append((_dot(e.astype(cd), mv_ref[mem, :, sl]) * (1.0 / l)).astype(cd))
        for t, parts in enumerate(outs):
            heads[t].append(parts[0] if len(parts) == 1 else jnp.concatenate(parts, axis=0))

    attn = [_dot(jnp.concatenate(hs, axis=-1), wo_ref[...]) for hs in heads]
    x2 = [_layer_norm(DN_ALPHA * a + b, g2_ref[...], b2_ref[...]) for a, b in zip(x1, attn)]

    r_hi, r_lo = _split2(rt_ref[...])
    r_both = jnp.concatenate([r_hi, r_lo], axis=0)
    eye_r = lax.broadcasted_iota(jnp.int32, (ROUTE_ROWS, ROUTE_LANES), 0)
    eye_c = lax.broadcasted_iota(jnp.int32, (ROUTE_ROWS, ROUTE_LANES), 1)
    eye = jnp.where(eye_r == eye_c, 1.0, 0.0).astype(BF16)
    logits = []
    for rows, t in zip(tiles, x2):
        x2_ref[rows] = t
        x_hi, x_lo = _split2(t)
        by_hi = _dot_nt(r_both, x_hi)
        logits.append(by_hi[N_EXPERTS:] + _dot_nt(r_hi, x_lo) + by_hi[:N_EXPERTS])
    hi, mid, lo = _split3(_route(logits[0] if len(logits) == 1 else jnp.concatenate(logits, axis=1)))
    for rows in tiles:
        route_ref[rows] = _dot_tn(hi[:, rows], eye) + _dot_tn(mid[:, rows], eye) + _dot_tn(lo[:, rows], eye)


def _mix(x, oa, ob, oc, gl, mk, mv, w, batch, seq):
    n_mem = mk.shape[1]
    if seq >= 2 * MIX_TILE:
        sub, tm, per_tile = MIX_TILE, 2 * MIX_TILE, 1
        mem = pl.BlockSpec((1, n_mem, D_MODEL), lambda b, i: (b, 0, 0))
    else:
        assert MIX_TILE % seq == 0 and batch % (MIX_TILE // seq) == 0
        sub, tm, per_tile = MIX_TILE, MIX_TILE, MIX_TILE // seq
        mem = pl.BlockSpec((per_tile, n_mem, D_MODEL), lambda b, i: (b, 0, 0), pipeline_mode=pl.Buffered(1))
    nt = (per_tile * seq) // tm
    row = lambda b, i: (b * nt + i, 0)
    vec = _const_spec((1, D_MODEL))
    sq = _layer_spec((D_MODEL, D_MODEL), w['layer'])
    return pl.pallas_call(
        functools.partial(_mix_kernel, sub=sub, seq_rows=seq),
        grid=(batch // per_tile, nt),
        in_specs=[pl.BlockSpec((tm, D_MODEL), row)] + [pl.BlockSpec((tm, A_WIDTH), row)] * 3
        + [pl.BlockSpec((tm, N_BRANCH * D_MODEL), row), mem, mem,
           _layer_spec((N_BRANCH, A_WIDTH, D_MODEL), w['layer']), sq, sq, sq, vec, vec, vec, vec,
           _const_spec((N_EXPERTS, D_MODEL))],
        out_specs=[pl.BlockSpec((tm, D_MODEL), row), pl.BlockSpec((tm, ROUTE_LANES), row)],
        out_shape=[jax.ShapeDtypeStruct((batch * seq, D_MODEL), F32),
                   jax.ShapeDtypeStruct((batch * seq, ROUTE_LANES), F32)],
        compiler_params=_params("arbitrary", "arbitrary"),
        name="mix",
    )(x, oa, ob, oc, gl, mk, mv, w['w_branch'], w['w_out'], w['xa_wq'], w['xa_wo'],
      w['ln1_g'], w['ln1_b'], w['ln2_g'], w['ln2_b'], w['router_t'])


def _sc_gather_rows(tables, idx):
    n_out = idx.shape[0]
    per_worker = n_out // SC_WORKERS
    window = min(SC_GATHER_WINDOW, per_worker)
    steps = per_worker // window
    assert per_worker * SC_WORKERS == n_out and steps * window == per_worker and window % 8 == 0
    idx3 = idx.astype(jnp.int32).reshape(SC_WORKERS, steps, window)
    mesh = plsc.VectorSubcoreMesh(core_axis_name="core", subcore_axis_name="subcore")
    n_tab = len(tables)

    def body(*refs):
        tab_hbm = refs[:n_tab]
        idx_hbm = refs[n_tab]
        out_hbm = refs[n_tab + 1:2 * n_tab + 1]
        idx_v = refs[2 * n_tab + 1]
        rows_v = refs[2 * n_tab + 2:3 * n_tab + 2]
        sem = refs[3 * n_tab + 2]
        wid = lax.axis_index("subcore") * SC_CORES + lax.axis_index("core")
        pltpu.sync_copy(idx_hbm.at[wid], idx_v)

        @pl.loop(0, steps)
        def _(j):
            base = wid * per_worker + j * window
            for k in range(n_tab):
                pltpu.async_copy(tab_hbm[k].at[idx_v.at[j]], rows_v[k], sem).wait()
                pltpu.sync_copy(rows_v[k], out_hbm[k].at[pl.ds(base, window)])

    call = pl.kernel(
        body,
        out_type=[jax.ShapeDtypeStruct((n_out, t.shape[1]), t.dtype) for t in tables],
        mesh=mesh,
        scratch_types=[pltpu.VMEM((steps, window), jnp.int32)]
        + [pltpu.VMEM((window, t.shape[1]), t.dtype) for t in tables] + [pltpu.SemaphoreType.DMA],
        name="sc_gather",
    )
    return call(*tables, idx3)


def _dispatch_plan(seg, tm, n_seg):
    n = seg.shape[0]
    n_pad = n + n_seg * tm
    ids = jnp.arange(n_seg, dtype=jnp.int32)
    onehot = (seg[:, None] == ids[None, :]).astype(jnp.int32)
    counts = jnp.sum(onehot, axis=0)
    padded = ((counts + tm - 1) // tm) * tm
    start_p = jnp.cumsum(padded) - padded
    start_u = jnp.cumsum(counts) - counts
    order = jnp.argsort(seg, stable=True).astype(jnp.int32)
    rank_sorted = jnp.argsort(order).astype(jnp.int32)
    pos = (rank_sorted + jnp.sum(onehot * (start_p - start_u)[None, :], axis=1)).astype(jnp.int32)
    gap = padded - counts
    gap_end = jnp.cumsum(gap)
    j = jnp.arange(n_pad - n, dtype=jnp.int32)
    region = jnp.sum((j[:, None] >= gap_end[None, :]).astype(jnp.int32), axis=1)
    oh_region = (region[:, None] == jnp.arange(n_seg + 1, dtype=jnp.int32)[None, :]).astype(jnp.int32)
    first_empty = jnp.concatenate([start_p + counts, jnp.sum(padded)[None]])
    region_start = jnp.concatenate([gap_end - gap, gap_end[-1:]])
    empty_slot = j + jnp.sum(oh_region * (first_empty - region_start)[None, :], axis=1)
    _, src = lax.sort_key_val(jnp.concatenate([pos, empty_slot.astype(jnp.int32)]),
                              jnp.concatenate([jnp.arange(n, dtype=jnp.int32), j % n]))
    tile_start = jnp.arange(n_pad // tm, dtype=jnp.int32) * tm
    seg_tile = jnp.minimum(jnp.sum((tile_start[:, None] >= (start_p + padded)[None, :]).astype(jnp.int32), axis=1),
                           n_seg - 1)
    return pos, src, seg_tile, jnp.sum(padded) // tm


def _moe_kernel(ex_ref, x_ref, r_ref, *refs, n_exp, col0):
    w_refs, (g_ref, b_ref, o_ref) = refs[:3 * n_exp], refs[3 * n_exp:]

    @pl.when(pl.program_id(0) < ex_ref[n_exp * pl.num_programs(0)])
    def _():
        x = x_ref[...]
        cd = w_refs[0].dtype
        xb = x.astype(cd)
        r = r_ref[...]
        y = None
        for e in range(n_exp):
            w1_ref, w3_ref, w2_ref = w_refs[3 * e:3 * e + 3]
            h1 = _dot(xb, w1_ref[...])
            h3 = _dot(xb, w3_ref[...])
            ye = r[:, col0 + e:col0 + e + 1] * _dot((h1 * _sigmoid(h1) * h3).astype(cd), w2_ref[...])
            y = ye if y is None else y + ye
        o_ref[...] = _layer_norm(DN_ALPHA * x + y, g_ref[...], b_ref[...])


def _moe_sorted(xs, rs, experts, tiles_used, w, tm, col0):
    n_pad = xs.shape[0]
    n_exp, nt = experts.shape
    layer = w['layer']
    row = lambda i, ex: (i, 0)
    vec = pl.BlockSpec((1, D_MODEL), lambda i, ex: (0, 0))
    mode = dict(pipeline_mode=pl.Buffered(1)) if w['w1'].dtype == F32 else {}
    w_specs, w_args = [], []
    for e in range(n_exp):
        pick = lambda i, ex, e=e: (layer, ex[e * nt + i], 0, 0)
        w_specs += [pl.BlockSpec((None, None, D_MODEL, D_EXPERT), pick, **mode),
                    pl.BlockSpec((None, None, D_MODEL, D_EXPERT), pick, **mode),
                    pl.BlockSpec((None, None, D_EXPERT, D_MODEL), pick, **mode)]
        w_args += [w['w1'], w['w3'], w['w2']]
    return pl.pallas_call(
        functools.partial(_moe_kernel, n_exp=n_exp, col0=col0),
        grid_spec=pltpu.PrefetchScalarGridSpec(
            num_scalar_prefetch=1,
            grid=(nt,),
            in_specs=[pl.BlockSpec((tm, D_MODEL), row), pl.BlockSpec((tm, ROUTE_LANES), row)] + w_specs + [vec, vec],
            out_specs=pl.BlockSpec((tm, D_MODEL), row),
        ),
        out_shape=jax.ShapeDtypeStruct((n_pad, D_MODEL), F32),
        compiler_params=_params("arbitrary"),
        name="moe",
    )(jnp.concatenate([experts.reshape(n_exp * nt), tiles_used.astype(jnp.int32).reshape(1)]),
      xs, rs, *w_args, w['ln3_g'], w['ln3_b'])


def _moe(x2, route, w):
    n = x2.shape[0]
    n_pairs = len(EXPERT_PAIRS)
    by_pair = n >= N_SEGMENTS * MOE_TILE
    if by_pair:
        tm, n_seg, col0 = MOE_TILE, N_SEGMENTS, 0
        seg = route[:, ROUTE_PAIR_COL].astype(jnp.int32)
    else:
        tm, n_seg, col0 = min(MOE_TILE, n // N_GROUPS), N_GROUPS, ROUTE_PAIR_COL + 1
        seg = route[:, ROUTE_GROUP_COL].astype(jnp.int32)
    pos, src, seg_tile, tiles_used = _dispatch_plan(seg, tm, n_seg)
    if by_pair:
        p = seg_tile % n_pairs
        member = lambda side: functools.reduce(
            lambda x, y: x + y, [jnp.where(p == k, pr[side], 0) for k, pr in enumerate(EXPERT_PAIRS)])
        experts = GROUP_SIZE * (seg_tile // n_pairs) + jnp.stack([member(0), member(1)])
    else:
        experts = GROUP_SIZE * seg_tile[None, :] + jnp.arange(GROUP_SIZE, dtype=jnp.int32)[:, None]
    xs, rs = _sc_gather_rows([x2, route], src)
    ys = _moe_sorted(xs, rs, experts.astype(jnp.int32), tiles_used, w, tm, col0)
    return _sc_gather_rows([ys], pos)[0]


def _trunk_layer(x, w, mem_k, mem_v, cache_k, cache_v, s0, h0, conv0, batch, seq, prompt):
    qkv, hg, rg, gl = _inproj(x, w['w_in'], w['layer'])
    keep = min(A_WINDOW, seq)
    x_keep = x.reshape(batch, seq, D_MODEL)[:, seq - keep:].reshape(batch * keep, D_MODEL)
    k32, v32 = [t.reshape(batch, keep, A_HEADS, A_HEAD_DIM) for t in _kv_rows(x_keep, w['w_in'], w['layer'])]
    if prompt:
        oa = _attn_prompt(qkv, w['band_bias'], batch, seq)
        new_k, new_v = k32, v32
        tile = 256
    else:
        win = cache_k.shape[1]
        ck = cache_k.reshape(batch, win, A_WIDTH)
        cv = cache_v.reshape(batch, win, A_WIDTH)
        oa = _attn_sample(qkv, ck, cv, w['band_bias'][:, :seq, :win + 2 * CHUNK], batch, seq)
        shift = lambda cache, new: (jnp.concatenate([cache[:, seq:], new], axis=1) if seq <= win
                                    else new[:, seq - win:])
        new_k, new_v = shift(cache_k, k32), shift(cache_v, v32)
        tile = seq
    conv0_pad = jnp.concatenate([jnp.zeros((batch, 8 - (C_CONV - 1), C_WIDTH), F32), conv0.astype(F32)], axis=1)
    ob, s_bd, oc, h_new = _recurrences(hg, rg, _state_to_block_diag(s0), conv0_pad,
                                       h0.astype(F32).reshape(batch, 1, C_WIDTH), w, batch, seq, tile, prompt)
    n_tail = min(seq, C_CONV - 1)
    xr_tail = rg.reshape(batch, seq, 2 * C_WIDTH)[:, seq - n_tail:, :C_WIDTH]
    conv_new = jnp.concatenate([conv0.astype(F32), xr_tail], axis=1)[:, n_tail:]
    x2, route = _mix(x, oa, ob, oc, gl, mem_k, mem_v, w, batch, seq)
    x3 = _moe(x2, route, w)
    return x3, (new_k, new_v, _block_diag_to_state(s_bd), h_new.reshape(batch, C_WIDTH), conv_new)


def kernel(x_prompt, x_sample, cache_attn_k, cache_attn_v, state_hgrn, state_rglru, state_conv, cache_mem_k, cache_mem_v, mem_prompt, w_in, attn_rel_bias, hgrn_lb_logits, hgrn_norm_g, rg_conv_w, rg_conv_b, rg_wa, rg_ba, rg_wx, rg_bx, rg_lambda, w_branch, w_out, ln1_g, ln1_b, xa_wq, xa_wk, xa_wv, xa_wo, ln2_g, ln2_b, moe_router, moe_w1, moe_w3, moe_w2, ln3_g, ln3_b):
    bp, tp, _ = x_prompt.shape
    bs, ts, _ = x_sample.shape
    n_mem = mem_prompt.shape[1]
    depth = w_in.shape[0]

    p = jax.nn.softmax(hgrn_lb_logits.astype(F32), axis=0)
    lb_all = jnp.cumsum(p, axis=0) - p[0:1]
    vec = lambda t: t.astype(F32).reshape(1, -1)

    xp = x_prompt.reshape(bp * tp, D_MODEL)
    xs = x_sample.reshape(bs * ts, D_MODEL)
    mem2d = mem_prompt.reshape(bp * n_mem, D_MODEL)
    s0p = jnp.zeros((bp, B_HEADS, B_KEY_DIM, B_VAL_DIM), F32)
    h0p = jnp.zeros((bp, C_WIDTH), F32)
    conv0p = jnp.zeros((bp, C_CONV - 1, C_WIDTH), F32)
    router_t = moe_router.astype(F32).T

    stacked = {'w_in': w_in, 'w_branch': w_branch, 'w_out': w_out, 'xa_wq': xa_wq, 'xa_wo': xa_wo,
               'w1': moe_w1, 'w3': moe_w3, 'w2': moe_w2}
    stacked_fast = {k: v.astype(BF16) for k, v in stacked.items()}
    stacked_precise = {k: v.astype(F32) for k, v in stacked.items()}
    wk_b, wv_b = xa_wk.astype(BF16), xa_wv.astype(BF16)

    outs_p, outs_s, mem_ks, mem_vs = [], [], [], []
    for l in range(depth):
        wf = {'wa_bd': _block_diag_weight(rg_wa[l]), 'wx_bd': _block_diag_weight(rg_wx[l])}
        shared = {
            'layer': l, 'band_bias': _band_bias(attn_rel_bias[l]),
            'lb': vec(lb_all[l]), 'hgrn_g': vec(jnp.tile(hgrn_norm_g[l], B_HEADS)),
            'conv_w': rg_conv_w[l].astype(F32), 'conv_b': vec(rg_conv_b[l]),
            'ba': vec(rg_ba[l]), 'bx': vec(rg_bx[l]), 'lam': vec(rg_lambda[l]),
            'ln1_g': vec(ln1_g[l]), 'ln1_b': vec(ln1_b[l]), 'ln2_g': vec(ln2_g[l]), 'ln2_b': vec(ln2_b[l]),
            'ln3_g': vec(ln3_g[l]), 'ln3_b': vec(ln3_b[l]), 'router_t': router_t,
        }
        w_fast = dict(shared, **stacked_fast, **{k: v.astype(BF16) for k, v in wf.items()})
        w_precise = dict(shared, **stacked_precise, **{k: v.astype(F32) for k, v in wf.items()})
        mk_p = _matmul(mem2d, wk_b, l)
        mv_p = _matmul(mem2d, wv_b, l)
        mem_ks.append(mk_p.reshape(bp, n_mem, X_HEADS, X_HEAD_DIM))
        mem_vs.append(mv_p.reshape(bp, n_mem, X_HEADS, X_HEAD_DIM))
        xp, st_p = _trunk_layer(xp, w_fast, mk_p.reshape(bp, n_mem, D_MODEL).astype(BF16),
                                mv_p.reshape(bp, n_mem, D_MODEL).astype(BF16),
                                None, None, s0p, h0p, conv0p, bp, tp, True)
        xs, st_s = _trunk_layer(xs, w_precise, cache_mem_k[l].reshape(bs, n_mem, D_MODEL).astype(F32),
                                cache_mem_v[l].reshape(bs, n_mem, D_MODEL).astype(F32),
                                cache_attn_k[l], cache_attn_v[l], state_hgrn[l], state_rglru[l], state_conv[l],
                                bs, ts, False)
        outs_p.append(st_p)
        outs_s.append(st_s)

    stack = lambda items, j: jnp.stack([it[j] for it in items])
    return (xp.reshape(bp, tp, D_MODEL), xs.reshape(bs, ts, D_MODEL),
            stack(outs_p, 0), stack(outs_p, 1), stack(outs_p, 2), stack(outs_p, 3), stack(outs_p, 4),
            jnp.stack(mem_ks), jnp.stack(mem_vs),
            stack(outs_s, 0), stack(outs_s, 1), stack(outs_s, 2), stack(outs_s, 3), stack(outs_s, 4))
```
